```python
import math
import jax
import jax.numpy as jnp
from jax import lax
import numpy as np

D_MODEL = 1024
BATCH = 8
SEQ = 4096
DEPTH = 4

GRID_W = 64
CTX_LEN = 256
D_MIX = D_MODEL
HY_W = D_MIX // 4
ATT_W = D_MIX // 2
S5_W = D_MIX - HY_W - ATT_W
HEAD_DIM = 64
N_HEADS = ATT_W // HEAD_DIM
N_KV_HEADS = 2
Q_PER_KV = N_HEADS // N_KV_HEADS
KV_W = N_KV_HEADS * HEAD_DIM
WINDOW = 128
QBLK = 128
KBLK = QBLK + 2 * WINDOW
ROPE_BASE = 10000.0
NEG_INF = -1e30
HY_ORDER = 2
HY_SHORT = 3
HY_BANDS = 16
HY_EMB = 1 + 2 * HY_BANDS
HY_FILTER_HIDDEN = 64
HY_DECAY_MIN = math.log(1e-2) / 1.5
HY_DECAY_MAX = math.log(1e-2) / 0.3
S5_CPG = 16
S5_GROUPS = S5_W // S5_CPG
S5_STATE = 64
S5_DT_MIN = 1e-3
S5_DT_MAX = 1e-1
N_EXPERTS = 16
N_EXPERT_GROUPS = 4
EXPERTS_PER_GROUP = N_EXPERTS // N_EXPERT_GROUPS
TOP_K = 2
D_EXPERT = 1024
MOE_BLK = 128
HY_END = 3 * HY_W
Q_END = HY_END + ATT_W
K_END = Q_END + KV_W
V_END = K_END + KV_W
IN_W = V_END + S5_W
EPS = 1e-6
F32 = jnp.float32

kernel_name = 'hybrid_hyena_swa_s5_moe_dit'


def rms_norm(x, g):
    xf = x.astype(F32)
    y = xf * lax.rsqrt(jnp.mean(xf * xf, axis=-1, keepdims=True) + EPS)
    return (y * g.astype(F32)).astype(x.dtype)


def group_rms_norm(parts, g):
    normed = []
    for p in parts:
        pf = p.astype(F32)
        normed.append(pf * lax.rsqrt(jnp.mean(pf * pf, axis=-1, keepdims=True) + EPS))
    return (jnp.concatenate(normed, axis=-1) * g.astype(F32)).astype(parts[0].dtype)


def modulate(h, shift, scale):
    return h * (1.0 + scale) + shift


def axial_rope(x, rows, cols):
    half = HEAD_DIM // 2
    quarter = half // 2
    inv_freq = ROPE_BASE ** (-jnp.arange(quarter, dtype=F32) / quarter)
    extra = (1,) * (x.ndim - 3)

    def rotate(xa, pos):
        ang = pos.astype(F32)[:, None] * inv_freq[None, :]
        cos = jnp.cos(ang).reshape((1, pos.shape[0]) + extra + (quarter,))
        sin = jnp.sin(ang).reshape((1, pos.shape[0]) + extra + (quarter,))
        x1 = xa[..., :quarter].astype(F32)
        x2 = xa[..., quarter:].astype(F32)
        return jnp.concatenate([x1 * cos - x2 * sin, x1 * sin + x2 * cos], axis=-1)

    out = jnp.concatenate([rotate(x[..., :half], rows), rotate(x[..., half:], cols)], axis=-1)
    return out.astype(x.dtype)


def short_conv(z, w, b):
    L = z.shape[1]
    pad = HY_SHORT // 2
    zp = jnp.pad(z, ((0, 0), (pad, pad), (0, 0)))
    out = b
    for j in range(HY_SHORT):
        out = out + zp[:, j:j + L] * w[j]
    return out


def hyena_filter_spectrum(L, w1, b1, freq1, w2, b2, freq2, w3):
    pos = jnp.arange(L, dtype=F32)
    t = pos / float(max(L - 1, 1))
    bands = jnp.linspace(1e-4, HY_BANDS - 1, HY_BANDS, dtype=F32)
    ang = (2.0 * math.pi / L) * pos[:, None] * bands[None, :]
    feats = jnp.concatenate([t[:, None], jnp.cos(ang), -jnp.sin(ang)], axis=-1)
    hid = jnp.sin(freq1.astype(F32) * (feats @ w1.astype(F32) + b1.astype(F32)))
    hid = jnp.sin(freq2.astype(F32) * (hid @ w2.astype(F32) + b2.astype(F32)))
    taps = (hid @ w3.astype(F32)).reshape(L, HY_ORDER, 2, HY_W)
    decay = jnp.abs(jnp.linspace(HY_DECAY_MIN, HY_DECAY_MAX, HY_W, dtype=F32))
    taps = taps * jnp.exp(-t[:, None, None, None] * decay)
    fwd = taps[:, :, 0]
    bwd = taps[:, :, 1]
    kern = jnp.concatenate([fwd, jnp.zeros((1, HY_ORDER, HY_W), F32), bwd[:0:-1]], axis=0)
    kern = kern / jnp.sum(jnp.abs(kern), axis=0, keepdims=True)
    return jnp.fft.rfft(kern, axis=0)


def fft_long_conv(u, spec, bias):
    L = u.shape[1]
    uf = jnp.fft.rfft(u.astype(F32), n=2 * L, axis=1)
    y = jnp.fft.irfft(uf * spec[None], n=2 * L, axis=1)[:, :L]
    return (y + u.astype(F32) * bias.astype(F32)).astype(u.dtype)


def hyena_mix(z, conv_w, conv_b, w1, b1, freq1, w2, b2, freq2, w3, bias):
    L = z.shape[1]
    z = short_conv(z, conv_w, conv_b)
    v, x1, x2 = jnp.split(z, 3, axis=-1)
    spec = hyena_filter_spectrum(L, w1, b1, freq1, w2, b2, freq2, w3)
    z1 = x1 * fft_long_conv(v, spec[:, 0], bias[0])
    return x2 * fft_long_conv(z1, spec[:, 1], bias[1])


def banded_attention(q, k, v, k_c, v_c, sink):
    B, N = q.shape[0], q.shape[1]
    C = k_c.shape[1]
    scale = HEAD_DIM ** -0.5
    kp = jnp.pad(k, ((0, 0), (WINDOW, WINDOW), (0, 0), (0, 0)))
    vp = jnp.pad(v, ((0, 0), (WINDOW, WINDOW), (0, 0), (0, 0)))
    sink_cols = jnp.broadcast_to(sink[None, :, :, None, None], (B, N_KV_HEADS, Q_PER_KV, QBLK, 1))

    def block(n):
        start = n * QBLK
        qb = lax.dynamic_slice_in_dim(q, start, QBLK, axis=1)
        kb = lax.dynamic_slice_in_dim(kp, start, KBLK, axis=1)
        vb = lax.dynamic_slice_in_dim(vp, start, KBLK, axis=1)
        s_loc = jnp.einsum('bqkgd,bskd->bkgqs', qb, kb).astype(F32) * scale
        s_ctx = jnp.einsum('bqkgd,bckd->bkgqc', qb, k_c).astype(F32) * scale
        q_pos = start + jnp.arange(QBLK)
        k_pos = start - WINDOW + jnp.arange(KBLK)
        valid = (jnp.abs(k_pos[None, :] - q_pos[:, None]) <= WINDOW) & (k_pos >= 0)[None, :] & (k_pos < N)[None, :]
        s_loc = jnp.where(valid, s_loc, NEG_INF)
        p = jax.nn.softmax(jnp.concatenate([s_loc, s_ctx, sink_cols], axis=-1), axis=-1)
        p_loc = p[..., :KBLK].astype(v.dtype)
        p_ctx = p[..., KBLK:KBLK + C].astype(v.dtype)
        return (jnp.einsum('bkgqs,bskd->bqkgd', p_loc, vb)
                + jnp.einsum('bkgqc,bckd->bqkgd', p_ctx, v_c))

    out = lax.map(block, jnp.arange(N // QBLK))
    return jnp.moveaxis(out, 0, 1).reshape(B, N, ATT_W)


def context_attention(q_c, k_c, v_c, sink):
    B, C = q_c.shape[0], q_c.shape[1]
    s = jnp.einsum('bqkgd,bckd->bkgqc', q_c, k_c).astype(F32) * (HEAD_DIM ** -0.5)
    sink_cols = jnp.broadcast_to(sink[None, :, :, None, None], s.shape[:-1] + (1,))
    p = jax.nn.softmax(jnp.concatenate([s, sink_cols], axis=-1), axis=-1)[..., :C].astype(v_c.dtype)
    return jnp.einsum('bkgqc,bckd->bqkgd', p, v_c).reshape(B, C, ATT_W)


def s5_discretize(lam_re, lam_im, log_dt, b_re, b_im):
    lam_re = lam_re.astype(F32)
    lam_im = lam_im.astype(F32)
    b_re = b_re.astype(F32)
    b_im = b_im.astype(F32)
    dt = jnp.exp(log_dt.astype(F32))[:, None]
    mag = jnp.exp(lam_re * dt)
    ab_re = mag * jnp.cos(lam_im * dt)
    ab_im = mag * jnp.sin(lam_im * dt)
    num_re = ab_re - 1.0
    num_im = ab_im
    den = lam_re * lam_re + lam_im * lam_im
    co_re = ((num_re * lam_re + num_im * lam_im) / den)[..., None]
    co_im = ((num_im * lam_re - num_re * lam_im) / den)[..., None]
    bb_re = co_re * b_re - co_im * b_im
    bb_im = co_re * b_im + co_im * b_re
    return ab_re, ab_im, bb_re, bb_im


def s5_combine(e1, e2):
    a1r, a1i, b1r, b1i = e1
    a2r, a2i, b2r, b2i = e2
    return (a1r * a2r - a1i * a2i,
            a1r * a2i + a1i * a2r,
            a2r * b1r - a2i * b1i + b2r,
            a2r * b1i + a2i * b1r + b2i)


def s5_scan(u, ab_re, ab_im, bb_re, bb_im, h0_re, h0_im):
    L = u.shape[0]
    bu_re = jnp.einsum('lbgc,gpc->lbgp', u, bb_re)
    bu_im = jnp.einsum('lbgc,gpc->lbgp', u, bb_im)
    bu_re = bu_re.at[0].add(ab_re * h0_re - ab_im * h0_im)
    bu_im = bu_im.at[0].add(ab_re * h0_im + ab_im * h0_re)
    a_re = jnp.broadcast_to(ab_re[None, None], (L, 1, S5_GROUPS, S5_STATE))
    a_im = jnp.broadcast_to(ab_im[None, None], (L, 1, S5_GROUPS, S5_STATE))
    _, _, s_re, s_im = lax.associative_scan(s5_combine, (a_re, a_im, bu_re, bu_im), axis=0)
    return s_re, s_im


def s5_readout(s_re, s_im, c_re, c_im):
    return (jnp.einsum('lbgp,gcp->lbgc', s_re, c_re.astype(F32))
            - jnp.einsum('lbgp,gcp->lbgc', s_im, c_im.astype(F32)))


def s5_output(y, u, d_skip, glu_w, glu_b):
    Lq, B = y.shape[0], y.shape[1]
    y = y + u * d_skip.astype(F32).reshape(S5_GROUPS, S5_CPG)
    y = jnp.swapaxes(y, 0, 1).reshape(B, Lq, S5_W)
    g = jax.nn.gelu(y)
    return g * jax.nn.sigmoid(g @ glu_w.astype(F32) + glu_b.astype(F32))


def flip_time(a, rev):
    return a[::-1] if rev else a


def s5_mix(u_ctx, u_lat, lam_re, lam_im, log_dt, b_re, b_im, c_re, c_im, d_skip, glu_w, glu_b, with_ctx):
    B = u_lat.shape[0]

    def time_major(u):
        return jnp.swapaxes(u.astype(F32).reshape(B, u.shape[1], S5_GROUPS, S5_CPG), 0, 1)

    uc = time_major(u_ctx)
    ul = time_major(u_lat)
    zero = jnp.zeros((B, S5_GROUPS, S5_STATE), F32)
    y_lat = 0.0
    y_ctx = 0.0
    for d in range(2):
        rev = d == 1
        ab_re, ab_im, bb_re, bb_im = s5_discretize(lam_re[d], lam_im[d], log_dt[d], b_re[d], b_im[d])
        sc_re, sc_im = s5_scan(flip_time(uc, rev), ab_re, ab_im, bb_re, bb_im, zero, zero)
        sl_re, sl_im = s5_scan(flip_time(ul, rev), ab_re, ab_im, bb_re, bb_im, sc_re[-1], sc_im[-1])
        y_lat = y_lat + flip_time(s5_readout(sl_re, sl_im, c_re[d], c_im[d]), rev)
        if with_ctx:
            y_ctx = y_ctx + flip_time(s5_readout(sc_re, sc_im, c_re[d], c_im[d]), rev)
    lat = s5_output(y_lat, ul, d_skip, glu_w, glu_b).astype(u_lat.dtype)
    ctx_out = s5_output(y_ctx, uc, d_skip, glu_w, glu_b).astype(u_ctx.dtype) if with_ctx else None
    return lat, ctx_out


def hybrid_mixer(h, hc, w_in_l, conv_w, conv_b, f_w1, f_b1, f_freq1, f_w2, f_b2, f_freq2, f_w3, hy_bias_l,
                 sink, lam_re, lam_im, log_dt, b_re, b_im, c_re, c_im, d_skip, glu_w, glu_b, out_g,
                 rows, cols, with_ctx):
    def heads_q(a):
        return a.reshape(a.shape[0], a.shape[1], N_KV_HEADS, Q_PER_KV, HEAD_DIM)

    def heads_kv(a):
        return a.reshape(a.shape[0], a.shape[1], N_KV_HEADS, HEAD_DIM)

    p = h @ w_in_l
    if with_ctx:
        pc = hc @ w_in_l
        kvs_c = pc[..., Q_END:]
    else:
        kvs_c = hc @ w_in_l[:, Q_END:]
    k_c = heads_kv(kvs_c[..., :KV_W])
    v_c = heads_kv(kvs_c[..., KV_W:2 * KV_W])
    u_c = kvs_c[..., 2 * KV_W:]
    sink_kg = sink.astype(F32).reshape(N_KV_HEADS, Q_PER_KV)

    hy = hyena_mix(p[..., :HY_END], conv_w, conv_b, f_w1, f_b1, f_freq1, f_w2, f_b2, f_freq2, f_w3, hy_bias_l)
    q = axial_rope(heads_q(p[..., HY_END:Q_END]), rows, cols)
    k = axial_rope(heads_kv(p[..., Q_END:K_END]), rows, cols)
    v = heads_kv(p[..., K_END:V_END])
    att = banded_attention(q, k, v, k_c, v_c, sink_kg)
    s5_lat, s5_ctx = s5_mix(u_c, p[..., V_END:], lam_re, lam_im, log_dt, b_re, b_im, c_re, c_im,
                            d_skip, glu_w, glu_b, with_ctx)
    lat = group_rms_norm([hy, att, s5_lat], out_g)
    if not with_ctx:
        return lat, None
    hy_c = hyena_mix(pc[..., :HY_END], conv_w, conv_b, f_w1, f_b1, f_freq1, f_w2, f_b2, f_freq2, f_w3, hy_bias_l)
    att_c = context_attention(heads_q(pc[..., HY_END:Q_END]), k_c, v_c, sink_kg)
    ctx_out = group_rms_norm([hy_c, att_c, s5_ctx], out_g)
    return lat, ctx_out


def moe_ffn(h, router_w, router_b, w_gate, w_up, w_down):
    T, D = h.shape
    probs = jax.nn.softmax(jnp.dot(h.astype(F32), router_w.astype(F32)), axis=-1)
    sel = (probs + router_b.astype(F32)).reshape(T, N_EXPERT_GROUPS, EXPERTS_PER_GROUP)
    group_score = jnp.sum(lax.top_k(sel, 2)[0], axis=-1)
    g_idx = jnp.argmax(group_score, axis=-1)
    in_group = sel[jnp.arange(T), g_idx]
    _, local = lax.top_k(in_group, TOP_K)
    e_idx = g_idx[:, None] * EXPERTS_PER_GROUP + local
    gates = jnp.take_along_axis(probs, e_idx, axis=1)
    gates = gates / jnp.sum(gates, axis=-1, keepdims=True)
    M = T * TOP_K
    flat_e = e_idx.reshape(-1)
    flat_t = jnp.repeat(jnp.arange(T), TOP_K)
    flat_w = gates.reshape(-1)
    order = jnp.argsort(flat_e)
    se, st, sw = flat_e[order], flat_t[order], flat_w[order]
    counts = jnp.bincount(flat_e, length=N_EXPERTS)
    starts = jnp.cumsum(counts) - counts
    padded = (counts + MOE_BLK - 1) // MOE_BLK * MOE_BLK
    ends = jnp.cumsum(padded)
    pstarts = ends - padded
    dest = pstarts[se] + (jnp.arange(M) - starts[se])
    n_blocks = -(-M // MOE_BLK) + N_EXPERTS
    row_tok = jnp.zeros((n_blocks * MOE_BLK,), jnp.int32).at[dest].set(st)
    row_w = jnp.zeros((n_blocks * MOE_BLK,), F32).at[dest].set(sw)
    block_e = jnp.clip(jnp.searchsorted(ends, jnp.arange(n_blocks) * MOE_BLK, side='right'), 0, N_EXPERTS - 1)
    xb = h[row_tok].reshape(n_blocks, MOE_BLK, D)

    def expert_block(args):
        xblk, e = args
        a = xblk @ w_gate[e]
        u = xblk @ w_up[e]
        return (jax.nn.silu(a) * u) @ w_down[e]

    yb = lax.map(expert_block, (xb, block_e)).reshape(n_blocks * MOE_BLK, D)
    yb = yb * row_w[:, None].astype(yb.dtype)
    return jnp.zeros_like(h).at[row_tok].add(yb)


def setup_inputs(seed: int = 0) -> dict:
    key = jax.random.key(seed)
    ks = iter(jax.random.split(key, 40))

    def nrm(shape, s):
        return jax.random.normal(next(ks), shape, F32) * s

    L = DEPTH
    lam_im_base = jnp.broadcast_to(math.pi * jnp.arange(S5_STATE, dtype=F32), (L, 2, S5_GROUPS, S5_STATE))
    return {
        'x': nrm((BATCH, SEQ, D_MODEL), 1.0),
        'c': nrm((BATCH, D_MODEL), 1.0),
        'ctx': nrm((BATCH, CTX_LEN, D_MODEL), 1.0),
        'c_ctx': nrm((D_MODEL,), 1.0),
        'norm1_g': 1.0 + nrm((L, D_MODEL), 0.02),
        'norm2_g': 1.0 + nrm((L, D_MODEL), 0.02),
        'ada_w': nrm((L, D_MODEL, 6 * D_MODEL), 0.01),
        'ada_b': nrm((L, 6 * D_MODEL), 0.02),
        'w_in': nrm((L, D_MODEL, IN_W), D_MODEL ** -0.5),
        'w_out': nrm((L, D_MIX, D_MODEL), D_MIX ** -0.5),
        'mix_norm_g': 1.0 + nrm((L, D_MIX), 0.02),
        'hy_conv_w': nrm((L, HY_SHORT, 3 * HY_W), HY_SHORT ** -0.5),
        'hy_conv_b': nrm((L, 3 * HY_W), 0.02),
        'hy_f_w1': nrm((L, HY_EMB, HY_FILTER_HIDDEN), HY_EMB ** -0.5),
        'hy_f_b1': nrm((L, HY_FILTER_HIDDEN), 0.1),
        'hy_f_freq1': 1.0 + nrm((L, HY_FILTER_HIDDEN), 0.02),
        'hy_f_w2': nrm((L, HY_FILTER_HIDDEN, HY_FILTER_HIDDEN), HY_FILTER_HIDDEN ** -0.5),
        'hy_f_b2': nrm((L, HY_FILTER_HIDDEN), 0.1),
        'hy_f_freq2': 1.0 + nrm((L, HY_FILTER_HIDDEN), 0.02),
        'hy_f_w3': nrm((L, HY_FILTER_HIDDEN, HY_ORDER * 2 * HY_W), HY_FILTER_HIDDEN ** -0.5),
        'hy_bias': nrm((L, HY_ORDER, HY_W), 1.0),
        'attn_sink': nrm((L, N_HEADS), 0.5),
        's5_lam_re': -0.5 + nrm((L, 2, S5_GROUPS, S5_STATE), 0.01),
        's5_lam_im': lam_im_base + nrm((L, 2, S5_GROUPS, S5_STATE), 0.01),
        's5_log_dt': jax.random.uniform(next(ks), (L, 2, S5_GROUPS), F32,
                                        minval=math.log(S5_DT_MIN), maxval=math.log(S5_DT_MAX)),
        's5_b_re': nrm((L, 2, S5_GROUPS, S5_STATE, S5_CPG), (2 * S5_CPG) ** -0.5),
        's5_b_im': nrm((L, 2, S5_GROUPS, S5_STATE, S5_CPG), (2 * S5_CPG) ** -0.5),
        's5_c_re': nrm((L, 2, S5_GROUPS, S5_CPG, S5_STATE), (2 * S5_STATE) ** -0.5),
        's5_c_im': nrm((L, 2, S5_GROUPS, S5_CPG, S5_STATE), (2 * S5_STATE) ** -0.5),
        's5_d': nrm((L, S5_W), 1.0),
        's5_glu_w': nrm((L, S5_W, S5_W), S5_W ** -0.5),
        's5_glu_b': nrm((L, S5_W), 0.02),
        'router_w': nrm((D_MODEL, N_EXPERTS), D_MODEL ** -0.5),
        'router_b': nrm((N_EXPERTS,), 0.01),
        'moe_w_gate': nrm((L, N_EXPERTS, D_MODEL, D_EXPERT), D_MODEL ** -0.5),
        'moe_w_up': nrm((L, N_EXPERTS, D_MODEL, D_EXPERT), D_MODEL ** -0.5),
        'moe_w_down': nrm((L, N_EXPERTS, D_EXPERT, D_MODEL), D_EXPERT ** -0.5),
        'final_g': 1.0 + nrm((D_MODEL,), 0.02),
    }


def reference(x, c, ctx, c_ctx, norm1_g, norm2_g, ada_w, ada_b, w_in, w_out, mix_norm_g,
              hy_conv_w, hy_conv_b, hy_f_w1, hy_f_b1, hy_f_freq1, hy_f_w2, hy_f_b2, hy_f_freq2, hy_f_w3, hy_bias,
              attn_sink, s5_lam_re, s5_lam_im, s5_log_dt, s5_b_re, s5_b_im, s5_c_re, s5_c_im, s5_d,
              s5_glu_w, s5_glu_b, router_w, router_b, moe_w_gate, moe_w_up, moe_w_down, final_g):
    B, N, D = x.shape
    C = ctx.shape[1]
    ROWS = N // GRID_W
    rows = jnp.repeat(jnp.arange(ROWS), GRID_W)
    cols = jnp.tile(jnp.arange(GRID_W), ROWS)
    xc = ctx
    for l in range(DEPTH):
        with_ctx = l < DEPTH - 1
        mod = jax.nn.silu(c) @ ada_w[l] + ada_b[l]
        mod_c = jax.nn.silu(c_ctx) @ ada_w[l] + ada_b[l]
        sh1, sc1, g1, sh2, sc2, g2 = jnp.split(mod[:, None, :], 6, axis=-1)
        csh1, csc1, cg1, csh2, csc2, cg2 = jnp.split(mod_c, 6)
        h = modulate(rms_norm(x, norm1_g[l]), sh1, sc1)
        hc = modulate(rms_norm(xc, norm1_g[l]), csh1, csc1)
        mix, mix_c = hybrid_mixer(h, hc, w_in[l], hy_conv_w[l], hy_conv_b[l], hy_f_w1[l], hy_f_b1[l],
                                  hy_f_freq1[l], hy_f_w2[l], hy_f_b2[l], hy_f_freq2[l], hy_f_w3[l], hy_bias[l],
                                  attn_sink[l], s5_lam_re[l], s5_lam_im[l], s5_log_dt[l], s5_b_re[l], s5_b_im[l],
                                  s5_c_re[l], s5_c_im[l], s5_d[l], s5_glu_w[l], s5_glu_b[l], mix_norm_g[l],
                                  rows, cols, with_ctx)
        x = x + g1 * (mix @ w_out[l])
        h2 = modulate(rms_norm(x, norm2_g[l]), sh2, sc2)
        if with_ctx:
            xc = xc + cg1 * (mix_c @ w_out[l])
            hc2 = modulate(rms_norm(xc, norm2_g[l]), csh2, csc2)
            tokens = jnp.concatenate([hc2.reshape(B * C, D), h2.reshape(B * N, D)], axis=0)
            y = moe_ffn(tokens, router_w, router_b, moe_w_gate[l], moe_w_up[l], moe_w_down[l])
            xc = xc + cg2 * y[:B * C].reshape(B, C, D)
            x = x + g2 * y[B * C:].reshape(B, N, D)
        else:
            y = moe_ffn(h2.reshape(B * N, D), router_w, router_b, moe_w_gate[l], moe_w_up[l], moe_w_down[l])
            x = x + g2 * y.reshape(B, N, D)
    return rms_norm(x, final_g)
```

```python
import functools
import math

import jax
import jax.numpy as jnp
from jax import lax
from jax.experimental import pallas as pl
from jax.experimental.pallas import tpu as pltpu

F32 = jnp.float32
BF16 = jnp.bfloat16
HIGHEST = lax.Precision.HIGHEST

D_MODEL = 1024
GRID_W = 64
HY_W = 256
ATT_W = 512
S5_W = 256
HEAD_DIM = 64
N_HEADS = 8
N_KV_HEADS = 2
Q_PER_KV = 4
KV_W = 128
WINDOW = 128
QBLK = 128
ROPE_BASE = 10000.0
NEG_INF = -1e30
HY_BANDS = 16
HY_DECAY_MIN = math.log(1e-2) / 1.5
HY_DECAY_MAX = math.log(1e-2) / 0.3
S5_CPG = 16
S5_GROUPS = 16
S5_STATE = 64
N_EXPERTS = 16
N_EXPERT_GROUPS = 4
EXPERTS_PER_GROUP = 4
HY_END = 768
Q_END = 1280
K_END = 1408
V_END = 1536
IN_W = 1792
EPS = 1e-6

TM = 256
MOE_ROWS = 256
S5_TC = 64
DFT_P = 512
VMEM_LIMIT = 56 * 1024 * 1024


def _cparams(*sem):
    return pltpu.CompilerParams(dimension_semantics=sem, vmem_limit_bytes=VMEM_LIMIT)


def _dot(a, b):
    return jnp.dot(a, b, preferred_element_type=F32)


def _mod_kernel(c_ref, w_ref, b_ref, o_ref):
    c = c_ref[...]
    a = c * jax.nn.sigmoid(c)
    o_ref[...] = _dot(a.astype(BF16), w_ref[...].astype(BF16)) + b_ref[...]


def _modulation(cc, ada_w, ada_b):
    depth, d, w6 = ada_w.shape
    nb = 1536
    return pl.pallas_call(
        _mod_kernel,
        grid=(depth, w6 // nb),
        in_specs=[
            pl.BlockSpec((16, d), lambda l, j: (0, 0)),
            pl.BlockSpec((None, d, nb), lambda l, j: (l, 0, j)),
            pl.BlockSpec((None, 1, nb), lambda l, j: (l, 0, j)),
        ],
        out_specs=pl.BlockSpec((None, 16, nb), lambda l, j: (l, 0, j)),
        out_shape=jax.ShapeDtypeStruct((depth, 16, w6), F32),
        compiler_params=_cparams("arbitrary", "arbitrary"),
        name="adaln_mod",
    )(cc, ada_w, ada_b.reshape(depth, 1, w6))


def _rms(x, g):
    return x * lax.rsqrt(jnp.mean(x * x, axis=-1, keepdims=True) + EPS) * g


def _in_kernel(x_ref, g_ref, sh_ref, sc_ref, w_ref, cos_ref, sin_ref,
               hy_ref, q_ref, k_ref, v_ref, u_ref):
    h = _rms(x_ref[...], g_ref[...]) * (1.0 + sc_ref[...]) + sh_ref[...]
    p = _dot(h.astype(BF16), w_ref[...])
    hy_ref[...] = p[:, :HY_END]
    cos = cos_ref[...]
    sin = sin_ref[...]
    lane = lax.broadcasted_iota(jnp.int32, cos.shape, 1)
    first = (lane % 32) < 16

    def rope(z):
        swapped = jnp.where(first, pltpu.roll(z, 112, 1), pltpu.roll(z, 16, 1))
        return z * cos + swapped * sin

    for j in range(ATT_W // 128):
        q_ref[:, j * 128:(j + 1) * 128] = rope(p[:, HY_END + j * 128:HY_END + (j + 1) * 128]).astype(BF16)
    k_ref[...] = rope(p[:, Q_END:K_END]).astype(BF16)
    v_ref[...] = p[:, K_END:V_END].astype(BF16)
    u_ref[...] = p[:, V_END:]


def _rope_tables(n):
    quarter = HEAD_DIM // 4
    inv_freq = ROPE_BASE ** (-jnp.arange(quarter, dtype=F32) / quarter)
    t = jnp.arange(n)
    rows = (t // GRID_W).astype(F32)
    cols = (t % GRID_W).astype(F32)
    ang_r = rows[:, None] * inv_freq[None, :]
    ang_c = cols[:, None] * inv_freq[None, :]
    cos64 = jnp.concatenate([jnp.cos(ang_r)] * 2 + [jnp.cos(ang_c)] * 2, axis=-1)
    sin64 = jnp.concatenate([-jnp.sin(ang_r), jnp.sin(ang_r), -jnp.sin(ang_c), jnp.sin(ang_c)], axis=-1)
    cos = jnp.concatenate([jnp.tile(cos64, (1, 2)), jnp.ones((TM, 128), F32)], axis=0)
    sin = jnp.concatenate([jnp.tile(sin64, (1, 2)), jnp.zeros((TM, 128), F32)], axis=0)
    return cos, sin


def _input_proj(x_all, g, mod4, w_in_bf, cos, sin, nb, n):
    t_all = x_all.shape[0]
    n_lat_tiles = nb * n // TM
    tpb = n // TM

    def mrow(i):
        return jnp.where(i < n_lat_tiles, i // tpb, nb)

    def trow(i):
        return jnp.where(i < n_lat_tiles, i % tpb, tpb)

    row = lambda w: pl.BlockSpec((TM, w), lambda i: (i, 0))
    return pl.pallas_call(
        _in_kernel,
        grid=(t_all // TM,),
        in_specs=[
            row(D_MODEL),
            pl.BlockSpec((1, D_MODEL), lambda i: (0, 0)),
            pl.BlockSpec((None, None, 1, D_MODEL), lambda i: (mrow(i), 0, 0, 0)),
            pl.BlockSpec((None, None, 1, D_MODEL), lambda i: (mrow(i), 1, 0, 0)),
            pl.BlockSpec((D_MODEL, IN_W), lambda i: (0, 0)),
            pl.BlockSpec((TM, 128), lambda i: (trow(i), 0)),
            pl.BlockSpec((TM, 128), lambda i: (trow(i), 0)),
        ],
        out_specs=[row(HY_END), row(ATT_W), row(KV_W), row(KV_W), row(S5_W)],
        out_shape=[
            jax.ShapeDtypeStruct((t_all, HY_END), F32),
            jax.ShapeDtypeStruct((t_all, ATT_W), BF16),
            jax.ShapeDtypeStruct((t_all, KV_W), BF16),
            jax.ShapeDtypeStruct((t_all, KV_W), BF16),
            jax.ShapeDtypeStruct((t_all, S5_W), F32),
        ],
        compiler_params=_cparams("arbitrary"),
        name="norm_mod_inproj",
    )(x_all, g.reshape(1, D_MODEL), mod4, mod4, w_in_bf, cos, sin)


def _sconv_kernel(z_ref, w_ref, b_ref, o_ref):
    z = z_ref[...]
    length = z.shape[0]
    row = lax.broadcasted_iota(jnp.int32, z.shape, 0)
    zm = jnp.where(row == 0, 0.0, pltpu.roll(z, 1, 0))
    zp = jnp.where(row == length - 1, 0.0, pltpu.roll(z, length - 1, 0))
    o_ref[...] = b_ref[...] + zm * w_ref[0:1, :] + z * w_ref[1:2, :] + zp * w_ref[2:3, :]


def _short_conv(hy, conv_w, conv_b, nb, n, c):
    t_all = hy.shape[0]
    out = None
    for length, off in ((n, 0), (c, nb * n // c)):
        kwargs = {}
        args = [hy, conv_w, conv_b.reshape(1, HY_END)]
        in_specs = [
            pl.BlockSpec((length, 256), lambda b, j, off=off: (off + b, j)),
            pl.BlockSpec((3, 256), lambda b, j: (0, j)),
            pl.BlockSpec((1, 256), lambda b, j: (0, j)),
        ]
        kern = _sconv_kernel
        if out is not None:
            args.append(out)
            in_specs.append(pl.BlockSpec(memory_space=pl.ANY))
            kwargs["input_output_aliases"] = {3: 0}
            kern = lambda z, w, b, prev, o: _sconv_kernel(z, w, b, o)
        out = pl.pallas_call(
            kern,
            grid=(nb, 3),
            in_specs=in_specs,
            out_specs=pl.BlockSpec((length, 256), lambda b, j, off=off: (off + b, j)),
            out_shape=jax.ShapeDtypeStruct((t_all, HY_END), F32),
            compiler_params=_cparams("arbitrary", "arbitrary"),
            name="hyena_short_conv",
            **kwargs,
        )(*args)
    return out


def _dft_plan(length):
    p = min(DFT_P, length)
    return p, 2 * length // p


def _dft_tables(length):
    p, na = _dft_plan(length)
    m = 2 * length
    a = jnp.arange(na, dtype=jnp.int32)
    ang_w = (2.0 * math.pi / na) * ((a[:, None] * a[None, :]) % na).astype(F32)
    twr, twi = jnp.cos(ang_w), -jnp.sin(ang_w)
    r = jnp.arange(p, dtype=jnp.int32)
    k = a[:, None, None] + na * r[None, :, None]
    ang = (2.0 * math.pi / m) * ((k * r[None, None, :]) % m).astype(F32)
    er, ei = jnp.cos(ang), -jnp.sin(ang)
    return twr, twi, er, ei


def _filter_kernel(twr_ref, twi_ref, w1_ref, b1_ref, f1_ref, w2_ref, b2_ref, f2_ref, w3_ref,
                   bands_ref, decay_ref, er_ref, ei_ref, h_ref, kern_ref, norm_ref, *, length, p, na):
    ka = pl.program_id(0)
    m = 2 * length
    hdot = functools.partial(jnp.dot, precision=HIGHEST, preferred_element_type=F32)

    @pl.when(ka == 0)
    def _():
        norm = jnp.zeros((1, 2 * HY_W), F32)
        for a in range(na):
            n_idx = a * p + lax.broadcasted_iota(jnp.int32, (p, 1), 0)
            is_fwd = n_idx < length
            pos = jnp.where(is_fwd, n_idx, m - n_idx).astype(F32)
            t = pos / float(max(length - 1, 1))
            ang = (2.0 * math.pi / length) * pos * bands_ref[...]
            pre = (t * w1_ref[0:1, :] + hdot(jnp.cos(ang), w1_ref[1:1 + HY_BANDS, :])
                   + hdot(-jnp.sin(ang), w1_ref[1 + HY_BANDS:, :]) + b1_ref[...])
            hid = jnp.sin(f1_ref[...] * pre)
            hid = jnp.sin(f2_ref[...] * (hdot(hid, w2_ref[...]) + b2_ref[...]))
            taps = hdot(hid, w3_ref[...])
            wnd = jnp.exp(-t * decay_ref[...])
            live = n_idx != length
            for o in range(2):
                fwd = taps[:, o * 2 * HY_W:o * 2 * HY_W + HY_W]
                bwd = taps[:, o * 2 * HY_W + HY_W:(o + 1) * 2 * HY_W]
                kern = jnp.where(live, jnp.where(is_fwd, fwd, bwd) * wnd, 0.0)
                kern_ref[a, :, o * HY_W:(o + 1) * HY_W] = kern
            norm = norm + jnp.sum(jnp.abs(kern_ref[a]), axis=0, keepdims=True)
        norm_ref[...] = norm

    gr = jnp.zeros((p, 2 * HY_W), F32)
    gi = jnp.zeros((p, 2 * HY_W), F32)
    for a in range(na):
        slab = kern_ref[a]
        gr = gr + twr_ref[ka, a] * slab
        gi = gi + twi_ref[ka, a] * slab
    er = er_ref[...]
    ei = ei_ref[...]
    inv = 1.0 / norm_ref[...]
    h_ref[0] = (hdot(er, gr) - hdot(ei, gi)) * inv
    h_ref[1] = (hdot(ei, gr) + hdot(er, gi)) * inv


def _filter_spectrum(length, tables, w1, b1, f1, w2, b2, f2, w3):
    p, na = _dft_plan(length)
    twr, twi, er, ei = tables
    bands = jnp.linspace(1e-4, HY_BANDS - 1, HY_BANDS, dtype=F32).reshape(1, HY_BANDS)
    decay = jnp.abs(jnp.linspace(HY_DECAY_MIN, HY_DECAY_MAX, HY_W, dtype=F32)).reshape(1, HY_W)
    full = lambda arr: pl.BlockSpec(arr.shape, lambda ka, *_: (0,) * arr.ndim)
    vec = lambda v: v.reshape(1, -1)
    ins = [w1, vec(b1), vec(f1), w2, vec(b2), vec(f2), w3, bands, decay]
    return pl.pallas_call(
        functools.partial(_filter_kernel, length=length, p=p, na=na),
        grid_spec=pltpu.PrefetchScalarGridSpec(
            num_scalar_prefetch=2,
            grid=(na,),
            in_specs=[full(x) for x in ins] + [
                pl.BlockSpec((None, p, p), lambda ka, *_: (ka, 0, 0)),
                pl.BlockSpec((None, p, p), lambda ka, *_: (ka, 0, 0)),
            ],
            out_specs=pl.BlockSpec((None, 2, p, 2 * HY_W), lambda ka, *_: (ka, 0, 0, 0)),
            scratch_shapes=[pltpu.VMEM((na, p, 2 * HY_W), F32), pltpu.VMEM((1, 2 * HY_W), F32)],
        ),
        out_shape=jax.ShapeDtypeStruct((na, 2, p, 2 * HY_W), F32),
        compiler_params=_cparams("arbitrary"),
        name="hyena_filter_spectrum",
    )(twr, twi, *[vec(x) if x.ndim == 1 else x for x in ins], er, ei)


def _conv_kernel(twr_ref, twi_ref, u_ref, gate_ref, bias_ref, er_ref, ei_ref, ert_ref, eit_ref, h_ref,
                 *rest, length, p, na):
    o_ref = rest[-1]
    ka = pl.program_id(1)
    nz = length // p
    w = o_ref.shape[-1]
    gr = jnp.zeros((p, w), F32)
    gi = jnp.zeros((p, w), F32)
    for a in range(nz):
        wr = twr_ref[ka, a]
        wi = twi_ref[ka, a]
        zr = u_ref[a * p:(a + 1) * p, :]
        zi = u_ref[length + a * p:length + (a + 1) * p, :]
        gr = gr + (wr * zr - wi * zi)
        gi = gi + (wr * zi + wi * zr)
    grb = gr.astype(BF16)
    gib = gi.astype(BF16)
    er = er_ref[...]
    ei = ei_ref[...]
    sr = _dot(er, grb) - _dot(ei, gib)
    si = _dot(ei, grb) + _dot(er, gib)
    hr = h_ref[0]
    hi = h_ref[1]
    yr = (sr * hr - si * hi).astype(BF16)
    yi = (sr * hi + si * hr).astype(BF16)
    ert = ert_ref[...]
    eit = eit_ref[...]
    vr = _dot(ert, yr) + _dot(eit, yi)
    vi = _dot(ert, yi) - _dot(eit, yr)

    @pl.when(ka == 0)
    def _():
        o_ref[...] = jnp.zeros(o_ref.shape, F32)

    scale = 1.0 / (2 * length)
    for a in range(nz):
        wr = twr_ref[ka, a] * scale
        wi = twi_ref[ka, a] * scale
        o_ref[a * p:(a + 1) * p, :] += wr * vr + wi * vi
        o_ref[length + a * p:length + (a + 1) * p, :] += wr * vi - wi * vr

    @pl.when(ka == na - 1)
    def _():
        o_ref[...] = gate_ref[...] * (o_ref[...] + u_ref[...] * bias_ref[...])


def _long_conv(u_arr, u_col, gate_arr, gate_col, bias, spec, order, tables_bf, length, row_off, nb, prev_out):
    p, na = _dft_plan(length)
    twr, twi, er, ei, ert, eit = tables_bf
    t_all = u_arr.shape[0]
    blk = 2 * length
    off = row_off // blk
    mat = lambda: pl.BlockSpec((None, p, p), lambda j, ka, *_: (ka, 0, 0))
    args = [twr, twi, u_arr, gate_arr, bias.reshape(1, HY_W), er, ei, ert, eit, spec]
    in_specs = [
        pl.BlockSpec((blk, HY_W), lambda j, ka, *_: (off + j, u_col), pipeline_mode=pl.Buffered(1)),
        pl.BlockSpec((blk, HY_W), lambda j, ka, *_: (off + j, gate_col), pipeline_mode=pl.Buffered(1)),
        pl.BlockSpec((1, HY_W), lambda j, ka, *_: (0, 0)),
        mat(), mat(), mat(), mat(),
        pl.BlockSpec((None, 2, p, HY_W), lambda j, ka, *_: (ka, 0, 0, order)),
    ]
    kwargs = {}
    if prev_out is not None:
        args.append(prev_out)
        in_specs.append(pl.BlockSpec(memory_space=pl.ANY))
        kwargs["input_output_aliases"] = {len(args) - 1: 0}
    return pl.pallas_call(
        functools.partial(_conv_kernel, length=length, p=p, na=na),
        grid_spec=pltpu.PrefetchScalarGridSpec(
            num_scalar_prefetch=2,
            grid=(nb // 2, na),
            in_specs=in_specs,
            out_specs=pl.BlockSpec((blk, HY_W), lambda j, ka, *_: (off + j, 0)),
        ),
        out_shape=jax.ShapeDtypeStruct((t_all, HY_W), F32),
        compiler_params=_cparams("arbitrary", "arbitrary"),
        name="hyena_long_conv",
        **kwargs,
    )(*args)


def _attn_kernel(sink_ref, q_ref, *refs, local):
    if local:
        kp_ref, kc_ref, kn_ref, vp_ref, vc_ref, vn_ref, kx_ref, vx_ref, o_ref = refs
        n = pl.program_id(1)
        last = pl.num_programs(1) - 1
        qi = lax.broadcasted_iota(jnp.int32, (QBLK, QBLK), 0)
        ki = lax.broadcasted_iota(jnp.int32, (QBLK, QBLK), 1)
        ok_prev = (ki >= qi) & (n > 0)
        ok_next = (ki <= qi) & (n < last)
    else:
        kx_ref, vx_ref, o_ref = refs
    scale = HEAD_DIM ** -0.5
    nt = (((1,), (1,)), ((), ()))
    for h in range(N_HEADS):
        kv = h // Q_PER_KV
        cs = slice(kv * HEAD_DIM, (kv + 1) * HEAD_DIM)
        qh = q_ref[:, h * HEAD_DIM:(h + 1) * HEAD_DIM]
        sink = sink_ref[h]
        s_x = lax.dot_general(qh, kx_ref[:, cs], nt, preferred_element_type=F32) * scale
        mx = jnp.maximum(jnp.max(s_x, axis=-1, keepdims=True), sink)
        if local:
            s_p = jnp.where(ok_prev, lax.dot_general(qh, kp_ref[:, cs], nt, preferred_element_type=F32) * scale, NEG_INF)
            s_c = lax.dot_general(qh, kc_ref[:, cs], nt, preferred_element_type=F32) * scale
            s_n = jnp.where(ok_next, lax.dot_general(qh, kn_ref[:, cs], nt, preferred_element_type=F32) * scale, NEG_INF)
            for s in (s_p, s_c, s_n):
                mx = jnp.maximum(mx, jnp.max(s, axis=-1, keepdims=True))
        p_x = jnp.exp(s_x - mx)
        den = jnp.sum(p_x, axis=-1, keepdims=True) + jnp.exp(sink - mx)
        acc = _dot(p_x.astype(BF16), vx_ref[:, cs])
        if local:
            for s, v_ref in ((s_p, vp_ref), (s_c, vc_ref), (s_n, vn_ref)):
                pm = jnp.exp(s - mx)
                den = den + jnp.sum(pm, axis=-1, keepdims=True)
                acc = acc + _dot(pm.astype(BF16), v_ref[:, cs])
        o_ref[:, h * HEAD_DIM:(h + 1) * HEAD_DIM] = acc / den


def _attention(q, k, v, sink, nb, n, c):
    t_all = q.shape[0]
    nqb = n // QBLK
    ctx_blk0 = nb * n // c
    kvspec = lambda fn: pl.BlockSpec((QBLK, KV_W), fn)
    prev = lambda b, j: (b * nqb + jnp.maximum(j - 1, 0), 0)
    cur = lambda b, j: (b * nqb + j, 0)
    nxt = lambda b, j: (b * nqb + jnp.minimum(j + 1, nqb - 1), 0)
    ctxs = pl.BlockSpec((c, KV_W), lambda b, j: (ctx_blk0 + b, 0))
    smem = pl.BlockSpec(memory_space=pltpu.SMEM)
    lat = pl.pallas_call(
        functools.partial(_attn_kernel, local=True),
        grid=(nb, nqb),
        in_specs=[smem, pl.BlockSpec((QBLK, ATT_W), cur),
                  kvspec(prev), kvspec(cur), kvspec(nxt), kvspec(prev), kvspec(cur), kvspec(nxt), ctxs, ctxs],
        out_specs=pl.BlockSpec((QBLK, ATT_W), cur),
        out_shape=jax.ShapeDtypeStruct((t_all, ATT_W), F32),
        compiler_params=_cparams("arbitrary", "arbitrary"),
        name="banded_attention",
    )(sink, q, k, k, k, v, v, v, k, v)
    ctx1 = pl.BlockSpec((c, KV_W), lambda b: (ctx_blk0 + b, 0))
    return pl.pallas_call(
        lambda s, qq, kx, vx, prev_o, o: _attn_kernel(s, qq, kx, vx, o, local=False),
        grid=(nb,),
        in_specs=[smem, pl.BlockSpec((c, ATT_W), lambda b: (ctx_blk0 + b, 0)), ctx1, ctx1,
                  pl.BlockSpec(memory_space=pl.ANY)],
        out_specs=pl.BlockSpec((c, ATT_W), lambda b: (ctx_blk0 + b, 0)),
        out_shape=jax.ShapeDtypeStruct((t_all, ATT_W), F32),
        input_output_aliases={4: 0},
        compiler_params=_cparams("arbitrary"),
        name="context_attention",
    )(sink, q, k, v, lat)


def _s5disc_kernel(lre_ref, lim_ref, dt_ref, bre_ref, bim_ref, are_ref, aim_ref, bbre_ref, bbim_ref):
    lam_re = lre_ref[...]
    lam_im = lim_ref[...]
    dt = jnp.exp(dt_ref[...])
    mag = jnp.exp(lam_re * dt)
    ab_re = mag * jnp.cos(lam_im * dt)
    ab_im = mag * jnp.sin(lam_im * dt)
    num_re = ab_re - 1.0
    num_im = ab_im
    den = lam_re * lam_re + lam_im * lam_im
    co_re = (num_re * lam_re + num_im * lam_im) / den
    co_im = (num_im * lam_re - num_re * lam_im) / den
    b_re = bre_ref[...]
    b_im = bim_ref[...]
    are_ref[...] = ab_re
    aim_ref[...] = ab_im
    bbre_ref[...] = co_re * b_re - co_im * b_im
    bbim_ref[...] = co_re * b_im + co_im * b_re


def _s5_discretize(lam_re, lam_im, log_dt, b_re, b_im):
    lead = lam_re.shape[:-1]
    rep = lambda a: jnp.repeat(a.reshape(-1, 1, S5_STATE), S5_CPG, axis=1).reshape(-1, S5_STATE)
    dt = jnp.broadcast_to(log_dt.reshape(-1, 1, 1), (math.prod(lead), S5_CPG, S5_STATE)).reshape(-1, S5_STATE)
    tr = lambda b: jnp.swapaxes(b, -1, -2).reshape(-1, S5_STATE)
    rows = math.prod(lead) * S5_CPG
    shp = jax.ShapeDtypeStruct((rows, S5_STATE), F32)
    ab_re, ab_im, bb_re, bb_im = pl.pallas_call(
        _s5disc_kernel, out_shape=[shp] * 4, name="s5_discretize",
    )(rep(lam_re), rep(lam_im), dt, tr(b_re), tr(b_im))
    full = lead + (S5_CPG, S5_STATE)
    return (ab_re.reshape(full)[..., 0, :], ab_im.reshape(full)[..., 0, :],
            bb_re.reshape(full), bb_im.reshape(full))


def _s5_kernel(uf_ref, ub_ref, bmat_ref, cmat_ref, a_ref, yf_ref, yb_ref, st_ref, bu_ref, *, tc, nbatch):
    i = pl.program_id(0)
    gp = S5_GROUPS * S5_STATE

    @pl.when(i == 0)
    def _():
        st_ref[...] = jnp.zeros(st_ref.shape, F32)

    for d, (u_ref, y_ref) in enumerate(((uf_ref, yf_ref), (ub_ref, yb_ref))):
        bu_ref[...] = _dot(u_ref[...].astype(BF16), bmat_ref[d])
        ar = a_ref[d, 0]
        ai = a_ref[d, 1]

        def body(j, carry, d=d, ar=ar, ai=ai):
            hr, hi = carry
            t = j if d == 0 else tc - 1 - j
            r0 = pl.multiple_of(t * nbatch, nbatch)
            nr = ar * hr - ai * hi + bu_ref[pl.ds(r0, nbatch), 0:gp]
            ni = ar * hi + ai * hr + bu_ref[pl.ds(r0, nbatch), gp:2 * gp]
            bu_ref[pl.ds(r0, nbatch), 0:gp] = nr
            bu_ref[pl.ds(r0, nbatch), gp:2 * gp] = ni
            return nr, ni

        hr, hi = lax.fori_loop(0, tc, body, (st_ref[d, 0], st_ref[d, 1]))
        st_ref[d, 0] = hr
        st_ref[d, 1] = hi
        y_ref[...] = _dot(bu_ref[...].astype(BF16), cmat_ref[d])


def _s5_scan(u_tm, bmat, cmat, a_bc, nb, n, c):
    tc = S5_TC
    rows = tc * nb
    nl, nc = n // tc, c // tc
    gp = S5_GROUPS * S5_STATE
    fwd = lambda i: (jnp.where(i < nc, nl + i, i - nc), 0)
    bwd = lambda i: (nl + nc - 1 - i, 0)
    full = lambda arr: pl.BlockSpec(arr.shape, lambda i: (0,) * arr.ndim)
    shp = jax.ShapeDtypeStruct(u_tm.shape, F32)
    return pl.pallas_call(
        functools.partial(_s5_kernel, tc=tc, nbatch=nb),
        grid=(nl + nc,),
        in_specs=[pl.BlockSpec((rows, S5_W), fwd), pl.BlockSpec((rows, S5_W), bwd),
                  full(bmat), full(cmat), full(a_bc)],
        out_specs=[pl.BlockSpec((rows, S5_W), fwd), pl.BlockSpec((rows, S5_W), bwd)],
        out_shape=[shp, shp],
        scratch_shapes=[pltpu.VMEM((2, 2, nb, gp), F32), pltpu.VMEM((rows, 2 * gp), F32)],
        compiler_params=_cparams("arbitrary"),
        name="s5_scan",
    )(u_tm, u_tm, bmat, cmat, a_bc)


def _s5_matrices(ab_re, ab_im, bb_re, bb_im, c_re, c_im, nb):
    eye = jnp.eye(S5_GROUPS, dtype=F32)
    bd_in = lambda b: jnp.einsum("dgcp,gh->dgchp", b, eye).reshape(2, S5_W, S5_GROUPS * S5_STATE)
    bd_out = lambda cc: jnp.einsum("dgcp,gh->dgphc", cc, eye).reshape(2, S5_GROUPS * S5_STATE, S5_W)
    bmat = jnp.concatenate([bd_in(bb_re), bd_in(bb_im)], axis=-1).astype(BF16)
    cmat = jnp.concatenate([bd_out(c_re), -bd_out(c_im)], axis=1).astype(BF16)
    a = jnp.stack([ab_re.reshape(2, -1), ab_im.reshape(2, -1)], axis=1)
    a_bc = jnp.broadcast_to(a[:, :, None, :], (2, 2, nb, S5_GROUPS * S5_STATE))
    return bmat, cmat, a_bc


def _out_kernel(x_ref, hy_ref, att_ref, y_ref, u_ref, d_ref, gluw_ref, glub_ref, ng_ref, wout_ref,
                g1_ref, sh2_ref, sc2_ref, n2g_ref, rw_ref, rb_ref, tri_ref,
                xo_ref, h2_ref, eidx_ref, gate_ref, rank_ref, cnt_ref, carry_ref):
    i = pl.program_id(0)

    @pl.when(i == 0)
    def _():
        carry_ref[...] = jnp.zeros(carry_ref.shape, F32)

    y = y_ref[...] + u_ref[...] * d_ref[...]
    g = jax.nn.gelu(y)
    s5 = g * jax.nn.sigmoid(_dot(g.astype(BF16), gluw_ref[...]) + glub_ref[...])

    def nrm(part):
        return part * lax.rsqrt(jnp.mean(part * part, axis=-1, keepdims=True) + EPS)

    mix = jnp.concatenate([nrm(hy_ref[...]), nrm(att_ref[...]), nrm(s5)], axis=-1) * ng_ref[...]
    x = x_ref[...] + g1_ref[...] * _dot(mix.astype(BF16), wout_ref[...])
    xo_ref[...] = x
    h2 = _rms(x, n2g_ref[...]) * (1.0 + sc2_ref[...]) + sh2_ref[...]
    h2_ref[...] = h2

    logits = lax.dot_general(rw_ref[...], h2, (((1,), (1,)), ((), ())), precision=HIGHEST,
                             preferred_element_type=F32)
    e = jnp.exp(logits - jnp.max(logits, axis=0, keepdims=True))
    probs = e / jnp.sum(e, axis=0, keepdims=True)
    sel = probs + rb_ref[...]
    rows = [sel[r:r + 1, :] for r in range(N_EXPERTS)]
    best = None
    for grp in range(N_EXPERT_GROUPS):
        a = rows[grp * EXPERTS_PER_GROUP:(grp + 1) * EXPERTS_PER_GROUP]
        score = None
        for p0 in range(EXPERTS_PER_GROUP):
            for p1 in range(p0 + 1, EXPERTS_PER_GROUP):
                pair = a[p0] + a[p1]
                score = pair if score is None else jnp.maximum(score, pair)
        if best is None:
            best, gidx = score, jnp.zeros(score.shape, jnp.int32)
        else:
            better = score > best
            gidx = jnp.where(better, grp, gidx)
            best = jnp.where(better, score, best)
    ing = []
    for j in range(EXPERTS_PER_GROUP):
        v = rows[j]
        for grp in range(1, N_EXPERT_GROUPS):
            v = jnp.where(gidx == grp, rows[grp * EXPERTS_PER_GROUP + j], v)
        ing.append(v)
    first_v, first_i = ing[0], jnp.zeros(gidx.shape, jnp.int32)
    for j in range(1, EXPERTS_PER_GROUP):
        better = ing[j] > first_v
        first_i = jnp.where(better, j, first_i)
        first_v = jnp.where(better, ing[j], first_v)
    second_v, second_i = None, None
    for j in range(EXPERTS_PER_GROUP):
        cand = jnp.where(first_i == j, -jnp.inf, ing[j])
        if second_v is None:
            second_v, second_i = cand, jnp.zeros(gidx.shape, jnp.int32)
        else:
            better = cand > second_v
            second_i = jnp.where(better, j, second_i)
            second_v = jnp.where(better, cand, second_v)
    e0 = gidx * EXPERTS_PER_GROUP + first_i
    e1 = gidx * EXPERTS_PER_GROUP + second_i
    eid = lax.broadcasted_iota(jnp.int32, probs.shape, 0)
    oh0 = eid == e0
    oh1 = eid == e1
    p0v = jnp.sum(jnp.where(oh0, probs, 0.0), axis=0, keepdims=True)
    p1v = jnp.sum(jnp.where(oh1, probs, 0.0), axis=0, keepdims=True)
    tot = p0v + p1v
    member = jnp.where(oh0 | oh1, 1.0, 0.0)
    before = _dot(member.astype(BF16), tri_ref[...]) + carry_ref[...]
    r0 = jnp.sum(jnp.where(oh0, before, 0.0), axis=0, keepdims=True)
    r1 = jnp.sum(jnp.where(oh1, before, 0.0), axis=0, keepdims=True)
    carry = carry_ref[...] + jnp.sum(member, axis=1, keepdims=True)
    carry_ref[...] = carry
    cnt_ref[...] = carry
    eidx_ref[...] = jnp.concatenate([e0, e1], axis=0)
    gate_ref[...] = jnp.concatenate([p0v / tot, p1v / tot], axis=0)
    rank_ref[...] = jnp.concatenate([r0, r1], axis=0).astype(jnp.int32)


def _mixer_out(x_all, hy, att, y_tok, u, s5_d, glu_w_bf, glu_b, mix_g, w_out_bf, mod4, n2g,
               router_wt, router_b, nb, n):
    t_all = x_all.shape[0]
    n_lat_tiles = nb * n // TM
    tpb = n // TM
    mrow = lambda i: jnp.where(i < n_lat_tiles, i // tpb, nb)
    row = lambda w: pl.BlockSpec((TM, w), lambda i: (i, 0))
    full = lambda arr: pl.BlockSpec(arr.shape, lambda i: (0,) * arr.ndim)
    modspec = lambda k: pl.BlockSpec((None, None, 1, D_MODEL), lambda i, k=k: (mrow(i), k, 0, 0))
    tri = (lax.broadcasted_iota(jnp.int32, (TM, TM), 0) < lax.broadcasted_iota(jnp.int32, (TM, TM), 1)).astype(BF16)
    v1 = lambda a: a.reshape(1, -1)
    ins = [v1(s5_d), glu_w_bf, v1(glu_b), v1(mix_g), w_out_bf]
    tail = [v1(n2g), router_wt, router_b.reshape(N_EXPERTS, 1), tri]
    tok = lambda: pl.BlockSpec((2, TM), lambda i: (0, i))
    return pl.pallas_call(
        _out_kernel,
        grid=(t_all // TM,),
        in_specs=[row(D_MODEL), row(HY_W), row(ATT_W), row(S5_W), row(S5_W)] + [full(a) for a in ins]
                 + [modspec(2), modspec(3), modspec(4)] + [full(a) for a in tail],
        out_specs=[row(D_MODEL), row(D_MODEL), tok(), tok(), tok(),
                   pl.BlockSpec((N_EXPERTS, 1), lambda i: (0, 0))],
        out_shape=[
            jax.ShapeDtypeStruct((t_all, D_MODEL), F32),
            jax.ShapeDtypeStruct((t_all, D_MODEL), F32),
            jax.ShapeDtypeStruct((2, t_all), jnp.int32),
            jax.ShapeDtypeStruct((2, t_all), F32),
            jax.ShapeDtypeStruct((2, t_all), jnp.int32),
            jax.ShapeDtypeStruct((N_EXPERTS, 1), F32),
        ],
        scratch_shapes=[pltpu.VMEM((N_EXPERTS, 1), F32)],
        compiler_params=_cparams("arbitrary"),
        name="mixer_out_router",
    )(x_all, hy, att, y_tok, u, *ins, mod4, mod4, mod4, *tail)


def _dispatch_kernel(pos_ref, h2_ref, zero_ref, xb_ref, sem):
    del zero_ref
    i = pl.program_id(0)
    base = i * (2 * TM)

    def copy(r, slot):
        return pltpu.make_async_copy(h2_ref.at[pl.ds(r, 1)], xb_ref.at[pl.ds(slot, 1)], sem)

    def issue(r, carry):
        copy(r, pos_ref[base + r]).start()
        copy(r, pos_ref[base + TM + r]).start()
        return carry

    lax.fori_loop(0, TM, issue, 0)

    def drain(r, carry):
        copy(r, pos_ref[base + r]).wait()
        copy(r, pos_ref[base + TM + r]).wait()
        return carry

    lax.fori_loop(0, TM, drain, 0)


def _dispatch(pos_flat, h2, n_slots):
    t_all = h2.shape[0]
    zeros = jnp.zeros((n_slots, D_MODEL), F32)
    return pl.pallas_call(
        _dispatch_kernel,
        grid_spec=pltpu.PrefetchScalarGridSpec(
            num_scalar_prefetch=1,
            grid=(t_all // TM,),
            in_specs=[pl.BlockSpec((TM, D_MODEL), lambda i, *_: (i, 0)), pl.BlockSpec(memory_space=pl.ANY)],
            out_specs=pl.BlockSpec(memory_space=pl.ANY),
            scratch_shapes=[pltpu.SemaphoreType.DMA(())],
        ),
        out_shape=jax.ShapeDtypeStruct((n_slots, D_MODEL), F32),
        input_output_aliases={2: 0},
        compiler_params=_cparams("arbitrary"),
        name="moe_dispatch",
    )(pos_flat, h2, zeros)


def _ffn_kernel(be_ref, nu_ref, x_ref, wg_ref, wu_ref, wd_ref, o_ref):
    del be_ref

    @pl.when(pl.program_id(0) < nu_ref[0])
    def _():
        x = x_ref[...].astype(BF16)
        a = _dot(x, wg_ref[...])
        u = _dot(x, wu_ref[...])
        hidden = (a * jax.nn.sigmoid(a)) * u
        o_ref[...] = _dot(hidden.astype(BF16), wd_ref[...])


def _expert_ffn(block_e, n_used, xb, wg, wu, wd):
    n_slots = xb.shape[0]
    blk = lambda j, be, nu: jnp.minimum(j, nu[0] - 1)
    wspec = lambda: pl.BlockSpec((None, D_MODEL, D_MODEL), lambda j, be, nu: (be[blk(j, be, nu)], 0, 0))
    return pl.pallas_call(
        _ffn_kernel,
        grid_spec=pltpu.PrefetchScalarGridSpec(
            num_scalar_prefetch=2,
            grid=(n_slots // MOE_ROWS,),
            in_specs=[pl.BlockSpec((MOE_ROWS, D_MODEL), lambda j, be, nu: (blk(j, be, nu), 0)),
                      wspec(), wspec(), wspec()],
            out_specs=pl.BlockSpec((MOE_ROWS, D_MODEL), lambda j, be, nu: (blk(j, be, nu), 0)),
        ),
        out_shape=jax.ShapeDtypeStruct((n_slots, D_MODEL), F32),
        compiler_params=_cparams("arbitrary"),
        name="moe_expert_ffn",
    )(block_e, n_used, xb, wg, wu, wd)


def _combine_kernel(pos_ref, x_ref, gates_ref, g2_ref, yb_ref, xo_ref, buf_ref, sem):
    i = pl.program_id(0)
    base = i * (2 * TM)

    def copy(k, r, slot):
        return pltpu.make_async_copy(yb_ref.at[pl.ds(slot, 1)], buf_ref.at[k, pl.ds(r, 1)], sem)

    def issue(r, carry):
        copy(0, r, pos_ref[base + r]).start()
        copy(1, r, pos_ref[base + TM + r]).start()
        return carry

    lax.fori_loop(0, TM, issue, 0)

    def drain(r, carry):
        copy(0, r, pos_ref[base + r]).wait()
        copy(1, r, pos_ref[base + TM + r]).wait()
        return carry

    lax.fori_loop(0, TM, drain, 0)
    gates = gates_ref[...]
    y = buf_ref[0] * gates[:, 0:1] + buf_ref[1] * gates[:, 1:2]
    xo_ref[...] = x_ref[...] + g2_ref[...] * y


def _combine(pos_flat, x_all, gates_t, mod4, yb, nb, n):
    t_all = x_all.shape[0]
    n_lat_tiles = nb * n // TM
    tpb = n // TM
    mrow = lambda i: jnp.where(i < n_lat_tiles, i // tpb, nb)
    return pl.pallas_call(
        _combine_kernel,
        grid_spec=pltpu.PrefetchScalarGridSpec(
            num_scalar_prefetch=1,
            grid=(t_all // TM,),
            in_specs=[pl.BlockSpec((TM, D_MODEL), lambda i, *_: (i, 0)),
                      pl.BlockSpec((TM, 2), lambda i, *_: (i, 0)),
                      pl.BlockSpec((None, None, 1, D_MODEL), lambda i, *_: (mrow(i), 5, 0, 0)),
                      pl.BlockSpec(memory_space=pl.ANY)],
            out_specs=pl.BlockSpec((TM, D_MODEL), lambda i, *_: (i, 0)),
            scratch_shapes=[pltpu.VMEM((2, TM, D_MODEL), F32), pltpu.SemaphoreType.DMA(())],
        ),
        out_shape=jax.ShapeDtypeStruct((t_all, D_MODEL), F32),
        compiler_params=_cparams("arbitrary"),
        name="moe_combine",
    )(pos_flat, x_all, gates_t, mod4, yb)


def _final_kernel(x_ref, g_ref, o_ref):
    o_ref[...] = _rms(x_ref[...], g_ref[...])


def _final_norm(x_all, g, rows):
    return pl.pallas_call(
        _final_kernel,
        grid=(rows // TM,),
        in_specs=[pl.BlockSpec((TM, D_MODEL), lambda i: (i, 0)), pl.BlockSpec((1, D_MODEL), lambda i: (0, 0))],
        out_specs=pl.BlockSpec((TM, D_MODEL), lambda i: (i, 0)),
        out_shape=jax.ShapeDtypeStruct((rows, D_MODEL), F32),
        compiler_params=_cparams("arbitrary"),
        name="final_norm",
    )(x_all, g.reshape(1, D_MODEL))


def _time_major(a, nb, n, c):
    w = a.shape[-1]
    lat = jnp.swapaxes(a[:nb * n].reshape(nb, n, w), 0, 1).reshape(n * nb, w)
    ctx = jnp.swapaxes(a[nb * n:].reshape(nb, c, w), 0, 1).reshape(c * nb, w)
    return jnp.concatenate([lat, ctx], axis=0)


def _token_major(a, nb, n, c):
    w = a.shape[-1]
    lat = jnp.swapaxes(a[:n * nb].reshape(n, nb, w), 0, 1).reshape(nb * n, w)
    ctx = jnp.swapaxes(a[n * nb:].reshape(c, nb, w), 0, 1).reshape(nb * c, w)
    return jnp.concatenate([lat, ctx], axis=0)


def kernel(x, c, ctx, c_ctx, norm1_g, norm2_g, ada_w, ada_b, w_in, w_out, mix_norm_g, hy_conv_w, hy_conv_b, hy_f_w1, hy_f_b1, hy_f_freq1, hy_f_w2, hy_f_b2, hy_f_freq2, hy_f_w3, hy_bias, attn_sink, s5_lam_re, s5_lam_im, s5_log_dt, s5_b_re, s5_b_im, s5_c_re, s5_c_im, s5_d, s5_glu_w, s5_glu_b, router_w, router_b, moe_w_gate, moe_w_up, moe_w_down, final_g):
    nb, n, d = x.shape
    cl = ctx.shape[1]
    depth = w_in.shape[0]
    t_all = nb * (n + cl)

    cc = jnp.zeros((16, d), F32).at[:nb].set(c).at[nb].set(c_ctx)
    mod_all = _modulation(cc, ada_w, ada_b).reshape(depth, 16, 6, 1, d)
    cos, sin = _rope_tables(n)
    ab_re, ab_im, bb_re, bb_im = _s5_discretize(s5_lam_re, s5_lam_im, s5_log_dt, s5_b_re, s5_b_im)

    tabs = {}
    for length in (n, cl):
        twr, twi, er, ei = _dft_tables(length)
        tabs[length] = ((twr, twi, er, ei),
                        (twr, twi, er.astype(BF16), ei.astype(BF16),
                         jnp.swapaxes(er, 1, 2).astype(BF16), jnp.swapaxes(ei, 1, 2).astype(BF16)))

    router_wt = router_w.T
    n_slots = (-(-(2 * t_all) // MOE_ROWS) + N_EXPERTS) * MOE_ROWS
    x_all = jnp.concatenate([x.reshape(nb * n, d), ctx.reshape(nb * cl, d)], axis=0)

    for l in range(depth):
        mod4 = mod_all[l]
        hy, q, k, v, u = _input_proj(x_all, norm1_g[l], mod4, w_in[l].astype(BF16), cos, sin, nb, n)

        zc = _short_conv(hy, hy_conv_w[l], hy_conv_b[l], nb, n, cl)
        hy_out = None
        z1 = None
        specs = {}
        for length in (n, cl):
            specs[length] = _filter_spectrum(length, tabs[length][0], hy_f_w1[l], hy_f_b1[l], hy_f_freq1[l],
                                             hy_f_w2[l], hy_f_b2[l], hy_f_freq2[l], hy_f_w3[l])
        for length, off in ((n, 0), (cl, nb * n)):
            z1 = _long_conv(zc, 0, zc, 1, hy_bias[l, 0], specs[length], 0, tabs[length][1], length, off, nb, z1)
        for length, off in ((n, 0), (cl, nb * n)):
            hy_out = _long_conv(z1, 0, zc, 2, hy_bias[l, 1], specs[length], 1, tabs[length][1], length, off, nb,
                                hy_out)

        att = _attention(q, k, v, attn_sink[l], nb, n, cl)

        bmat, cmat, a_bc = _s5_matrices(ab_re[l], ab_im[l], bb_re[l], bb_im[l], s5_c_re[l], s5_c_im[l], nb)
        y_f, y_b = _s5_scan(_time_major(u, nb, n, cl), bmat, cmat, a_bc, nb, n, cl)
        y_tok = _token_major(y_f + y_b, nb, n, cl)

        x_all, h2, eidx, gates, rank, counts = _mixer_out(
            x_all, hy_out, att, y_tok, u, s5_d[l], s5_glu_w[l].astype(BF16), s5_glu_b[l], mix_norm_g[l],
            w_out[l].astype(BF16), mod4, norm2_g[l], router_wt, router_b, nb, n)

        cnt = counts.reshape(N_EXPERTS).astype(jnp.int32)
        padded = (cnt + MOE_ROWS - 1) // MOE_ROWS * MOE_ROWS
        ends = jnp.cumsum(padded)
        pstart = ends - padded
        pos = pstart[eidx] + rank
        pos_flat = jnp.swapaxes(pos.reshape(2, t_all // TM, TM), 0, 1).reshape(-1)
        n_blocks = n_slots // MOE_ROWS
        block_e = jnp.clip(jnp.searchsorted(ends, jnp.arange(n_blocks, dtype=jnp.int32) * MOE_ROWS, side="right"),
                           0, N_EXPERTS - 1).astype(jnp.int32)
        n_used = (ends[-1] // MOE_ROWS).reshape(1).astype(jnp.int32)

        xb = _dispatch(pos_flat, h2, n_slots)
        yb = _expert_ffn(block_e, n_used, xb, moe_w_gate[l].astype(BF16), moe_w_up[l].astype(BF16),
                         moe_w_down[l].astype(BF16))
        x_all = _combine(pos_flat, x_all, gates.T, mod4, yb, nb, n)

    return _final_norm(x_all, final_g, nb * n).reshape(nb, n, d)
```

```python
import functools
import math

import jax
import jax.numpy as jnp
from jax import lax
from jax.experimental import pallas as pl
from jax.experimental.pallas import tpu as pltpu

F32 = jnp.float32
BF16 = jnp.bfloat16
HIGHEST = lax.Precision.HIGHEST

D_MODEL = 1024
GRID_W = 64
HY_W = 256
ATT_W = 512
S5_W = 256
HEAD_DIM = 64
N_HEADS = 8
N_KV_HEADS = 2
Q_PER_KV = 4
KV_W = 128
WINDOW = 128
QBLK = 128
ROPE_BASE = 10000.0
NEG_INF = -1e30
HY_BANDS = 16
HY_DECAY_MIN = math.log(1e-2) / 1.5
HY_DECAY_MAX = math.log(1e-2) / 0.3
S5_CPG = 16
S5_GROUPS = 16
S5_STATE = 64
N_EXPERTS = 16
N_EXPERT_GROUPS = 4
EXPERTS_PER_GROUP = 4
HY_END = 768
Q_END = 1280
K_END = 1408
V_END = 1536
IN_W = 1792
EPS = 1e-6

TM = 256
MOE_ROWS = 256
S5_TC = 64
DFT_P = 512
VMEM_LIMIT = 56 * 1024 * 1024


def _cparams(*sem):
    return pltpu.CompilerParams(dimension_semantics=sem, vmem_limit_bytes=VMEM_LIMIT)


def _dot(a, b):
    return jnp.dot(a, b, preferred_element_type=F32)


def _mod_kernel(c_ref, w_ref, b_ref, o_ref):
    c = c_ref[...]
    a = c * jax.nn.sigmoid(c)
    o_ref[...] = _dot(a.astype(BF16), w_ref[...].astype(BF16)) + b_ref[...]


def _modulation(cc, ada_w, ada_b):
    depth, d, w6 = ada_w.shape
    nb = 1536
    return pl.pallas_call(
        _mod_kernel,
        grid=(depth, w6 // nb),
        in_specs=[
            pl.BlockSpec((16, d), lambda l, j: (0, 0)),
            pl.BlockSpec((None, d, nb), lambda l, j: (l, 0, j)),
            pl.BlockSpec((None, 1, nb), lambda l, j: (l, 0, j)),
        ],
        out_specs=pl.BlockSpec((None, 16, nb), lambda l, j: (l, 0, j)),
        out_shape=jax.ShapeDtypeStruct((depth, 16, w6), F32),
        compiler_params=_cparams("arbitrary", "arbitrary"),
        name="adaln_mod",
    )(cc, ada_w, ada_b.reshape(depth, 1, w6))


def _rms(x, g):
    return x * lax.rsqrt(jnp.mean(x * x, axis=-1, keepdims=True) + EPS) * g


def _in_kernel(x_ref, g_ref, sh_ref, sc_ref, w_ref, cos_ref, sin_ref,
               hy_ref, q_ref, k_ref, v_ref, u_ref):
    h = _rms(x_ref[...], g_ref[...]) * (1.0 + sc_ref[...]) + sh_ref[...]
    p = _dot(h.astype(BF16), w_ref[...])
    hy_ref[...] = p[:, :HY_END]
    cos = cos_ref[...]
    sin = sin_ref[...]
    lane = lax.broadcasted_iota(jnp.int32, cos.shape, 1)
    first = (lane % 32) < 16

    def rope(z):
        swapped = jnp.where(first, pltpu.roll(z, 112, 1), pltpu.roll(z, 16, 1))
        return z * cos + swapped * sin

    for j in range(ATT_W // 128):
        q_ref[:, j * 128:(j + 1) * 128] = rope(p[:, HY_END + j * 128:HY_END + (j + 1) * 128]).astype(BF16)
    k_ref[...] = rope(p[:, Q_END:K_END]).astype(BF16)
    v_ref[...] = p[:, K_END:V_END].astype(BF16)
    u_ref[...] = p[:, V_END:]


def _rope_tables(n):
    quarter = HEAD_DIM // 4
    inv_freq = ROPE_BASE ** (-jnp.arange(quarter, dtype=F32) / quarter)
    t = jnp.arange(n)
    rows = (t // GRID_W).astype(F32)
    cols = (t % GRID_W).astype(F32)
    ang_r = rows[:, None] * inv_freq[None, :]
    ang_c = cols[:, None] * inv_freq[None, :]
    cos64 = jnp.concatenate([jnp.cos(ang_r)] * 2 + [jnp.cos(ang_c)] * 2, axis=-1)
    sin64 = jnp.concatenate([-jnp.sin(ang_r), jnp.sin(ang_r), -jnp.sin(ang_c), jnp.sin(ang_c)], axis=-1)
    cos = jnp.concatenate([jnp.tile(cos64, (1, 2)), jnp.ones((TM, 128), F32)], axis=0)
    sin = jnp.concatenate([jnp.tile(sin64, (1, 2)), jnp.zeros((TM, 128), F32)], axis=0)
    return cos, sin


def _input_proj(x_all, g, mod4, w_in_bf, cos, sin, nb, n):
    t_all = x_all.shape[0]
    n_lat_tiles = nb * n // TM
    tpb = n // TM

    def mrow(i):
        return jnp.where(i < n_lat_tiles, i // tpb, nb)

    def trow(i):
        return jnp.where(i < n_lat_tiles, i % tpb, tpb)

    row = lambda w: pl.BlockSpec((TM, w), lambda i: (i, 0))
    return pl.pallas_call(
        _in_kernel,
        grid=(t_all // TM,),
        in_specs=[
            row(D_MODEL),
            pl.BlockSpec((1, D_MODEL), lambda i: (0, 0)),
            pl.BlockSpec((None, None, 1, D_MODEL), lambda i: (mrow(i), 0, 0, 0)),
            pl.BlockSpec((None, None, 1, D_MODEL), lambda i: (mrow(i), 1, 0, 0)),
            pl.BlockSpec((D_MODEL, IN_W), lambda i: (0, 0)),
            pl.BlockSpec((TM, 128), lambda i: (trow(i), 0)),
            pl.BlockSpec((TM, 128), lambda i: (trow(i), 0)),
        ],
        out_specs=[row(HY_END), row(ATT_W), row(KV_W), row(KV_W), row(S5_W)],
        out_shape=[
            jax.ShapeDtypeStruct((t_all, HY_END), F32),
            jax.ShapeDtypeStruct((t_all, ATT_W), BF16),
            jax.ShapeDtypeStruct((t_all, KV_W), BF16),
            jax.ShapeDtypeStruct((t_all, KV_W), BF16),
            jax.ShapeDtypeStruct((t_all, S5_W), F32),
        ],
        compiler_params=_cparams("arbitrary"),
        name="norm_mod_inproj",
    )(x_all, g.reshape(1, D_MODEL), mod4, mod4, w_in_bf, cos, sin)


def _sconv_kernel(z_ref, w_ref, b_ref, o_ref):
    z = z_ref[...]
    length = z.shape[0]
    row = lax.broadcasted_iota(jnp.int32, z.shape, 0)
    zm = jnp.where(row == 0, 0.0, pltpu.roll(z, 1, 0))
    zp = jnp.where(row == length - 1, 0.0, pltpu.roll(z, length - 1, 0))
    o_ref[...] = (b_ref[...] + zm * w_ref[0:1, :] + z * w_ref[1:2, :] + zp * w_ref[2:3, :]).astype(o_ref.dtype)


def _short_conv(hy, conv_w, conv_b, nb, n, c):
    t_all = hy.shape[0]
    out = None
    for length, off in ((n, 0), (c, nb * n // c)):
        kwargs = {}
        args = [hy, conv_w, conv_b.reshape(1, HY_END)]
        in_specs = [
            pl.BlockSpec((length, 256), lambda b, j, off=off: (off + b, j)),
            pl.BlockSpec((3, 256), lambda b, j: (0, j)),
            pl.BlockSpec((1, 256), lambda b, j: (0, j)),
        ]
        kern = _sconv_kernel
        if out is not None:
            args.append(out)
            in_specs.append(pl.BlockSpec(memory_space=pl.ANY))
            kwargs["input_output_aliases"] = {3: 0}
            kern = lambda z, w, b, prev, o: _sconv_kernel(z, w, b, o)
        out = pl.pallas_call(
            kern,
            grid=(nb, 3),
            in_specs=in_specs,
            out_specs=pl.BlockSpec((length, 256), lambda b, j, off=off: (off + b, j)),
            out_shape=jax.ShapeDtypeStruct((t_all, HY_END), BF16),
            compiler_params=_cparams("arbitrary", "arbitrary"),
            name="hyena_short_conv",
            **kwargs,
        )(*args)
    return out


def _dft_plan(length):
    p = min(DFT_P, length)
    return p, 2 * length // p


def _dft_tables(length):
    p, na = _dft_plan(length)
    m = 2 * length
    a = jnp.arange(na, dtype=jnp.int32)
    ang_w = (2.0 * math.pi / na) * ((a[:, None] * a[None, :]) % na).astype(F32)
    twr, twi = jnp.cos(ang_w), -jnp.sin(ang_w)
    r = jnp.arange(p, dtype=jnp.int32)
    k = a[:, None, None] + na * r[None, :, None]
    ang = (2.0 * math.pi / m) * ((k * r[None, None, :]) % m).astype(F32)
    er, ei = jnp.cos(ang), -jnp.sin(ang)
    return twr, twi, er, ei


def _filter_kernel(twr_ref, twi_ref, w1_ref, b1_ref, f1_ref, w2_ref, b2_ref, f2_ref, w3_ref,
                   bands_ref, decay_ref, er_ref, ei_ref, h_ref, kern_ref, norm_ref, *, length, p, na):
    ka = pl.program_id(0)
    m = 2 * length
    hdot = functools.partial(jnp.dot, precision=HIGHEST, preferred_element_type=F32)

    @pl.when(ka == 0)
    def _():
        norm = jnp.zeros((1, 2 * HY_W), F32)
        for a in range(na):
            n_idx = a * p + lax.broadcasted_iota(jnp.int32, (p, 1), 0)
            is_fwd = n_idx < length
            pos = jnp.where(is_fwd, n_idx, m - n_idx).astype(F32)
            t = pos / float(max(length - 1, 1))
            ang = (2.0 * math.pi / length) * pos * bands_ref[...]
            pre = (t * w1_ref[0:1, :] + hdot(jnp.cos(ang), w1_ref[1:1 + HY_BANDS, :])
                   + hdot(-jnp.sin(ang), w1_ref[1 + HY_BANDS:, :]) + b1_ref[...])
            hid = jnp.sin(f1_ref[...] * pre)
            hid = jnp.sin(f2_ref[...] * (hdot(hid, w2_ref[...]) + b2_ref[...]))
            taps = hdot(hid, w3_ref[...])
            wnd = jnp.exp(-t * decay_ref[...])
            live = n_idx != length
            for o in range(2):
                fwd = taps[:, o * 2 * HY_W:o * 2 * HY_W + HY_W]
                bwd = taps[:, o * 2 * HY_W + HY_W:(o + 1) * 2 * HY_W]
                kern = jnp.where(live, jnp.where(is_fwd, fwd, bwd) * wnd, 0.0)
                kern_ref[a, :, o * HY_W:(o + 1) * HY_W] = kern
            norm = norm + jnp.sum(jnp.abs(kern_ref[a]), axis=0, keepdims=True)
        norm_ref[...] = norm

    gr = jnp.zeros((p, 2 * HY_W), F32)
    gi = jnp.zeros((p, 2 * HY_W), F32)
    for a in range(na):
        slab = kern_ref[a]
        gr = gr + twr_ref[ka, a] * slab
        gi = gi + twi_ref[ka, a] * slab
    er = er_ref[...]
    ei = ei_ref[...]
    inv = 1.0 / norm_ref[...]
    h_ref[0] = (hdot(er, gr) - hdot(ei, gi)) * inv
    h_ref[1] = (hdot(ei, gr) + hdot(er, gi)) * inv


def _filter_spectrum(length, tables, w1, b1, f1, w2, b2, f2, w3):
    p, na = _dft_plan(length)
    twr, twi, er, ei = tables
    bands = jnp.linspace(1e-4, HY_BANDS - 1, HY_BANDS, dtype=F32).reshape(1, HY_BANDS)
    decay = jnp.abs(jnp.linspace(HY_DECAY_MIN, HY_DECAY_MAX, HY_W, dtype=F32)).reshape(1, HY_W)
    full = lambda arr: pl.BlockSpec(arr.shape, lambda ka, *_: (0,) * arr.ndim)
    vec = lambda v: v.reshape(1, -1)
    ins = [w1, vec(b1), vec(f1), w2, vec(b2), vec(f2), w3, bands, decay]
    return pl.pallas_call(
        functools.partial(_filter_kernel, length=length, p=p, na=na),
        grid_spec=pltpu.PrefetchScalarGridSpec(
            num_scalar_prefetch=2,
            grid=(na,),
            in_specs=[full(x) for x in ins] + [
                pl.BlockSpec((None, p, p), lambda ka, *_: (ka, 0, 0)),
                pl.BlockSpec((None, p, p), lambda ka, *_: (ka, 0, 0)),
            ],
            out_specs=pl.BlockSpec((None, 2, p, 2 * HY_W), lambda ka, *_: (ka, 0, 0, 0)),
            scratch_shapes=[pltpu.VMEM((na, p, 2 * HY_W), F32), pltpu.VMEM((1, 2 * HY_W), F32)],
        ),
        out_shape=jax.ShapeDtypeStruct((na, 2, p, 2 * HY_W), F32),
        compiler_params=_cparams("arbitrary"),
        name="hyena_filter_spectrum",
    )(twr, twi, *[vec(x) if x.ndim == 1 else x for x in ins], er, ei)


def _cadd(x, y):
    return x[0] + y[0], x[1] + y[1]


def _csub(x, y):
    return x[0] - y[0], x[1] - y[1]


def _radix_mix(parts, sign):
    if len(parts) == 1:
        return parts
    p0, p1, p2, p3 = parts
    t0, t1, t2, t3 = _cadd(p0, p2), _csub(p0, p2), _cadd(p1, p3), _csub(p1, p3)
    it3 = (-t3[1], t3[0])
    if sign < 0:
        return [_cadd(t0, t2), _csub(t1, it3), _csub(t0, t2), _cadd(t1, it3)]
    return [_cadd(t0, t2), _cadd(t1, it3), _csub(t0, t2), _csub(t1, it3)]


def _conv_kernel(twr_ref, twi_ref, u_ref, gate_ref, bias_ref, er_ref, ei_ref, h_ref, *rest, length, p, radix):
    o_ref, acc_ref = rest[-2], rest[-1]
    q = pl.program_id(1)
    nz = length // p
    tdot = lambda x, y: lax.dot_general(x, y, (((0,), (0,)), ((), ())), preferred_element_type=F32)

    parts = [None] * radix
    for a in range(nz):
        wr = twr_ref[q, a]
        wi = twi_ref[q, a]
        zr = u_ref[a * p:(a + 1) * p, :].astype(F32)
        zi = u_ref[length + a * p:length + (a + 1) * p, :].astype(F32)
        term = (wr * zr - wi * zi, wr * zi + wi * zr)
        parts[a % radix] = term if parts[a % radix] is None else _cadd(parts[a % radix], term)
    zero = jnp.zeros((p, o_ref.shape[-1]), F32)
    parts = [(zero, zero) if t is None else t for t in parts]
    g = _radix_mix(parts, -1)

    v = []
    for m in range(radix):
        grb = g[m][0].astype(BF16)
        gib = g[m][1].astype(BF16)
        er = er_ref[m]
        ei = ei_ref[m]
        sr = _dot(er, grb) - _dot(ei, gib)
        si = _dot(ei, grb) + _dot(er, gib)
        hr = h_ref[m, 0].astype(F32)
        hi = h_ref[m, 1].astype(F32)
        yr = (sr * hr - si * hi).astype(BF16)
        yi = (sr * hi + si * hr).astype(BF16)
        v.append((tdot(er, yr) + tdot(ei, yi), tdot(er, yi) - tdot(ei, yr)))
    qs = _radix_mix(v, +1)

    @pl.when(q == 0)
    def _():
        acc_ref[...] = jnp.zeros(acc_ref.shape, F32)

    scale = 1.0 / (2 * length)
    for a in range(nz):
        wr = twr_ref[q, a] * scale
        wi = twi_ref[q, a] * scale
        vr, vi = qs[a % radix]
        acc_ref[a * p:(a + 1) * p, :] += wr * vr + wi * vi
        acc_ref[length + a * p:length + (a + 1) * p, :] += wr * vi - wi * vr

    @pl.when(q == pl.num_programs(1) - 1)
    def _():
        u = u_ref[...].astype(F32)
        o_ref[...] = (gate_ref[...].astype(F32) * (acc_ref[...] + u * bias_ref[...])).astype(o_ref.dtype)


def _long_conv(u_arr, u_col, gate_arr, gate_col, bias, spec, order, tables_bf, length, row_off, nb, prev_out):
    p, na = _dft_plan(length)
    radix = 4 if na % 4 == 0 and na >= 8 else 1
    nq = na // radix
    twr, twi, er, ei = tables_bf
    t_all = u_arr.shape[0]
    blk = 2 * length
    off = row_off // blk
    mat = lambda: pl.BlockSpec((radix, None, p, p), lambda j, q, *_: (0, q, 0, 0))
    args = [twr, twi, u_arr, gate_arr, bias.reshape(1, HY_W), er.reshape(radix, nq, p, p),
            ei.reshape(radix, nq, p, p), spec.reshape(radix, nq, 2, p, 2 * HY_W)]
    in_specs = [
        pl.BlockSpec((blk, HY_W), lambda j, q, *_: (off + j, u_col), pipeline_mode=pl.Buffered(1)),
        pl.BlockSpec((blk, HY_W), lambda j, q, *_: (off + j, gate_col), pipeline_mode=pl.Buffered(1)),
        pl.BlockSpec((1, HY_W), lambda j, q, *_: (0, 0)),
        mat(), mat(),
        pl.BlockSpec((radix, None, 2, p, HY_W), lambda j, q, *_: (0, q, 0, 0, order)),
    ]
    kwargs = {}
    if prev_out is not None:
        args.append(prev_out)
        in_specs.append(pl.BlockSpec(memory_space=pl.ANY))
        kwargs["input_output_aliases"] = {len(args) - 1: 0}
    return pl.pallas_call(
        functools.partial(_conv_kernel, length=length, p=p, radix=radix),
        grid_spec=pltpu.PrefetchScalarGridSpec(
            num_scalar_prefetch=2,
            grid=(nb // 2, nq),
            in_specs=in_specs,
            out_specs=pl.BlockSpec((blk, HY_W), lambda j, q, *_: (off + j, 0)),
            scratch_shapes=[pltpu.VMEM((blk, HY_W), F32)],
        ),
        out_shape=jax.ShapeDtypeStruct((t_all, HY_W), BF16),
        compiler_params=_cparams("arbitrary", "arbitrary"),
        name="hyena_long_conv",
        **kwargs,
    )(*args)


def _attn_kernel(sink_ref, q_ref, *refs, local):
    if local:
        kp_ref, kc_ref, kn_ref, vp_ref, vc_ref, vn_ref, kx_ref, vx_ref, o_ref = refs
        n = pl.program_id(1)
        last = pl.num_programs(1) - 1
        qi = lax.broadcasted_iota(jnp.int32, (QBLK, QBLK), 0)
        ki = lax.broadcasted_iota(jnp.int32, (QBLK, QBLK), 1)
        ok_prev = (ki >= qi) & (n > 0)
        ok_next = (ki <= qi) & (n < last)
    else:
        kx_ref, vx_ref, o_ref = refs
    scale = HEAD_DIM ** -0.5
    nt = (((1,), (1,)), ((), ()))
    nq = q_ref.shape[0]
    if local:
        bias1 = jnp.concatenate([jnp.where(ok_prev, 0.0, NEG_INF), jnp.zeros((QBLK, QBLK), F32),
                                 jnp.where(ok_next, 0.0, NEG_INF), jnp.zeros((QBLK, kx_ref.shape[0]), F32)], axis=1)
        bias = jnp.concatenate([bias1] * Q_PER_KV, axis=0)
    for kv in range(N_KV_HEADS):
        cs = slice(kv * HEAD_DIM, (kv + 1) * HEAD_DIM)
        heads = range(kv * Q_PER_KV, (kv + 1) * Q_PER_KV)
        qs = jnp.concatenate([q_ref[:, h * HEAD_DIM:(h + 1) * HEAD_DIM] for h in heads], axis=0)
        sink = jnp.concatenate([jnp.full((nq, 1), sink_ref[h], F32) for h in heads], axis=0)
        if local:
            keys = jnp.concatenate([kp_ref[:, cs], kc_ref[:, cs], kn_ref[:, cs], kx_ref[:, cs]], axis=0)
            vals = jnp.concatenate([vp_ref[:, cs], vc_ref[:, cs], vn_ref[:, cs], vx_ref[:, cs]], axis=0)
        else:
            keys, vals = kx_ref[:, cs], vx_ref[:, cs]
        s = lax.dot_general(qs, keys, nt, preferred_element_type=F32) * scale
        if local:
            s = s + bias
        mx = jnp.maximum(jnp.max(s, axis=-1, keepdims=True), sink)
        p = jnp.exp(s - mx)
        den = jnp.sum(p, axis=-1, keepdims=True) + jnp.exp(sink - mx)
        out = _dot(p.astype(BF16), vals) / den
        for g, h in enumerate(heads):
            o_ref[:, h * HEAD_DIM:(h + 1) * HEAD_DIM] = out[g * nq:(g + 1) * nq]


def _attention(q, k, v, sink, nb, n, c):
    t_all = q.shape[0]
    nqb = n // QBLK
    ctx_blk0 = nb * n // c
    kvspec = lambda fn: pl.BlockSpec((QBLK, KV_W), fn)
    prev = lambda b, j: (b * nqb + jnp.maximum(j - 1, 0), 0)
    cur = lambda b, j: (b * nqb + j, 0)
    nxt = lambda b, j: (b * nqb + jnp.minimum(j + 1, nqb - 1), 0)
    ctxs = pl.BlockSpec((c, KV_W), lambda b, j: (ctx_blk0 + b, 0))
    smem = pl.BlockSpec(memory_space=pltpu.SMEM)
    lat = pl.pallas_call(
        functools.partial(_attn_kernel, local=True),
        grid=(nb, nqb),
        in_specs=[smem, pl.BlockSpec((QBLK, ATT_W), cur),
                  kvspec(prev), kvspec(cur), kvspec(nxt), kvspec(prev), kvspec(cur), kvspec(nxt), ctxs, ctxs],
        out_specs=pl.BlockSpec((QBLK, ATT_W), cur),
        out_shape=jax.ShapeDtypeStruct((t_all, ATT_W), F32),
        compiler_params=_cparams("arbitrary", "arbitrary"),
        name="banded_attention",
    )(sink, q, k, k, k, v, v, v, k, v)
    ctx1 = pl.BlockSpec((c, KV_W), lambda b: (ctx_blk0 + b, 0))
    return pl.pallas_call(
        lambda s, qq, kx, vx, prev_o, o: _attn_kernel(s, qq, kx, vx, o, local=False),
        grid=(nb,),
        in_specs=[smem, pl.BlockSpec((c, ATT_W), lambda b: (ctx_blk0 + b, 0)), ctx1, ctx1,
                  pl.BlockSpec(memory_space=pl.ANY)],
        out_specs=pl.BlockSpec((c, ATT_W), lambda b: (ctx_blk0 + b, 0)),
        out_shape=jax.ShapeDtypeStruct((t_all, ATT_W), F32),
        input_output_aliases={4: 0},
        compiler_params=_cparams("arbitrary"),
        name="context_attention",
    )(sink, q, k, v, lat)


def _s5disc_kernel(lre_ref, lim_ref, dt_ref, bre_ref, bim_ref, are_ref, aim_ref, bbre_ref, bbim_ref):
    lam_re = lre_ref[...]
    lam_im = lim_ref[...]
    dt = jnp.exp(dt_ref[...])
    mag = jnp.exp(lam_re * dt)
    ab_re = mag * jnp.cos(lam_im * dt)
    ab_im = mag * jnp.sin(lam_im * dt)
    num_re = ab_re - 1.0
    num_im = ab_im
    den = lam_re * lam_re + lam_im * lam_im
    co_re = (num_re * lam_re + num_im * lam_im) / den
    co_im = (num_im * lam_re - num_re * lam_im) / den
    b_re = bre_ref[...]
    b_im = bim_ref[...]
    are_ref[...] = ab_re
    aim_ref[...] = ab_im
    bbre_ref[...] = co_re * b_re - co_im * b_im
    bbim_ref[...] = co_re * b_im + co_im * b_re


def _s5_discretize(lam_re, lam_im, log_dt, b_re, b_im):
    lead = lam_re.shape[:-1]
    rep = lambda a: jnp.repeat(a.reshape(-1, 1, S5_STATE), S5_CPG, axis=1).reshape(-1, S5_STATE)
    dt = jnp.broadcast_to(log_dt.reshape(-1, 1, 1), (math.prod(lead), S5_CPG, S5_STATE)).reshape(-1, S5_STATE)
    tr = lambda b: jnp.swapaxes(b, -1, -2).reshape(-1, S5_STATE)
    rows = math.prod(lead) * S5_CPG
    shp = jax.ShapeDtypeStruct((rows, S5_STATE), F32)
    ab_re, ab_im, bb_re, bb_im = pl.pallas_call(
        _s5disc_kernel, out_shape=[shp] * 4, name="s5_discretize",
    )(rep(lam_re), rep(lam_im), dt, tr(b_re), tr(b_im))
    full = lead + (S5_CPG, S5_STATE)
    return (ab_re.reshape(full)[..., 0, :], ab_im.reshape(full)[..., 0, :],
            bb_re.reshape(full), bb_im.reshape(full))


def _s5_kernel(uf_ref, ub_ref, bmat_ref, cmat_ref, a_ref, yf_ref, yb_ref, st_ref, bu_ref, *, tc, nbatch):
    i = pl.program_id(0)
    gp = S5_GROUPS * S5_STATE

    @pl.when(i == 0)
    def _():
        st_ref[...] = jnp.zeros(st_ref.shape, F32)

    for d, (u_ref, y_ref) in enumerate(((uf_ref, yf_ref), (ub_ref, yb_ref))):
        bu_ref[...] = _dot(u_ref[...].astype(BF16), bmat_ref[d])
        ar = a_ref[d, 0]
        ai = a_ref[d, 1]

        def body(j, carry, d=d, ar=ar, ai=ai):
            hr, hi = carry
            t = j if d == 0 else tc - 1 - j
            r0 = pl.multiple_of(t * nbatch, nbatch)
            nr = ar * hr - ai * hi + bu_ref[pl.ds(r0, nbatch), 0:gp]
            ni = ar * hi + ai * hr + bu_ref[pl.ds(r0, nbatch), gp:2 * gp]
            bu_ref[pl.ds(r0, nbatch), 0:gp] = nr
            bu_ref[pl.ds(r0, nbatch), gp:2 * gp] = ni
            return nr, ni

        hr, hi = lax.fori_loop(0, tc, body, (st_ref[d, 0], st_ref[d, 1]))
        st_ref[d, 0] = hr
        st_ref[d, 1] = hi
        y_ref[...] = _dot(bu_ref[...].astype(BF16), cmat_ref[d])


def _s5_scan(u_tm, bmat, cmat, a_bc, nb, n, c):
    tc = S5_TC
    rows = tc * nb
    nl, nc = n // tc, c // tc
    gp = S5_GROUPS * S5_STATE
    fwd = lambda i: (jnp.where(i < nc, nl + i, i - nc), 0)
    bwd = lambda i: (nl + nc - 1 - i, 0)
    full = lambda arr: pl.BlockSpec(arr.shape, lambda i: (0,) * arr.ndim)
    shp = jax.ShapeDtypeStruct(u_tm.shape, F32)
    return pl.pallas_call(
        functools.partial(_s5_kernel, tc=tc, nbatch=nb),
        grid=(nl + nc,),
        in_specs=[pl.BlockSpec((rows, S5_W), fwd), pl.BlockSpec((rows, S5_W), bwd),
                  full(bmat), full(cmat), full(a_bc)],
        out_specs=[pl.BlockSpec((rows, S5_W), fwd), pl.BlockSpec((rows, S5_W), bwd)],
        out_shape=[shp, shp],
        scratch_shapes=[pltpu.VMEM((2, 2, nb, gp), F32), pltpu.VMEM((rows, 2 * gp), F32)],
        compiler_params=_cparams("arbitrary"),
        name="s5_scan",
    )(u_tm, u_tm, bmat, cmat, a_bc)


def _s5_matrices(ab_re, ab_im, bb_re, bb_im, c_re, c_im, nb):
    eye = jnp.eye(S5_GROUPS, dtype=F32)
    bd_in = lambda b: jnp.einsum("dgcp,gh->dgchp", b, eye).reshape(2, S5_W, S5_GROUPS * S5_STATE)
    bd_out = lambda cc: jnp.einsum("dgcp,gh->dgphc", cc, eye).reshape(2, S5_GROUPS * S5_STATE, S5_W)
    bmat = jnp.concatenate([bd_in(bb_re), bd_in(bb_im)], axis=-1).astype(BF16)
    cmat = jnp.concatenate([bd_out(c_re), -bd_out(c_im)], axis=1).astype(BF16)
    a = jnp.stack([ab_re.reshape(2, -1), ab_im.reshape(2, -1)], axis=1)
    a_bc = jnp.broadcast_to(a[:, :, None, :], (2, 2, nb, S5_GROUPS * S5_STATE))
    return bmat, cmat, a_bc


def _out_kernel(x_ref, hy_ref, att_ref, y_ref, u_ref, d_ref, gluw_ref, glub_ref, ng_ref, wout_ref,
                g1_ref, sh2_ref, sc2_ref, n2g_ref, rw_ref, rb_ref, tri_ref,
                xo_ref, h2_ref, eidx_ref, gate_ref, rank_ref, cnt_ref, carry_ref):
    i = pl.program_id(0)

    @pl.when(i == 0)
    def _():
        carry_ref[...] = jnp.zeros(carry_ref.shape, F32)

    y = y_ref[...] + u_ref[...] * d_ref[...]
    g = jax.nn.gelu(y)
    s5 = g * jax.nn.sigmoid(_dot(g.astype(BF16), gluw_ref[...]) + glub_ref[...])

    def nrm(part):
        return part * lax.rsqrt(jnp.mean(part * part, axis=-1, keepdims=True) + EPS)

    mix = jnp.concatenate([nrm(hy_ref[...].astype(F32)), nrm(att_ref[...]), nrm(s5)], axis=-1) * ng_ref[...]
    x = x_ref[...] + g1_ref[...] * _dot(mix.astype(BF16), wout_ref[...])
    xo_ref[...] = x
    h2 = _rms(x, n2g_ref[...]) * (1.0 + sc2_ref[...]) + sh2_ref[...]
    h2_ref[...] = h2

    logits = lax.dot_general(rw_ref[...], h2, (((1,), (1,)), ((), ())), precision=HIGHEST,
                             preferred_element_type=F32)
    e = jnp.exp(logits - jnp.max(logits, axis=0, keepdims=True))
    probs = e / jnp.sum(e, axis=0, keepdims=True)
    sel = probs + rb_ref[...]
    rows = [sel[r:r + 1, :] for r in range(N_EXPERTS)]
    best = None
    for grp in range(N_EXPERT_GROUPS):
        a = rows[grp * EXPERTS_PER_GROUP:(grp + 1) * EXPERTS_PER_GROUP]
        score = None
        for p0 in range(EXPERTS_PER_GROUP):
            for p1 in range(p0 + 1, EXPERTS_PER_GROUP):
                pair = a[p0] + a[p1]
                score = pair if score is None else jnp.maximum(score, pair)
        if best is None:
            best, gidx = score, jnp.zeros(score.shape, jnp.int32)
        else:
            better = score > best
            gidx = jnp.where(better, grp, gidx)
            best = jnp.where(better, score, best)
    ing = []
    for j in range(EXPERTS_PER_GROUP):
        v = rows[j]
        for grp in range(1, N_EXPERT_GROUPS):
            v = jnp.where(gidx == grp, rows[grp * EXPERTS_PER_GROUP + j], v)
        ing.append(v)
    first_v, first_i = ing[0], jnp.zeros(gidx.shape, jnp.int32)
    for j in range(1, EXPERTS_PER_GROUP):
        better = ing[j] > first_v
        first_i = jnp.where(better, j, first_i)
        first_v = jnp.where(better, ing[j], first_v)
    second_v, second_i = None, None
    for j in range(EXPERTS_PER_GROUP):
        cand = jnp.where(first_i == j, -jnp.inf, ing[j])
        if second_v is None:
            second_v, second_i = cand, jnp.zeros(gidx.shape, jnp.int32)
        else:
            better = cand > second_v
            second_i = jnp.where(better, j, second_i)
            second_v = jnp.where(better, cand, second_v)
    e0 = gidx * EXPERTS_PER_GROUP + first_i
    e1 = gidx * EXPERTS_PER_GROUP + second_i
    eid = lax.broadcasted_iota(jnp.int32, probs.shape, 0)
    oh0 = eid == e0
    oh1 = eid == e1
    p0v = jnp.sum(jnp.where(oh0, probs, 0.0), axis=0, keepdims=True)
    p1v = jnp.sum(jnp.where(oh1, probs, 0.0), axis=0, keepdims=True)
    tot = p0v + p1v
    member = jnp.where(oh0 | oh1, 1.0, 0.0)
    before = _dot(member.astype(BF16), tri_ref[...]) + carry_ref[...]
    r0 = jnp.sum(jnp.where(oh0, before, 0.0), axis=0, keepdims=True)
    r1 = jnp.sum(jnp.where(oh1, before, 0.0), axis=0, keepdims=True)
    carry = carry_ref[...] + jnp.sum(member, axis=1, keepdims=True)
    carry_ref[...] = carry
    cnt_ref[...] = carry
    eidx_ref[...] = jnp.concatenate([e0, e1], axis=0)
    gate_ref[...] = jnp.concatenate([p0v / tot, p1v / tot], axis=0)
    rank_ref[...] = jnp.concatenate([r0, r1], axis=0).astype(jnp.int32)


def _mixer_out(x_all, hy, att, y_tok, u, s5_d, glu_w_bf, glu_b, mix_g, w_out_bf, mod4, n2g,
               router_wt, router_b, nb, n):
    t_all = x_all.shape[0]
    n_lat_tiles = nb * n // TM
    tpb = n // TM
    mrow = lambda i: jnp.where(i < n_lat_tiles, i // tpb, nb)
    row = lambda w: pl.BlockSpec((TM, w), lambda i: (i, 0))
    full = lambda arr: pl.BlockSpec(arr.shape, lambda i: (0,) * arr.ndim)
    modspec = lambda k: pl.BlockSpec((None, None, 1, D_MODEL), lambda i, k=k: (mrow(i), k, 0, 0))
    tri = (lax.broadcasted_iota(jnp.int32, (TM, TM), 0) < lax.broadcasted_iota(jnp.int32, (TM, TM), 1)).astype(BF16)
    v1 = lambda a: a.reshape(1, -1)
    ins = [v1(s5_d), glu_w_bf, v1(glu_b), v1(mix_g), w_out_bf]
    tail = [v1(n2g), router_wt, router_b.reshape(N_EXPERTS, 1), tri]
    tok = lambda: pl.BlockSpec((2, TM), lambda i: (0, i))
    return pl.pallas_call(
        _out_kernel,
        grid=(t_all // TM,),
        in_specs=[row(D_MODEL), row(HY_W), row(ATT_W), row(S5_W), row(S5_W)] + [full(a) for a in ins]
                 + [modspec(2), modspec(3), modspec(4)] + [full(a) for a in tail],
        out_specs=[row(D_MODEL), row(D_MODEL), tok(), tok(), tok(),
                   pl.BlockSpec((N_EXPERTS, 1), lambda i: (0, 0))],
        out_shape=[
            jax.ShapeDtypeStruct((t_all, D_MODEL), F32),
            jax.ShapeDtypeStruct((t_all, D_MODEL), F32),
            jax.ShapeDtypeStruct((2, t_all), jnp.int32),
            jax.ShapeDtypeStruct((2, t_all), F32),
            jax.ShapeDtypeStruct((2, t_all), jnp.int32),
            jax.ShapeDtypeStruct((N_EXPERTS, 1), F32),
        ],
        scratch_shapes=[pltpu.VMEM((N_EXPERTS, 1), F32)],
        compiler_params=_cparams("arbitrary"),
        name="mixer_out_router",
    )(x_all, hy, att, y_tok, u, *ins, mod4, mod4, mod4, *tail)


def _dispatch_kernel(pos_ref, cnt_ref, pstart_ref, padded_ref, h2_ref, xb_ref, zrow_ref, sem, zsem):
    i = pl.program_id(0)
    base = i * (2 * TM)

    def issue(r, carry):
        for k in range(2):
            slot = pos_ref[base + k * TM + r]
            pltpu.make_async_copy(h2_ref.at[pl.ds(r, 1)], xb_ref.at[pl.ds(slot, 1)], sem).start()
        return carry

    lax.fori_loop(0, TM, issue, 0, unroll=8)

    @pl.when(i == 0)
    def _():
        zrow_ref[...] = jnp.zeros(zrow_ref.shape, F32)
        for e in range(N_EXPERTS):
            lo = pstart_ref[e] + cnt_ref[e]
            hi = pstart_ref[e] + padded_ref[e]

            def fill(slot, carry):
                pltpu.make_async_copy(zrow_ref, xb_ref.at[pl.ds(slot, 1)], zsem).start()
                return carry

            lax.fori_loop(lo, hi, fill, 0)

            def fill_wait(slot, carry):
                pltpu.make_async_copy(zrow_ref, xb_ref.at[pl.ds(slot, 1)], zsem).wait()
                return carry

            lax.fori_loop(lo, hi, fill_wait, 0)

    for k in range(2):
        pltpu.make_async_copy(h2_ref, xb_ref.at[pl.ds(0, TM)], sem).wait()


def _dispatch(pos_flat, cnt, pstart, padded, h2, n_slots):
    t_all = h2.shape[0]
    return pl.pallas_call(
        _dispatch_kernel,
        grid_spec=pltpu.PrefetchScalarGridSpec(
            num_scalar_prefetch=4,
            grid=(t_all // TM,),
            in_specs=[pl.BlockSpec((TM, D_MODEL), lambda i, *_: (i, 0))],
            out_specs=pl.BlockSpec(memory_space=pl.ANY),
            scratch_shapes=[pltpu.VMEM((1, D_MODEL), F32), pltpu.SemaphoreType.DMA(()), pltpu.SemaphoreType.DMA(())],
        ),
        out_shape=jax.ShapeDtypeStruct((n_slots, D_MODEL), F32),
        compiler_params=_cparams("arbitrary"),
        name="moe_dispatch",
    )(pos_flat, cnt, pstart, padded, h2)


def _ffn_kernel(be_ref, nu_ref, x_ref, wg_ref, wu_ref, wd_ref, o_ref):
    del be_ref

    @pl.when(pl.program_id(0) < nu_ref[0])
    def _():
        x = x_ref[...].astype(BF16)
        a = _dot(x, wg_ref[...])
        u = _dot(x, wu_ref[...])
        hidden = (a * jax.nn.sigmoid(a)) * u
        o_ref[...] = _dot(hidden.astype(BF16), wd_ref[...])


def _expert_ffn(block_e, n_used, xb, wg, wu, wd):
    n_slots = xb.shape[0]
    blk = lambda j, be, nu: jnp.minimum(j, nu[0] - 1)
    wspec = lambda: pl.BlockSpec((None, D_MODEL, D_MODEL), lambda j, be, nu: (be[blk(j, be, nu)], 0, 0))
    return pl.pallas_call(
        _ffn_kernel,
        grid_spec=pltpu.PrefetchScalarGridSpec(
            num_scalar_prefetch=2,
            grid=(n_slots // MOE_ROWS,),
            in_specs=[pl.BlockSpec((MOE_ROWS, D_MODEL), lambda j, be, nu: (blk(j, be, nu), 0)),
                      wspec(), wspec(), wspec()],
            out_specs=pl.BlockSpec((MOE_ROWS, D_MODEL), lambda j, be, nu: (blk(j, be, nu), 0)),
        ),
        out_shape=jax.ShapeDtypeStruct((n_slots, D_MODEL), F32),
        compiler_params=_cparams("arbitrary"),
        name="moe_expert_ffn",
    )(block_e, n_used, xb, wg, wu, wd)


def _combine_kernel(pos_ref, x_ref, gates_ref, g2_ref, yb_ref, xo_ref, buf_ref, sem):
    i = pl.program_id(0)
    n_tiles = pl.num_programs(0)

    def issue_tile(tile, slot):
        base = tile * (2 * TM)

        def body(r, carry):
            for k in range(2):
                src = pos_ref[base + k * TM + r]
                pltpu.make_async_copy(yb_ref.at[pl.ds(src, 1)], buf_ref.at[slot, k, pl.ds(r, 1)],
                                      sem.at[slot]).start()
            return carry

        lax.fori_loop(0, TM, body, 0, unroll=8)

    @pl.when(i == 0)
    def _():
        issue_tile(0, 0)

    @pl.when(i + 1 < n_tiles)
    def _():
        issue_tile(i + 1, (i + 1) % 2)

    slot = i % 2
    for k in range(2):
        pltpu.make_async_copy(yb_ref.at[pl.ds(0, TM)], buf_ref.at[slot, k], sem.at[slot]).wait()
    gates = gates_ref[...]
    y = buf_ref[slot, 0] * gates[:, 0:1] + buf_ref[slot, 1] * gates[:, 1:2]
    xo_ref[...] = x_ref[...] + g2_ref[...] * y


def _combine(pos_flat, x_all, gates_t, mod4, yb, nb, n):
    t_all = x_all.shape[0]
    n_lat_tiles = nb * n // TM
    tpb = n // TM
    mrow = lambda i: jnp.where(i < n_lat_tiles, i // tpb, nb)
    return pl.pallas_call(
        _combine_kernel,
        grid_spec=pltpu.PrefetchScalarGridSpec(
            num_scalar_prefetch=1,
            grid=(t_all // TM,),
            in_specs=[pl.BlockSpec((TM, D_MODEL), lambda i, *_: (i, 0)),
                      pl.BlockSpec((TM, 2), lambda i, *_: (i, 0)),
                      pl.BlockSpec((None, None, 1, D_MODEL), lambda i, *_: (mrow(i), 5, 0, 0)),
                      pl.BlockSpec(memory_space=pl.ANY)],
            out_specs=pl.BlockSpec((TM, D_MODEL), lambda i, *_: (i, 0)),
            scratch_shapes=[pltpu.VMEM((2, 2, TM, D_MODEL), F32), pltpu.SemaphoreType.DMA((2,))],
        ),
        out_shape=jax.ShapeDtypeStruct((t_all, D_MODEL), F32),
        compiler_params=_cparams("arbitrary"),
        name="moe_combine",
    )(pos_flat, x_all, gates_t, mod4, yb)


def _final_kernel(x_ref, g_ref, o_ref):
    o_ref[...] = _rms(x_ref[...], g_ref[...])


def _final_norm(x_all, g, rows):
    return pl.pallas_call(
        _final_kernel,
        grid=(rows // TM,),
        in_specs=[pl.BlockSpec((TM, D_MODEL), lambda i: (i, 0)), pl.BlockSpec((1, D_MODEL), lambda i: (0, 0))],
        out_specs=pl.BlockSpec((TM, D_MODEL), lambda i: (i, 0)),
        out_shape=jax.ShapeDtypeStruct((rows, D_MODEL), F32),
        compiler_params=_cparams("arbitrary"),
        name="final_norm",
    )(x_all, g.reshape(1, D_MODEL))


def _time_major(a, nb, n, c):
    w = a.shape[-1]
    lat = jnp.swapaxes(a[:nb * n].reshape(nb, n, w), 0, 1).reshape(n * nb, w)
    ctx = jnp.swapaxes(a[nb * n:].reshape(nb, c, w), 0, 1).reshape(c * nb, w)
    return jnp.concatenate([lat, ctx], axis=0)


def _token_major(a, nb, n, c):
    w = a.shape[-1]
    lat = jnp.swapaxes(a[:n * nb].reshape(n, nb, w), 0, 1).reshape(nb * n, w)
    ctx = jnp.swapaxes(a[n * nb:].reshape(c, nb, w), 0, 1).reshape(nb * c, w)
    return jnp.concatenate([lat, ctx], axis=0)


def kernel(x, c, ctx, c_ctx, norm1_g, norm2_g, ada_w, ada_b, w_in, w_out, mix_norm_g, hy_conv_w, hy_conv_b, hy_f_w1, hy_f_b1, hy_f_freq1, hy_f_w2, hy_f_b2, hy_f_freq2, hy_f_w3, hy_bias, attn_sink, s5_lam_re, s5_lam_im, s5_log_dt, s5_b_re, s5_b_im, s5_c_re, s5_c_im, s5_d, s5_glu_w, s5_glu_b, router_w, router_b, moe_w_gate, moe_w_up, moe_w_down, final_g):
    nb, n, d = x.shape
    cl = ctx.shape[1]
    depth = w_in.shape[0]
    t_all = nb * (n + cl)

    cc = jnp.zeros((16, d), F32).at[:nb].set(c).at[nb].set(c_ctx)
    mod_all = _modulation(cc, ada_w, ada_b).reshape(depth, 16, 6, 1, d)
    cos, sin = _rope_tables(n)
    ab_re, ab_im, bb_re, bb_im = _s5_discretize(s5_lam_re, s5_lam_im, s5_log_dt, s5_b_re, s5_b_im)

    tabs = {}
    for length in (n, cl):
        twr, twi, er, ei = _dft_tables(length)
        tabs[length] = ((twr, twi, er, ei), (twr, twi, er.astype(BF16), ei.astype(BF16)))

    router_wt = router_w.T
    expert_ids = jnp.arange(N_EXPERTS, dtype=jnp.int32)
    n_slots = (-(-(2 * t_all) // MOE_ROWS) + N_EXPERTS) * MOE_ROWS
    x_all = jnp.concatenate([x.reshape(nb * n, d), ctx.reshape(nb * cl, d)], axis=0)

    for l in range(depth):
        mod4 = mod_all[l]
        hy, q, k, v, u = _input_proj(x_all, norm1_g[l], mod4, w_in[l].astype(BF16), cos, sin, nb, n)

        zc = _short_conv(hy, hy_conv_w[l], hy_conv_b[l], nb, n, cl)
        hy_out = None
        z1 = None
        specs = {}
        for length in (n, cl):
            specs[length] = _filter_spectrum(length, tabs[length][0], hy_f_w1[l], hy_f_b1[l], hy_f_freq1[l],
                                             hy_f_w2[l], hy_f_b2[l], hy_f_freq2[l], hy_f_w3[l]).astype(BF16)
        for length, off in ((n, 0), (cl, nb * n)):
            z1 = _long_conv(zc, 0, zc, 1, hy_bias[l, 0], specs[length], 0, tabs[length][1], length, off, nb, z1)
        for length, off in ((n, 0), (cl, nb * n)):
            hy_out = _long_conv(z1, 0, zc, 2, hy_bias[l, 1], specs[length], 1, tabs[length][1], length, off, nb,
                                hy_out)

        att = _attention(q, k, v, attn_sink[l], nb, n, cl)

        bmat, cmat, a_bc = _s5_matrices(ab_re[l], ab_im[l], bb_re[l], bb_im[l], s5_c_re[l], s5_c_im[l], nb)
        y_f, y_b = _s5_scan(_time_major(u, nb, n, cl), bmat, cmat, a_bc, nb, n, cl)
        y_tok = _token_major(y_f + y_b, nb, n, cl)

        x_all, h2, eidx, gates, rank, counts = _mixer_out(
            x_all, hy_out, att, y_tok, u, s5_d[l], s5_glu_w[l].astype(BF16), s5_glu_b[l], mix_norm_g[l],
            w_out[l].astype(BF16), mod4, norm2_g[l], router_wt, router_b, nb, n)

        cnt = counts.reshape(N_EXPERTS).astype(jnp.int32)
        padded = (cnt + MOE_ROWS - 1) // MOE_ROWS * MOE_ROWS
        ends = jnp.sum(jnp.where(expert_ids[None, :] <= expert_ids[:, None], padded[None, :], 0), axis=1)
        pstart = ends - padded
        pos = rank + jnp.sum(jnp.where(eidx[None] == expert_ids[:, None, None], pstart[:, None, None], 0), axis=0)
        pos_flat = jnp.swapaxes(pos.reshape(2, t_all // TM, TM), 0, 1).reshape(-1)
        n_blocks = n_slots // MOE_ROWS
        block_start = jnp.arange(n_blocks, dtype=jnp.int32) * MOE_ROWS
        block_e = jnp.minimum(jnp.sum((ends[None, :] <= block_start[:, None]).astype(jnp.int32), axis=1),
                              N_EXPERTS - 1)
        n_used = (ends[-1] // MOE_ROWS).reshape(1)

        xb = _dispatch(pos_flat, cnt, pstart, padded, h2, n_slots)
        yb = _expert_ffn(block_e, n_used, xb, moe_w_gate[l].astype(BF16), moe_w_up[l].astype(BF16),
                         moe_w_down[l].astype(BF16))
        x_all = _combine(pos_flat, x_all, gates.T, mod4, yb, nb, n)

    return _final_norm(x_all, final_g, nb * n).reshape(nb, n, d)
```

```python
import functools
import math

import jax
import jax.numpy as jnp
from jax import lax
from jax.experimental import pallas as pl
from jax.experimental.pallas import tpu as pltpu

F32 = jnp.float32
BF16 = jnp.bfloat16
HIGHEST = lax.Precision.HIGHEST

D_MODEL = 1024
GRID_W = 64
HY_W = 256
ATT_W = 512
S5_W = 256
HEAD_DIM = 64
N_HEADS = 8
N_KV_HEADS = 2
Q_PER_KV = 4
KV_W = 128
WINDOW = 128
QBLK = 128
ROPE_BASE = 10000.0
NEG_INF = -1e30
HY_BANDS = 16
HY_DECAY_MIN = math.log(1e-2) / 1.5
HY_DECAY_MAX = math.log(1e-2) / 0.3
S5_CPG = 16
S5_GROUPS = 16
S5_STATE = 64
N_EXPERTS = 16
N_EXPERT_GROUPS = 4
EXPERTS_PER_GROUP = 4
HY_END = 768
Q_END = 1280
K_END = 1408
V_END = 1536
IN_W = 1792
EPS = 1e-6

TM = 256
MOE_ROWS = 256
S5_TC = 64
DFT_P = 512
VMEM_LIMIT = 56 * 1024 * 1024


def _cparams(*sem):
    return pltpu.CompilerParams(dimension_semantics=sem, vmem_limit_bytes=VMEM_LIMIT)


def _dot(a, b):
    return jnp.dot(a, b, preferred_element_type=F32)


def _mod_kernel(c_ref, w_ref, b_ref, o_ref):
    c = c_ref[...]
    a = c * jax.nn.sigmoid(c)
    o_ref[...] = _dot(a.astype(BF16), w_ref[...].astype(BF16)) + b_ref[...]


def _modulation(cc, ada_w, ada_b):
    depth, d, w6 = ada_w.shape
    nb = 1536
    return pl.pallas_call(
        _mod_kernel,
        grid=(depth, w6 // nb),
        in_specs=[
            pl.BlockSpec((16, d), lambda l, j: (0, 0)),
            pl.BlockSpec((None, d, nb), lambda l, j: (l, 0, j)),
            pl.BlockSpec((None, 1, nb), lambda l, j: (l, 0, j)),
        ],
        out_specs=pl.BlockSpec((None, 16, nb), lambda l, j: (l, 0, j)),
        out_shape=jax.ShapeDtypeStruct((depth, 16, w6), F32),
        compiler_params=_cparams("arbitrary", "arbitrary"),
        name="adaln_mod",
    )(cc, ada_w, ada_b.reshape(depth, 1, w6))


def _rms(x, g):
    return x * lax.rsqrt(jnp.mean(x * x, axis=-1, keepdims=True) + EPS) * g


def _in_kernel(x_ref, g_ref, sh_ref, sc_ref, w_ref, cos_ref, sin_ref,
               hy_ref, q_ref, k_ref, v_ref, u_ref):
    h = _rms(x_ref[...], g_ref[...]) * (1.0 + sc_ref[...]) + sh_ref[...]
    p = _dot(h.astype(BF16), w_ref[...])
    hy_ref[...] = p[:, :HY_END]
    cos = cos_ref[...]
    sin = sin_ref[...]
    lane = lax.broadcasted_iota(jnp.int32, cos.shape, 1)
    first = (lane % 32) < 16

    def rope(z):
        swapped = jnp.where(first, pltpu.roll(z, 112, 1), pltpu.roll(z, 16, 1))
        return z * cos + swapped * sin

    for j in range(ATT_W // 128):
        q_ref[:, j * 128:(j + 1) * 128] = rope(p[:, HY_END + j * 128:HY_END + (j + 1) * 128]).astype(BF16)
    k_ref[...] = rope(p[:, Q_END:K_END]).astype(BF16)
    v_ref[...] = p[:, K_END:V_END].astype(BF16)
    u_ref[...] = p[:, V_END:]


def _rope_tables(n):
    quarter = HEAD_DIM // 4
    inv_freq = ROPE_BASE ** (-jnp.arange(quarter, dtype=F32) / quarter)
    t = jnp.arange(n)
    rows = (t // GRID_W).astype(F32)
    cols = (t % GRID_W).astype(F32)
    ang_r = rows[:, None] * inv_freq[None, :]
    ang_c = cols[:, None] * inv_freq[None, :]
    cos64 = jnp.concatenate([jnp.cos(ang_r)] * 2 + [jnp.cos(ang_c)] * 2, axis=-1)
    sin64 = jnp.concatenate([-jnp.sin(ang_r), jnp.sin(ang_r), -jnp.sin(ang_c), jnp.sin(ang_c)], axis=-1)
    cos = jnp.concatenate([jnp.tile(cos64, (1, 2)), jnp.ones((TM, 128), F32)], axis=0)
    sin = jnp.concatenate([jnp.tile(sin64, (1, 2)), jnp.zeros((TM, 128), F32)], axis=0)
    return cos, sin


def _input_proj(x_all, g, mod4, w_in_bf, cos, sin, nb, n):
    t_all = x_all.shape[0]
    n_lat_tiles = nb * n // TM
    tpb = n // TM

    def mrow(i):
        return jnp.where(i < n_lat_tiles, i // tpb, nb)

    def trow(i):
        return jnp.where(i < n_lat_tiles, i % tpb, tpb)

    row = lambda w: pl.BlockSpec((TM, w), lambda i: (i, 0))
    return pl.pallas_call(
        _in_kernel,
        grid=(t_all // TM,),
        in_specs=[
            row(D_MODEL),
            pl.BlockSpec((1, D_MODEL), lambda i: (0, 0)),
            pl.BlockSpec((None, None, 1, D_MODEL), lambda i: (mrow(i), 0, 0, 0)),
            pl.BlockSpec((None, None, 1, D_MODEL), lambda i: (mrow(i), 1, 0, 0)),
            pl.BlockSpec((D_MODEL, IN_W), lambda i: (0, 0)),
            pl.BlockSpec((TM, 128), lambda i: (trow(i), 0)),
            pl.BlockSpec((TM, 128), lambda i: (trow(i), 0)),
        ],
        out_specs=[row(HY_END), row(ATT_W), row(KV_W), row(KV_W), row(S5_W)],
        out_shape=[
            jax.ShapeDtypeStruct((t_all, HY_END), F32),
            jax.ShapeDtypeStruct((t_all, ATT_W), BF16),
            jax.ShapeDtypeStruct((t_all, KV_W), BF16),
            jax.ShapeDtypeStruct((t_all, KV_W), BF16),
            jax.ShapeDtypeStruct((t_all, S5_W), F32),
        ],
        compiler_params=_cparams("arbitrary"),
        name="norm_mod_inproj",
    )(x_all, g.reshape(1, D_MODEL), mod4, mod4, w_in_bf, cos, sin)


def _sconv_kernel(z_ref, w_ref, b_ref, o_ref):
    z = z_ref[...]
    length = z.shape[0]
    row = lax.broadcasted_iota(jnp.int32, z.shape, 0)
    zm = jnp.where(row == 0, 0.0, pltpu.roll(z, 1, 0))
    zp = jnp.where(row == length - 1, 0.0, pltpu.roll(z, length - 1, 0))
    o_ref[...] = (b_ref[...] + zm * w_ref[0:1, :] + z * w_ref[1:2, :] + zp * w_ref[2:3, :]).astype(o_ref.dtype)


def _short_conv(hy, conv_w, conv_b, nb, n, c):
    t_all = hy.shape[0]
    out = None
    for length, off in ((n, 0), (c, nb * n // c)):
        kwargs = {}
        args = [hy, conv_w, conv_b.reshape(1, HY_END)]
        in_specs = [
            pl.BlockSpec((length, 256), lambda b, j, off=off: (off + b, j)),
            pl.BlockSpec((3, 256), lambda b, j: (0, j)),
            pl.BlockSpec((1, 256), lambda b, j: (0, j)),
        ]
        kern = _sconv_kernel
        if out is not None:
            args.append(out)
            in_specs.append(pl.BlockSpec(memory_space=pl.ANY))
            kwargs["input_output_aliases"] = {3: 0}
            kern = lambda z, w, b, prev, o: _sconv_kernel(z, w, b, o)
        out = pl.pallas_call(
            kern,
            grid=(nb, 3),
            in_specs=in_specs,
            out_specs=pl.BlockSpec((length, 256), lambda b, j, off=off: (off + b, j)),
            out_shape=jax.ShapeDtypeStruct((t_all, HY_END), BF16),
            compiler_params=_cparams("arbitrary", "arbitrary"),
            name="hyena_short_conv",
            **kwargs,
        )(*args)
    return out


def _dft_plan(length):
    p = min(DFT_P, length)
    return p, 2 * length // p


def _dft_tables(length):
    p, na = _dft_plan(length)
    m = 2 * length
    a = jnp.arange(na, dtype=jnp.int32)
    ang_w = (2.0 * math.pi / na) * ((a[:, None] * a[None, :]) % na).astype(F32)
    twr, twi = jnp.cos(ang_w), -jnp.sin(ang_w)
    r = jnp.arange(p, dtype=jnp.int32)
    k = a[:, None, None] + na * r[None, :, None]
    ang = (2.0 * math.pi / m) * ((k * r[None, None, :]) % m).astype(F32)
    er, ei = jnp.cos(ang), -jnp.sin(ang)
    return twr, twi, er, ei


def _filter_kernel(twr_ref, twi_ref, w1_ref, b1_ref, f1_ref, w2_ref, b2_ref, f2_ref, w3_ref,
                   bands_ref, decay_ref, er_ref, ei_ref, h_ref, kern_ref, norm_ref, *, length, p, na):
    ka = pl.program_id(0)
    m = 2 * length
    hdot = functools.partial(jnp.dot, precision=HIGHEST, preferred_element_type=F32)

    @pl.when(ka == 0)
    def _():
        norm = jnp.zeros((1, 2 * HY_W), F32)
        for a in range(na):
            n_idx = a * p + lax.broadcasted_iota(jnp.int32, (p, 1), 0)
            is_fwd = n_idx < length
            pos = jnp.where(is_fwd, n_idx, m - n_idx).astype(F32)
            t = pos / float(max(length - 1, 1))
            ang = (2.0 * math.pi / length) * pos * bands_ref[...]
            pre = (t * w1_ref[0:1, :] + hdot(jnp.cos(ang), w1_ref[1:1 + HY_BANDS, :])
                   + hdot(-jnp.sin(ang), w1_ref[1 + HY_BANDS:, :]) + b1_ref[...])
            hid = jnp.sin(f1_ref[...] * pre)
            hid = jnp.sin(f2_ref[...] * (hdot(hid, w2_ref[...]) + b2_ref[...]))
            taps = hdot(hid, w3_ref[...])
            wnd = jnp.exp(-t * decay_ref[...])
            live = n_idx != length
            for o in range(2):
                fwd = taps[:, o * 2 * HY_W:o * 2 * HY_W + HY_W]
                bwd = taps[:, o * 2 * HY_W + HY_W:(o + 1) * 2 * HY_W]
                kern = jnp.where(live, jnp.where(is_fwd, fwd, bwd) * wnd, 0.0)
                kern_ref[a, :, o * HY_W:(o + 1) * HY_W] = kern
            norm = norm + jnp.sum(jnp.abs(kern_ref[a]), axis=0, keepdims=True)
        norm_ref[...] = norm

    gr = jnp.zeros((p, 2 * HY_W), F32)
    gi = jnp.zeros((p, 2 * HY_W), F32)
    for a in range(na):
        slab = kern_ref[a]
        gr = gr + twr_ref[ka, a] * slab
        gi = gi + twi_ref[ka, a] * slab
    er = er_ref[...]
    ei = ei_ref[...]
    grb = gr.astype(BF16)
    gib = gi.astype(BF16)
    inv = 1.0 / norm_ref[...]
    h_ref[0] = ((_dot(er, grb) - _dot(ei, gib)) * inv).astype(h_ref.dtype)
    h_ref[1] = ((_dot(ei, grb) + _dot(er, gib)) * inv).astype(h_ref.dtype)


def _filter_spectrum(length, tables, w1, b1, f1, w2, b2, f2, w3):
    p, na = _dft_plan(length)
    twr, twi, er, ei = tables
    bands = jnp.linspace(1e-4, HY_BANDS - 1, HY_BANDS, dtype=F32).reshape(1, HY_BANDS)
    decay = jnp.abs(jnp.linspace(HY_DECAY_MIN, HY_DECAY_MAX, HY_W, dtype=F32)).reshape(1, HY_W)
    full = lambda arr: pl.BlockSpec(arr.shape, lambda ka, *_: (0,) * arr.ndim)
    vec = lambda v: v.reshape(1, -1)
    ins = [w1, vec(b1), vec(f1), w2, vec(b2), vec(f2), w3, bands, decay]
    return pl.pallas_call(
        functools.partial(_filter_kernel, length=length, p=p, na=na),
        grid_spec=pltpu.PrefetchScalarGridSpec(
            num_scalar_prefetch=2,
            grid=(na,),
            in_specs=[full(x) for x in ins] + [
                pl.BlockSpec((None, p, p), lambda ka, *_: (ka, 0, 0)),
                pl.BlockSpec((None, p, p), lambda ka, *_: (ka, 0, 0)),
            ],
            out_specs=pl.BlockSpec((None, 2, p, 2 * HY_W), lambda ka, *_: (ka, 0, 0, 0)),
            scratch_shapes=[pltpu.VMEM((na, p, 2 * HY_W), F32), pltpu.VMEM((1, 2 * HY_W), F32)],
        ),
        out_shape=jax.ShapeDtypeStruct((na, 2, p, 2 * HY_W), BF16),
        compiler_params=_cparams("arbitrary"),
        name="hyena_filter_spectrum",
    )(twr, twi, *[vec(x) if x.ndim == 1 else x for x in ins], er, ei)


def _cadd(x, y):
    return x[0] + y[0], x[1] + y[1]


def _csub(x, y):
    return x[0] - y[0], x[1] - y[1]


def _radix_mix(parts, sign):
    if len(parts) == 1:
        return parts
    p0, p1, p2, p3 = parts
    t0, t1, t2, t3 = _cadd(p0, p2), _csub(p0, p2), _cadd(p1, p3), _csub(p1, p3)
    it3 = (-t3[1], t3[0])
    if sign < 0:
        return [_cadd(t0, t2), _csub(t1, it3), _csub(t0, t2), _cadd(t1, it3)]
    return [_cadd(t0, t2), _cadd(t1, it3), _csub(t0, t2), _csub(t1, it3)]


def _conv_kernel(twr_ref, twi_ref, u_ref, gate_ref, bias_ref, er_ref, ei_ref, h_ref, *rest, length, p, radix):
    o_ref, acc_ref = rest[-2], rest[-1]
    q = pl.program_id(1)
    nz = length // p
    tdot = lambda x, y: lax.dot_general(x, y, (((0,), (0,)), ((), ())), preferred_element_type=F32)

    parts = [None] * radix
    for a in range(nz):
        wr = twr_ref[q, a]
        wi = twi_ref[q, a]
        zr = u_ref[a * p:(a + 1) * p, :].astype(F32)
        zi = u_ref[length + a * p:length + (a + 1) * p, :].astype(F32)
        term = (wr * zr - wi * zi, wr * zi + wi * zr)
        parts[a % radix] = term if parts[a % radix] is None else _cadd(parts[a % radix], term)
    zero = jnp.zeros((p, o_ref.shape[-1]), F32)
    parts = [(zero, zero) if t is None else t for t in parts]
    g = _radix_mix(parts, -1)

    v = []
    for m in range(radix):
        grb = g[m][0].astype(BF16)
        gib = g[m][1].astype(BF16)
        er = er_ref[m]
        ei = ei_ref[m]
        sr = _dot(er, grb) - _dot(ei, gib)
        si = _dot(ei, grb) + _dot(er, gib)
        hr = h_ref[m, 0].astype(F32)
        hi = h_ref[m, 1].astype(F32)
        yr = (sr * hr - si * hi).astype(BF16)
        yi = (sr * hi + si * hr).astype(BF16)
        v.append((tdot(er, yr) + tdot(ei, yi), tdot(er, yi) - tdot(ei, yr)))
    qs = _radix_mix(v, +1)

    @pl.when(q == 0)
    def _():
        acc_ref[...] = jnp.zeros(acc_ref.shape, F32)

    scale = 1.0 / (2 * length)
    for a in range(nz):
        wr = twr_ref[q, a] * scale
        wi = twi_ref[q, a] * scale
        vr, vi = qs[a % radix]
        acc_ref[a * p:(a + 1) * p, :] += wr * vr + wi * vi
        acc_ref[length + a * p:length + (a + 1) * p, :] += wr * vi - wi * vr

    @pl.when(q == pl.num_programs(1) - 1)
    def _():
        u = u_ref[...].astype(F32)
        o_ref[...] = (gate_ref[...].astype(F32) * (acc_ref[...] + u * bias_ref[...])).astype(o_ref.dtype)


def _long_conv(u_arr, u_col, gate_arr, gate_col, bias, spec, order, tables_bf, length, row_off, nb, prev_out):
    p, na = _dft_plan(length)
    radix = 4 if na % 4 == 0 and na >= 8 else 1
    nq = na // radix
    twr, twi, er, ei = tables_bf
    t_all = u_arr.shape[0]
    blk = 2 * length
    off = row_off // blk
    mat = lambda: pl.BlockSpec((radix, None, p, p), lambda j, q, *_: (0, q, 0, 0))
    args = [twr, twi, u_arr, gate_arr, bias.reshape(1, HY_W), er.reshape(radix, nq, p, p),
            ei.reshape(radix, nq, p, p), spec.reshape(radix, nq, 2, p, 2 * HY_W)]
    in_specs = [
        pl.BlockSpec((blk, HY_W), lambda j, q, *_: (off + j, u_col), pipeline_mode=pl.Buffered(1)),
        pl.BlockSpec((blk, HY_W), lambda j, q, *_: (off + j, gate_col), pipeline_mode=pl.Buffered(1)),
        pl.BlockSpec((1, HY_W), lambda j, q, *_: (0, 0)),
        mat(), mat(),
        pl.BlockSpec((radix, None, 2, p, HY_W), lambda j, q, *_: (0, q, 0, 0, order)),
    ]
    kwargs = {}
    if prev_out is not None:
        args.append(prev_out)
        in_specs.append(pl.BlockSpec(memory_space=pl.ANY))
        kwargs["input_output_aliases"] = {len(args) - 1: 0}
    return pl.pallas_call(
        functools.partial(_conv_kernel, length=length, p=p, radix=radix),
        grid_spec=pltpu.PrefetchScalarGridSpec(
            num_scalar_prefetch=2,
            grid=(nb // 2, nq),
            in_specs=in_specs,
            out_specs=pl.BlockSpec((blk, HY_W), lambda j, q, *_: (off + j, 0)),
            scratch_shapes=[pltpu.VMEM((blk, HY_W), F32)],
        ),
        out_shape=jax.ShapeDtypeStruct((t_all, HY_W), BF16),
        compiler_params=_cparams("arbitrary", "arbitrary"),
        name="hyena_long_conv",
        **kwargs,
    )(*args)


def _attn_kernel(sink_ref, q_ref, *refs, local):
    if local:
        kp_ref, kc_ref, kn_ref, vp_ref, vc_ref, vn_ref, kx_ref, vx_ref, o_ref = refs
        n = pl.program_id(1)
        last = pl.num_programs(1) - 1
        qi = lax.broadcasted_iota(jnp.int32, (QBLK, QBLK), 0)
        ki = lax.broadcasted_iota(jnp.int32, (QBLK, QBLK), 1)
        ok_prev = (ki >= qi) & (n > 0)
        ok_next = (ki <= qi) & (n < last)
    else:
        kx_ref, vx_ref, o_ref = refs
    scale = HEAD_DIM ** -0.5
    nt = (((1,), (1,)), ((), ()))
    nq = q_ref.shape[0]
    if local:
        bias1 = jnp.concatenate([jnp.where(ok_prev, 0.0, NEG_INF), jnp.zeros((QBLK, QBLK), F32),
                                 jnp.where(ok_next, 0.0, NEG_INF), jnp.zeros((QBLK, kx_ref.shape[0]), F32)], axis=1)
        bias = jnp.concatenate([bias1] * Q_PER_KV, axis=0)
    for kv in range(N_KV_HEADS):
        cs = slice(kv * HEAD_DIM, (kv + 1) * HEAD_DIM)
        heads = range(kv * Q_PER_KV, (kv + 1) * Q_PER_KV)
        qs = jnp.concatenate([q_ref[:, h * HEAD_DIM:(h + 1) * HEAD_DIM] for h in heads], axis=0)
        sink = jnp.concatenate([jnp.full((nq, 1), sink_ref[h], F32) for h in heads], axis=0)
        if local:
            keys = jnp.concatenate([kp_ref[:, cs], kc_ref[:, cs], kn_ref[:, cs], kx_ref[:, cs]], axis=0)
            vals = jnp.concatenate([vp_ref[:, cs], vc_ref[:, cs], vn_ref[:, cs], vx_ref[:, cs]], axis=0)
        else:
            keys, vals = kx_ref[:, cs], vx_ref[:, cs]
        s = lax.dot_general(qs, keys, nt, preferred_element_type=F32) * scale
        if local:
            s = s + bias
        mx = jnp.maximum(jnp.max(s, axis=-1, keepdims=True), sink)
        p = jnp.exp(s - mx)
        den = jnp.sum(p, axis=-1, keepdims=True) + jnp.exp(sink - mx)
        out = _dot(p.astype(BF16), vals) / den
        for g, h in enumerate(heads):
            o_ref[:, h * HEAD_DIM:(h + 1) * HEAD_DIM] = out[g * nq:(g + 1) * nq]


def _attention(q, k, v, sink, nb, n, c):
    t_all = q.shape[0]
    nqb = n // QBLK
    ctx_blk0 = nb * n // c
    kvspec = lambda fn: pl.BlockSpec((QBLK, KV_W), fn)
    prev = lambda b, j: (b * nqb + jnp.maximum(j - 1, 0), 0)
    cur = lambda b, j: (b * nqb + j, 0)
    nxt = lambda b, j: (b * nqb + jnp.minimum(j + 1, nqb - 1), 0)
    ctxs = pl.BlockSpec((c, KV_W), lambda b, j: (ctx_blk0 + b, 0))
    smem = pl.BlockSpec(memory_space=pltpu.SMEM)
    lat = pl.pallas_call(
        functools.partial(_attn_kernel, local=True),
        grid=(nb, nqb),
        in_specs=[smem, pl.BlockSpec((QBLK, ATT_W), cur),
                  kvspec(prev), kvspec(cur), kvspec(nxt), kvspec(prev), kvspec(cur), kvspec(nxt), ctxs, ctxs],
        out_specs=pl.BlockSpec((QBLK, ATT_W), cur),
        out_shape=jax.ShapeDtypeStruct((t_all, ATT_W), F32),
        compiler_params=_cparams("arbitrary", "arbitrary"),
        name="banded_attention",
    )(sink, q, k, k, k, v, v, v, k, v)
    ctx1 = pl.BlockSpec((c, KV_W), lambda b: (ctx_blk0 + b, 0))
    return pl.pallas_call(
        lambda s, qq, kx, vx, prev_o, o: _attn_kernel(s, qq, kx, vx, o, local=False),
        grid=(nb,),
        in_specs=[smem, pl.BlockSpec((c, ATT_W), lambda b: (ctx_blk0 + b, 0)), ctx1, ctx1,
                  pl.BlockSpec(memory_space=pl.ANY)],
        out_specs=pl.BlockSpec((c, ATT_W), lambda b: (ctx_blk0 + b, 0)),
        out_shape=jax.ShapeDtypeStruct((t_all, ATT_W), F32),
        input_output_aliases={4: 0},
        compiler_params=_cparams("arbitrary"),
        name="context_attention",
    )(sink, q, k, v, lat)


def _s5disc_kernel(lre_ref, lim_ref, dt_ref, bre_ref, bim_ref, are_ref, aim_ref, bbre_ref, bbim_ref):
    lam_re = lre_ref[...]
    lam_im = lim_ref[...]
    dt = jnp.exp(dt_ref[...])
    mag = jnp.exp(lam_re * dt)
    ab_re = mag * jnp.cos(lam_im * dt)
    ab_im = mag * jnp.sin(lam_im * dt)
    num_re = ab_re - 1.0
    num_im = ab_im
    den = lam_re * lam_re + lam_im * lam_im
    co_re = (num_re * lam_re + num_im * lam_im) / den
    co_im = (num_im * lam_re - num_re * lam_im) / den
    b_re = bre_ref[...]
    b_im = bim_ref[...]
    are_ref[...] = ab_re
    aim_ref[...] = ab_im
    bbre_ref[...] = co_re * b_re - co_im * b_im
    bbim_ref[...] = co_re * b_im + co_im * b_re


def _s5_discretize(lam_re, lam_im, log_dt, b_re, b_im):
    lead = lam_re.shape[:-1]
    rep = lambda a: jnp.repeat(a.reshape(-1, 1, S5_STATE), S5_CPG, axis=1).reshape(-1, S5_STATE)
    dt = jnp.broadcast_to(log_dt.reshape(-1, 1, 1), (math.prod(lead), S5_CPG, S5_STATE)).reshape(-1, S5_STATE)
    tr = lambda b: jnp.swapaxes(b, -1, -2).reshape(-1, S5_STATE)
    rows = math.prod(lead) * S5_CPG
    shp = jax.ShapeDtypeStruct((rows, S5_STATE), F32)
    ab_re, ab_im, bb_re, bb_im = pl.pallas_call(
        _s5disc_kernel, out_shape=[shp] * 4, name="s5_discretize",
    )(rep(lam_re), rep(lam_im), dt, tr(b_re), tr(b_im))
    full = lead + (S5_CPG, S5_STATE)
    return (ab_re.reshape(full)[..., 0, :], ab_im.reshape(full)[..., 0, :],
            bb_re.reshape(full), bb_im.reshape(full))


def _s5_kernel(uf_ref, ub_ref, bmat_ref, cmat_ref, a_ref, yf_ref, yb_ref, st_ref, bu_ref, *, tc, nbatch):
    i = pl.program_id(0)
    gp = S5_GROUPS * S5_STATE

    @pl.when(i == 0)
    def _():
        st_ref[...] = jnp.zeros(st_ref.shape, F32)

    for d, (u_ref, y_ref) in enumerate(((uf_ref, yf_ref), (ub_ref, yb_ref))):
        bu_ref[...] = _dot(u_ref[...].astype(BF16), bmat_ref[d])
        ar = a_ref[d, 0]
        ai = a_ref[d, 1]

        def body(j, carry, d=d, ar=ar, ai=ai):
            hr, hi = carry
            t = j if d == 0 else tc - 1 - j
            r0 = pl.multiple_of(t * nbatch, nbatch)
            nr = ar * hr - ai * hi + bu_ref[pl.ds(r0, nbatch), 0:gp]
            ni = ar * hi + ai * hr + bu_ref[pl.ds(r0, nbatch), gp:2 * gp]
            bu_ref[pl.ds(r0, nbatch), 0:gp] = nr
            bu_ref[pl.ds(r0, nbatch), gp:2 * gp] = ni
            return nr, ni

        hr, hi = lax.fori_loop(0, tc, body, (st_ref[d, 0], st_ref[d, 1]))
        st_ref[d, 0] = hr
        st_ref[d, 1] = hi
        y_ref[...] = _dot(bu_ref[...].astype(BF16), cmat_ref[d])


def _s5_scan(u_tm, bmat, cmat, a_bc, nb, n, c):
    tc = S5_TC
    rows = tc * nb
    nl, nc = n // tc, c // tc
    gp = S5_GROUPS * S5_STATE
    fwd = lambda i: (jnp.where(i < nc, nl + i, i - nc), 0)
    bwd = lambda i: (nl + nc - 1 - i, 0)
    full = lambda arr: pl.BlockSpec(arr.shape, lambda i: (0,) * arr.ndim)
    shp = jax.ShapeDtypeStruct(u_tm.shape, F32)
    return pl.pallas_call(
        functools.partial(_s5_kernel, tc=tc, nbatch=nb),
        grid=(nl + nc,),
        in_specs=[pl.BlockSpec((rows, S5_W), fwd), pl.BlockSpec((rows, S5_W), bwd),
                  full(bmat), full(cmat), full(a_bc)],
        out_specs=[pl.BlockSpec((rows, S5_W), fwd), pl.BlockSpec((rows, S5_W), bwd)],
        out_shape=[shp, shp],
        scratch_shapes=[pltpu.VMEM((2, 2, nb, gp), F32), pltpu.VMEM((rows, 2 * gp), F32)],
        compiler_params=_cparams("arbitrary"),
        name="s5_scan",
    )(u_tm, u_tm, bmat, cmat, a_bc)


def _s5_matrices(ab_re, ab_im, bb_re, bb_im, c_re, c_im, nb):
    eye = jnp.eye(S5_GROUPS, dtype=F32)
    bd_in = lambda b: jnp.einsum("dgcp,gh->dgchp", b, eye).reshape(2, S5_W, S5_GROUPS * S5_STATE)
    bd_out = lambda cc: jnp.einsum("dgcp,gh->dgphc", cc, eye).reshape(2, S5_GROUPS * S5_STATE, S5_W)
    bmat = jnp.concatenate([bd_in(bb_re), bd_in(bb_im)], axis=-1).astype(BF16)
    cmat = jnp.concatenate([bd_out(c_re), -bd_out(c_im)], axis=1).astype(BF16)
    a = jnp.stack([ab_re.reshape(2, -1), ab_im.reshape(2, -1)], axis=1)
    a_bc = jnp.broadcast_to(a[:, :, None, :], (2, 2, nb, S5_GROUPS * S5_STATE))
    return bmat, cmat, a_bc


def _out_kernel(x_ref, hy_ref, att_ref, y_ref, u_ref, d_ref, gluw_ref, glub_ref, ng_ref, wout_ref,
                g1_ref, sh2_ref, sc2_ref, n2g_ref, rw_ref, rb_ref, tri_ref,
                xo_ref, h2_ref, eidx_ref, gate_ref, rank_ref, cnt_ref):
    y = y_ref[...] + u_ref[...] * d_ref[...]
    g = jax.nn.gelu(y)
    s5 = g * jax.nn.sigmoid(_dot(g.astype(BF16), gluw_ref[...]) + glub_ref[...])

    def nrm(part):
        return part * lax.rsqrt(jnp.mean(part * part, axis=-1, keepdims=True) + EPS)

    mix = jnp.concatenate([nrm(hy_ref[...].astype(F32)), nrm(att_ref[...]), nrm(s5)], axis=-1) * ng_ref[...]
    x = x_ref[...] + g1_ref[...] * _dot(mix.astype(BF16), wout_ref[...])
    xo_ref[...] = x
    h2 = _rms(x, n2g_ref[...]) * (1.0 + sc2_ref[...]) + sh2_ref[...]
    h2_ref[...] = h2.astype(h2_ref.dtype)

    logits = lax.dot_general(rw_ref[...], h2, (((1,), (1,)), ((), ())), precision=HIGHEST,
                             preferred_element_type=F32)
    e = jnp.exp(logits - jnp.max(logits, axis=0, keepdims=True))
    probs = e / jnp.sum(e, axis=0, keepdims=True)
    sel = probs + rb_ref[...]
    rows = [sel[r:r + 1, :] for r in range(N_EXPERTS)]
    best = None
    for grp in range(N_EXPERT_GROUPS):
        a = rows[grp * EXPERTS_PER_GROUP:(grp + 1) * EXPERTS_PER_GROUP]
        score = None
        for p0 in range(EXPERTS_PER_GROUP):
            for p1 in range(p0 + 1, EXPERTS_PER_GROUP):
                pair = a[p0] + a[p1]
                score = pair if score is None else jnp.maximum(score, pair)
        if best is None:
            best, gidx = score, jnp.zeros(score.shape, jnp.int32)
        else:
            better = score > best
            gidx = jnp.where(better, grp, gidx)
            best = jnp.where(better, score, best)
    ing = []
    for j in range(EXPERTS_PER_GROUP):
        v = rows[j]
        for grp in range(1, N_EXPERT_GROUPS):
            v = jnp.where(gidx == grp, rows[grp * EXPERTS_PER_GROUP + j], v)
        ing.append(v)
    first_v, first_i = ing[0], jnp.zeros(gidx.shape, jnp.int32)
    for j in range(1, EXPERTS_PER_GROUP):
        better = ing[j] > first_v
        first_i = jnp.where(better, j, first_i)
        first_v = jnp.where(better, ing[j], first_v)
    second_v, second_i = None, None
    for j in range(EXPERTS_PER_GROUP):
        cand = jnp.where(first_i == j, -jnp.inf, ing[j])
        if second_v is None:
            second_v, second_i = cand, jnp.zeros(gidx.shape, jnp.int32)
        else:
            better = cand > second_v
            second_i = jnp.where(better, j, second_i)
            second_v = jnp.where(better, cand, second_v)
    e0 = gidx * EXPERTS_PER_GROUP + first_i
    e1 = gidx * EXPERTS_PER_GROUP + second_i
    eid = lax.broadcasted_iota(jnp.int32, probs.shape, 0)
    oh0 = eid == e0
    oh1 = eid == e1
    p0v = jnp.sum(jnp.where(oh0, probs, 0.0), axis=0, keepdims=True)
    p1v = jnp.sum(jnp.where(oh1, probs, 0.0), axis=0, keepdims=True)
    tot = p0v + p1v
    member = jnp.where(oh0 | oh1, 1.0, 0.0)
    before = _dot(member.astype(BF16), tri_ref[...])
    r0 = jnp.sum(jnp.where(oh0, before, 0.0), axis=0, keepdims=True)
    r1 = jnp.sum(jnp.where(oh1, before, 0.0), axis=0, keepdims=True)
    cnt_ref[...] = jnp.sum(member, axis=1, keepdims=True)
    eidx_ref[...] = jnp.concatenate([e0, e1], axis=0)
    gate_ref[...] = jnp.concatenate([p0v / tot, p1v / tot], axis=0)
    rank_ref[...] = jnp.concatenate([r0, r1], axis=0).astype(jnp.int32)


def _mixer_out(x_all, hy, att, y_tok, u, s5_d, glu_w_bf, glu_b, mix_g, w_out_bf, mod4, n2g,
               router_wt, router_b, nb, n):
    t_all = x_all.shape[0]
    n_lat_tiles = nb * n // TM
    tpb = n // TM
    mrow = lambda i: jnp.where(i < n_lat_tiles, i // tpb, nb)
    row = lambda w: pl.BlockSpec((TM, w), lambda i: (i, 0))
    full = lambda arr: pl.BlockSpec(arr.shape, lambda i: (0,) * arr.ndim)
    modspec = lambda k: pl.BlockSpec((None, None, 1, D_MODEL), lambda i, k=k: (mrow(i), k, 0, 0))
    tri = (lax.broadcasted_iota(jnp.int32, (TM, TM), 0) < lax.broadcasted_iota(jnp.int32, (TM, TM), 1)).astype(BF16)
    v1 = lambda a: a.reshape(1, -1)
    ins = [v1(s5_d), glu_w_bf, v1(glu_b), v1(mix_g), w_out_bf]
    tail = [v1(n2g), router_wt, router_b.reshape(N_EXPERTS, 1), tri]
    tok = lambda: pl.BlockSpec((2, TM), lambda i: (0, i))
    return pl.pallas_call(
        _out_kernel,
        grid=(t_all // TM,),
        in_specs=[row(D_MODEL), row(HY_W), row(ATT_W), row(S5_W), row(S5_W)] + [full(a) for a in ins]
                 + [modspec(2), modspec(3), modspec(4)] + [full(a) for a in tail],
        out_specs=[row(D_MODEL), row(D_MODEL), tok(), tok(), tok(),
                   pl.BlockSpec((None, N_EXPERTS, 1), lambda i: (i, 0, 0))],
        out_shape=[
            jax.ShapeDtypeStruct((t_all, D_MODEL), F32),
            jax.ShapeDtypeStruct((t_all, D_MODEL), BF16),
            jax.ShapeDtypeStruct((2, t_all), jnp.int32),
            jax.ShapeDtypeStruct((2, t_all), F32),
            jax.ShapeDtypeStruct((2, t_all), jnp.int32),
            jax.ShapeDtypeStruct((t_all // TM, N_EXPERTS, 1), F32),
        ],
        compiler_params=_cparams("arbitrary"),
        name="mixer_out_router",
    )(x_all, hy, att, y_tok, u, *ins, mod4, mod4, mod4, *tail)


SORT_ROWS = 2 * TM + 8 * N_EXPERTS
CHUNK_BITS = 6
TOT_BITS = 7
TAIL_BITS = 5


def _for_chunks(n8, bits, fn):
    for b in range(bits - 1, -1, -1):
        done = ((n8 >> (b + 1)) << (b + 1)) * 8

        @pl.when(((n8 >> b) & 1) == 1)
        def _(done=done, rows=8 << b):
            fn(pl.multiple_of(done, 8), rows)


def _dispatch_kernel(loc_ref, dst_ref, n8_ref, tot_ref, tail_ref, h2_ref, eidx_ref, lrank_ref, adj_ref,
                     xb_ref, spos_ref, sorted_ref, zero_ref, sem, zsem):
    i = pl.program_id(0)
    n_tiles = pl.num_programs(0)
    slot = i % 2

    def wait_tile(tile, buf):
        def wait(off, rows):
            pltpu.make_async_copy(sorted_ref.at[buf, pl.ds(0, rows)], xb_ref.at[pl.ds(0, rows)], sem.at[buf]).wait()
        _for_chunks(tot_ref[tile], TOT_BITS, wait)

    @pl.when(i >= 2)
    def _():
        wait_tile(i - 2, slot)

    @pl.when(i == 0)
    def _():
        zero_ref[...] = jnp.zeros(zero_ref.shape, F32)
        for start_wait in (True, False):
            for e in range(N_EXPERTS):
                def tail(off, rows, e=e, start_wait=start_wait):
                    cp = pltpu.make_async_copy(zero_ref.at[pl.ds(0, rows)],
                                               xb_ref.at[pl.ds(pl.multiple_of(tail_ref[e] + off, 8), rows)], zsem)
                    cp.start() if start_wait else cp.wait()
                _for_chunks(tail_ref[N_EXPERTS + e], TAIL_BITS, tail)

    eid = lax.broadcasted_iota(jnp.int32, (N_EXPERTS, TM), 0)
    adj = adj_ref[...]
    spos = []
    for k in range(2):
        chunk_start = jnp.sum(jnp.where(eid == eidx_ref[k:k + 1, :], adj, 0.0), axis=0, keepdims=True)
        spos.append(lrank_ref[k:k + 1, :] + chunk_start.astype(jnp.int32))
    spos_ref[...] = jnp.concatenate(spos, axis=0)
    rows = lax.broadcasted_iota(jnp.int32, (SORT_ROWS, TM), 0)
    perm = jnp.where((rows == spos[0]) | (rows == spos[1]), 1.0, 0.0).astype(BF16)
    sorted_ref[slot] = _dot(perm, h2_ref[...])

    for e in range(N_EXPERTS):
        idx = i * N_EXPERTS + e

        def send(off, rows, idx=idx):
            pltpu.make_async_copy(sorted_ref.at[slot, pl.ds(pl.multiple_of(loc_ref[idx] + off, 8), rows)],
                                  xb_ref.at[pl.ds(pl.multiple_of(dst_ref[idx] + off, 8), rows)], sem.at[slot]).start()
        _for_chunks(n8_ref[idx], CHUNK_BITS, send)

    @pl.when(i == n_tiles - 1)
    def _():
        wait_tile(i - 1, 1 - slot)
        wait_tile(i, slot)


def _dispatch(tables, h2, eidx, lrank, adj, n_slots):
    t_all = h2.shape[0]
    tok = lambda: pl.BlockSpec((2, TM), lambda i, *_: (0, i))
    return pl.pallas_call(
        _dispatch_kernel,
        grid_spec=pltpu.PrefetchScalarGridSpec(
            num_scalar_prefetch=5,
            grid=(t_all // TM,),
            in_specs=[pl.BlockSpec((TM, D_MODEL), lambda i, *_: (i, 0)), tok(), tok(),
                      pl.BlockSpec((None, N_EXPERTS, 1), lambda i, *_: (i, 0, 0))],
            out_specs=[pl.BlockSpec(memory_space=pl.ANY), tok()],
            scratch_shapes=[pltpu.VMEM((2, SORT_ROWS, D_MODEL), F32), pltpu.VMEM((MOE_ROWS // 2, D_MODEL), F32),
                            pltpu.SemaphoreType.DMA((2,)), pltpu.SemaphoreType.DMA(())],
        ),
        out_shape=[jax.ShapeDtypeStruct((n_slots, D_MODEL), F32), jax.ShapeDtypeStruct((2, t_all), jnp.int32)],
        compiler_params=_cparams("arbitrary"),
        name="moe_dispatch",
    )(*tables, h2, eidx, lrank, adj)


def _ffn_kernel(be_ref, nu_ref, x_ref, wg_ref, wu_ref, wd_ref, o_ref):
    del be_ref

    @pl.when(pl.program_id(0) < nu_ref[0])
    def _():
        x = x_ref[...].astype(BF16)
        a = _dot(x, wg_ref[...])
        u = _dot(x, wu_ref[...])
        hidden = (a * jax.nn.sigmoid(a)) * u
        o_ref[...] = _dot(hidden.astype(BF16), wd_ref[...])


def _expert_ffn(block_e, n_used, xb, wg, wu, wd):
    n_slots = xb.shape[0]
    blk = lambda j, be, nu: jnp.minimum(j, nu[0] - 1)
    wspec = lambda: pl.BlockSpec((None, D_MODEL, D_MODEL), lambda j, be, nu: (be[blk(j, be, nu)], 0, 0))
    return pl.pallas_call(
        _ffn_kernel,
        grid_spec=pltpu.PrefetchScalarGridSpec(
            num_scalar_prefetch=2,
            grid=(n_slots // MOE_ROWS,),
            in_specs=[pl.BlockSpec((MOE_ROWS, D_MODEL), lambda j, be, nu: (blk(j, be, nu), 0)),
                      wspec(), wspec(), wspec()],
            out_specs=pl.BlockSpec((MOE_ROWS, D_MODEL), lambda j, be, nu: (blk(j, be, nu), 0)),
        ),
        out_shape=jax.ShapeDtypeStruct((n_slots, D_MODEL), F32),
        compiler_params=_cparams("arbitrary"),
        name="moe_expert_ffn",
    )(block_e, n_used, xb, wg, wu, wd)


def _combine_kernel(loc_ref, dst_ref, n8_ref, tot_ref, x_ref, spos_ref, gates_ref, g2_ref, yb_ref, *rest,
                    final):
    if final:
        fg_ref, xo_ref, ybuf_ref, sem = rest
    else:
        xo_ref, ybuf_ref, sem = rest
    i = pl.program_id(0)
    n_tiles = pl.num_programs(0)
    slot = i % 2

    def fetch_tile(tile, buf):
        for e in range(N_EXPERTS):
            idx = tile * N_EXPERTS + e

            def fetch(off, rows, idx=idx):
                pltpu.make_async_copy(yb_ref.at[pl.ds(pl.multiple_of(dst_ref[idx] + off, 8), rows)],
                                      ybuf_ref.at[buf, pl.ds(pl.multiple_of(loc_ref[idx] + off, 8), rows)],
                                      sem.at[buf]).start()
            _for_chunks(n8_ref[idx], CHUNK_BITS, fetch)

    @pl.when(i == 0)
    def _():
        ybuf_ref[...] = jnp.zeros(ybuf_ref.shape, F32)
        fetch_tile(0, 0)

    @pl.when(i + 1 < n_tiles)
    def _():
        fetch_tile(i + 1, 1 - slot)

    def wait(off, rows):
        pltpu.make_async_copy(yb_ref.at[pl.ds(0, rows)], ybuf_ref.at[slot, pl.ds(0, rows)], sem.at[slot]).wait()
    _for_chunks(tot_ref[i], TOT_BITS, wait)

    y_sorted = ybuf_ref[slot].astype(BF16)
    lane = lax.broadcasted_iota(jnp.int32, (TM, SORT_ROWS), 1)
    gates = gates_ref[...]
    y = jnp.zeros((TM, D_MODEL), F32)
    for k in range(2):
        pick = jnp.where(lane == spos_ref[:, k:k + 1], 1.0, 0.0).astype(BF16)
        y = y + gates[:, k:k + 1] * _dot(pick, y_sorted)
    x = x_ref[...] + g2_ref[...] * y
    xo_ref[...] = _rms(x, fg_ref[...]) if final else x


def _combine(tables, x_all, spos_t, gates_t, mod4, yb, nb, n, final_g=None):
    final = final_g is not None
    t_all = nb * n if final else x_all.shape[0]
    n_lat_tiles = nb * n // TM
    tpb = n // TM
    mrow = lambda i: jnp.where(i < n_lat_tiles, i // tpb, nb)
    extra_specs = [pl.BlockSpec((1, D_MODEL), lambda i, *_: (0, 0))] if final else []
    extra_args = [final_g.reshape(1, D_MODEL)] if final else []
    return pl.pallas_call(
        functools.partial(_combine_kernel, final=final),
        grid_spec=pltpu.PrefetchScalarGridSpec(
            num_scalar_prefetch=4,
            grid=(t_all // TM,),
            in_specs=[pl.BlockSpec((TM, D_MODEL), lambda i, *_: (i, 0)),
                      pl.BlockSpec((TM, 2), lambda i, *_: (i, 0)),
                      pl.BlockSpec((TM, 2), lambda i, *_: (i, 0)),
                      pl.BlockSpec((None, None, 1, D_MODEL), lambda i, *_: (mrow(i), 5, 0, 0)),
                      pl.BlockSpec(memory_space=pl.ANY)] + extra_specs,
            out_specs=pl.BlockSpec((TM, D_MODEL), lambda i, *_: (i, 0)),
            scratch_shapes=[pltpu.VMEM((2, SORT_ROWS, D_MODEL), F32), pltpu.SemaphoreType.DMA((2,))],
        ),
        out_shape=jax.ShapeDtypeStruct((t_all, D_MODEL), F32),
        compiler_params=_cparams("arbitrary"),
        name="moe_combine",
    )(*tables, x_all, spos_t, gates_t, mod4, yb, *extra_args)


def _time_major(a, nb, n, c):
    w = a.shape[-1]
    lat = jnp.swapaxes(a[:nb * n].reshape(nb, n, w), 0, 1).reshape(n * nb, w)
    ctx = jnp.swapaxes(a[nb * n:].reshape(nb, c, w), 0, 1).reshape(c * nb, w)
    return jnp.concatenate([lat, ctx], axis=0)


def _token_major(a, nb, n, c):
    w = a.shape[-1]
    lat = jnp.swapaxes(a[:n * nb].reshape(n, nb, w), 0, 1).reshape(nb * n, w)
    ctx = jnp.swapaxes(a[n * nb:].reshape(c, nb, w), 0, 1).reshape(nb * c, w)
    return jnp.concatenate([lat, ctx], axis=0)


def kernel(x, c, ctx, c_ctx, norm1_g, norm2_g, ada_w, ada_b, w_in, w_out, mix_norm_g, hy_conv_w, hy_conv_b, hy_f_w1, hy_f_b1, hy_f_freq1, hy_f_w2, hy_f_b2, hy_f_freq2, hy_f_w3, hy_bias, attn_sink, s5_lam_re, s5_lam_im, s5_log_dt, s5_b_re, s5_b_im, s5_c_re, s5_c_im, s5_d, s5_glu_w, s5_glu_b, router_w, router_b, moe_w_gate, moe_w_up, moe_w_down, final_g):
    nb, n, d = x.shape
    cl = ctx.shape[1]
    depth = w_in.shape[0]
    t_all = nb * (n + cl)

    cc = jnp.zeros((16, d), F32).at[:nb].set(c).at[nb].set(c_ctx)
    mod_all = _modulation(cc, ada_w, ada_b).reshape(depth, 16, 6, 1, d)
    cos, sin = _rope_tables(n)
    ab_re, ab_im, bb_re, bb_im = _s5_discretize(s5_lam_re, s5_lam_im, s5_log_dt, s5_b_re, s5_b_im)

    tabs = {}
    for length in (n, cl):
        twr, twi, er, ei = _dft_tables(length)
        tabs[length] = (twr, twi, er.astype(BF16), ei.astype(BF16))

    router_wt = router_w.T
    expert_ids = jnp.arange(N_EXPERTS, dtype=jnp.int32)
    n_tiles = t_all // TM
    tile_ids = jnp.arange(n_tiles, dtype=jnp.int32)
    n_slots = (-(-(2 * t_all + 7 * N_EXPERTS * n_tiles) // MOE_ROWS) + N_EXPERTS) * MOE_ROWS
    x_all = jnp.concatenate([x.reshape(nb * n, d), ctx.reshape(nb * cl, d)], axis=0)

    for l in range(depth):
        mod4 = mod_all[l]
        hy, q, k, v, u = _input_proj(x_all, norm1_g[l], mod4, w_in[l].astype(BF16), cos, sin, nb, n)

        zc = _short_conv(hy, hy_conv_w[l], hy_conv_b[l], nb, n, cl)
        hy_out = None
        z1 = None
        specs = {}
        for length in (n, cl):
            specs[length] = _filter_spectrum(length, tabs[length], hy_f_w1[l], hy_f_b1[l], hy_f_freq1[l],
                                             hy_f_w2[l], hy_f_b2[l], hy_f_freq2[l], hy_f_w3[l])
        for length, off in ((n, 0), (cl, nb * n)):
            z1 = _long_conv(zc, 0, zc, 1, hy_bias[l, 0], specs[length], 0, tabs[length], length, off, nb, z1)
        for length, off in ((n, 0), (cl, nb * n)):
            hy_out = _long_conv(z1, 0, zc, 2, hy_bias[l, 1], specs[length], 1, tabs[length], length, off, nb,
                                hy_out)

        att = _attention(q, k, v, attn_sink[l], nb, n, cl)

        bmat, cmat, a_bc = _s5_matrices(ab_re[l], ab_im[l], bb_re[l], bb_im[l], s5_c_re[l], s5_c_im[l], nb)
        y_f, y_b = _s5_scan(_time_major(u, nb, n, cl), bmat, cmat, a_bc, nb, n, cl)
        y_tok = _token_major(y_f + y_b, nb, n, cl)

        x_all, h2, eidx, gates, lrank, counts = _mixer_out(
            x_all, hy_out, att, y_tok, u, s5_d[l], s5_glu_w[l].astype(BF16), s5_glu_b[l], mix_norm_g[l],
            w_out[l].astype(BF16), mod4, norm2_g[l], router_wt, router_b, nb, n)

        cnt = counts.reshape(n_tiles, N_EXPERTS).astype(jnp.int32)
        c8 = (cnt + 7) // 8 * 8
        loc = jnp.sum(jnp.where(expert_ids[None, :] < expert_ids[:, None], c8[:, None, :], 0), axis=2)
        seg_off = jnp.sum(jnp.where((tile_ids[None, :] < tile_ids[:, None])[:, :, None], c8[None, :, :], 0), axis=1)
        tot8 = jnp.sum(c8, axis=0)
        padded = (tot8 + MOE_ROWS - 1) // MOE_ROWS * MOE_ROWS
        ends = jnp.sum(jnp.where(expert_ids[None, :] <= expert_ids[:, None], padded[None, :], 0), axis=1)
        pstart = ends - padded
        dst = pstart[None, :] + seg_off
        tables = (loc.reshape(-1), dst.reshape(-1), (c8 // 8).reshape(-1), jnp.sum(c8, axis=1) // 8)
        tail = jnp.concatenate([pstart + tot8, (padded - tot8) // 8])
        n_blocks = n_slots // MOE_ROWS
        block_start = jnp.arange(n_blocks, dtype=jnp.int32) * MOE_ROWS
        block_e = jnp.minimum(jnp.sum((ends[None, :] <= block_start[:, None]).astype(jnp.int32), axis=1),
                              N_EXPERTS - 1)
        n_used = (ends[-1] // MOE_ROWS).reshape(1)

        xb, spos = _dispatch(tables + (tail,), h2, eidx, lrank, loc.astype(F32).reshape(n_tiles, N_EXPERTS, 1),
                             n_slots)
        yb = _expert_ffn(block_e, n_used, xb, moe_w_gate[l].astype(BF16), moe_w_up[l].astype(BF16),
                         moe_w_down[l].astype(BF16))
        x_all = _combine(tables, x_all, spos.T, gates.T, mod4, yb, nb, n, final_g if l == depth - 1 else None)

    return x_all.reshape(nb, n, d)
```

```python
import functools
import math

import jax
import jax.numpy as jnp
from jax import lax
from jax.experimental import pallas as pl
from jax.experimental.pallas import tpu as pltpu

F32 = jnp.float32
BF16 = jnp.bfloat16
HIGHEST = lax.Precision.HIGHEST

D_MODEL = 1024
GRID_W = 64
HY_W = 256
ATT_W = 512
S5_W = 256
HEAD_DIM = 64
N_HEADS = 8
N_KV_HEADS = 2
Q_PER_KV = 4
KV_W = 128
WINDOW = 128
QBLK = 128
ROPE_BASE = 10000.0
NEG_INF = -1e30
HY_BANDS = 16
HY_DECAY_MIN = math.log(1e-2) / 1.5
HY_DECAY_MAX = math.log(1e-2) / 0.3
S5_CPG = 16
S5_GROUPS = 16
S5_STATE = 64
N_EXPERTS = 16
N_EXPERT_GROUPS = 4
EXPERTS_PER_GROUP = 4
HY_END = 768
Q_END = 1280
K_END = 1408
V_END = 1536
IN_W = 1792
EPS = 1e-6

TM = 256
MOE_ROWS = 512
S5_TC = 64
DFT_P = 512
VMEM_LIMIT = 56 * 1024 * 1024


def _cparams(*sem):
    return pltpu.CompilerParams(dimension_semantics=sem, vmem_limit_bytes=VMEM_LIMIT)


def _dot(a, b):
    return jnp.dot(a, b, preferred_element_type=F32)


def _mod_kernel(c_ref, w_ref, b_ref, o_ref):
    c = c_ref[...]
    a = c * jax.nn.sigmoid(c)
    o_ref[...] = _dot(a.astype(BF16), w_ref[...].astype(BF16)) + b_ref[...]


def _modulation(cc, ada_w, ada_b):
    depth, d, w6 = ada_w.shape
    nb = 1536
    return pl.pallas_call(
        _mod_kernel,
        grid=(depth, w6 // nb),
        in_specs=[
            pl.BlockSpec((16, d), lambda l, j: (0, 0)),
            pl.BlockSpec((None, d, nb), lambda l, j: (l, 0, j)),
            pl.BlockSpec((None, 1, nb), lambda l, j: (l, 0, j)),
        ],
        out_specs=pl.BlockSpec((None, 16, nb), lambda l, j: (l, 0, j)),
        out_shape=jax.ShapeDtypeStruct((depth, 16, w6), F32),
        compiler_params=_cparams("arbitrary", "arbitrary"),
        name="adaln_mod",
    )(cc, ada_w, ada_b.reshape(depth, 1, w6))


def _rms(x, g):
    return x * lax.rsqrt(jnp.mean(x * x, axis=-1, keepdims=True) + EPS) * g


def _in_kernel(x_ref, g_ref, sh_ref, sc_ref, w_ref, cos_ref, sin_ref,
               hy_ref, q_ref, k_ref, v_ref, u_ref):
    h = _rms(x_ref[...], g_ref[...]) * (1.0 + sc_ref[...]) + sh_ref[...]
    p = _dot(h.astype(BF16), w_ref[...])
    hy_ref[...] = p[:, :HY_END]
    cos = cos_ref[...]
    sin = sin_ref[...]
    lane = lax.broadcasted_iota(jnp.int32, cos.shape, 1)
    first = (lane % 32) < 16

    def rope(z):
        swapped = jnp.where(first, pltpu.roll(z, 112, 1), pltpu.roll(z, 16, 1))
        return z * cos + swapped * sin

    for j in range(ATT_W // 128):
        qj = rope(p[:, HY_END + j * 128:HY_END + (j + 1) * 128]) * (HEAD_DIM ** -0.5)
        q_ref[:, j * 128:(j + 1) * 128] = qj.astype(BF16)
    k_ref[...] = rope(p[:, Q_END:K_END]).astype(BF16)
    v_ref[...] = p[:, K_END:V_END].astype(BF16)
    u_ref[...] = p[:, V_END:]


def _rope_tables(n):
    quarter = HEAD_DIM // 4
    inv_freq = ROPE_BASE ** (-jnp.arange(quarter, dtype=F32) / quarter)
    t = jnp.arange(n)
    rows = (t // GRID_W).astype(F32)
    cols = (t % GRID_W).astype(F32)
    ang_r = rows[:, None] * inv_freq[None, :]
    ang_c = cols[:, None] * inv_freq[None, :]
    cos64 = jnp.concatenate([jnp.cos(ang_r)] * 2 + [jnp.cos(ang_c)] * 2, axis=-1)
    sin64 = jnp.concatenate([-jnp.sin(ang_r), jnp.sin(ang_r), -jnp.sin(ang_c), jnp.sin(ang_c)], axis=-1)
    cos = jnp.concatenate([jnp.tile(cos64, (1, 2)), jnp.ones((TM, 128), F32)], axis=0)
    sin = jnp.concatenate([jnp.tile(sin64, (1, 2)), jnp.zeros((TM, 128), F32)], axis=0)
    return cos, sin


def _input_proj(x_all, g, mod4, w_in_bf, cos, sin, nb, n, layer):
    t_all = x_all.shape[0]
    n_lat_tiles = nb * n // TM
    tpb = n // TM

    def mrow(i):
        return jnp.where(i < n_lat_tiles, i // tpb, nb)

    def trow(i):
        return jnp.where(i < n_lat_tiles, i % tpb, tpb)

    row = lambda w: pl.BlockSpec((TM, w), lambda i: (i, 0))
    return pl.pallas_call(
        _in_kernel,
        grid=(t_all // TM,),
        in_specs=[
            row(D_MODEL),
            pl.BlockSpec((1, D_MODEL), lambda i: (0, 0)),
            pl.BlockSpec((None, None, 1, D_MODEL), lambda i: (mrow(i), 0, 0, 0)),
            pl.BlockSpec((None, None, 1, D_MODEL), lambda i: (mrow(i), 1, 0, 0)),
            pl.BlockSpec((None, D_MODEL, IN_W), lambda i: (layer, 0, 0)),
            pl.BlockSpec((TM, 128), lambda i: (trow(i), 0)),
            pl.BlockSpec((TM, 128), lambda i: (trow(i), 0)),
        ],
        out_specs=[row(HY_END), row(ATT_W), row(KV_W), row(KV_W), row(S5_W)],
        out_shape=[
            jax.ShapeDtypeStruct((t_all, HY_END), F32),
            jax.ShapeDtypeStruct((t_all, ATT_W), BF16),
            jax.ShapeDtypeStruct((t_all, KV_W), BF16),
            jax.ShapeDtypeStruct((t_all, KV_W), BF16),
            jax.ShapeDtypeStruct((t_all, S5_W), F32),
        ],
        compiler_params=_cparams("arbitrary"),
        name="norm_mod_inproj",
    )(x_all, g.reshape(1, D_MODEL), mod4, mod4, w_in_bf, cos, sin)


def _sconv_kernel(z_ref, w_ref, b_ref, o_ref):
    z = z_ref[...]
    length = z.shape[0]
    row = lax.broadcasted_iota(jnp.int32, z.shape, 0)
    zm = jnp.where(row == 0, 0.0, pltpu.roll(z, 1, 0))
    zp = jnp.where(row == length - 1, 0.0, pltpu.roll(z, length - 1, 0))
    o_ref[...] = (b_ref[...] + zm * w_ref[0:1, :] + z * w_ref[1:2, :] + zp * w_ref[2:3, :]).astype(o_ref.dtype)


def _short_conv(hy, conv_w, conv_b, nb, n, c):
    t_all = hy.shape[0]
    out = None
    for length, off in ((n, 0), (c, nb * n // c)):
        kwargs = {}
        args = [hy, conv_w, conv_b.reshape(1, HY_END)]
        in_specs = [
            pl.BlockSpec((length, 256), lambda b, j, off=off: (off + b, j)),
            pl.BlockSpec((3, 256), lambda b, j: (0, j)),
            pl.BlockSpec((1, 256), lambda b, j: (0, j)),
        ]
        kern = _sconv_kernel
        if out is not None:
            args.append(out)
            in_specs.append(pl.BlockSpec(memory_space=pl.ANY))
            kwargs["input_output_aliases"] = {3: 0}
            kern = lambda z, w, b, prev, o: _sconv_kernel(z, w, b, o)
        out = pl.pallas_call(
            kern,
            grid=(nb, 3),
            in_specs=in_specs,
            out_specs=pl.BlockSpec((length, 256), lambda b, j, off=off: (off + b, j)),
            out_shape=jax.ShapeDtypeStruct((t_all, HY_END), BF16),
            compiler_params=_cparams("arbitrary", "arbitrary"),
            name="hyena_short_conv",
            **kwargs,
        )(*args)
    return out


def _dft_plan(length):
    p = min(DFT_P, length)
    return p, 2 * length // p


def _dft_tables(length):
    p, na = _dft_plan(length)
    m = 2 * length
    a = jnp.arange(na, dtype=jnp.int32)
    ang_w = (2.0 * math.pi / na) * ((a[:, None] * a[None, :]) % na).astype(F32)
    twr, twi = jnp.cos(ang_w), -jnp.sin(ang_w)
    r = jnp.arange(p, dtype=jnp.int32)
    k = a[:, None, None] + na * r[None, :, None]
    ang = (2.0 * math.pi / m) * ((k * r[None, None, :]) % m).astype(F32)
    er, ei = jnp.cos(ang), -jnp.sin(ang)
    return twr, twi, er, ei


def _filter_kernel(twr_ref, twi_ref, w1_ref, b1_ref, f1_ref, w2_ref, b2_ref, f2_ref, w3_ref,
                   bands_ref, decay_ref, er_ref, ei_ref, h_ref, kern_ref, norm_ref, *, length, p, na):
    ka = pl.program_id(0)
    m = 2 * length
    hdot = functools.partial(jnp.dot, precision=HIGHEST, preferred_element_type=F32)

    @pl.when(ka == 0)
    def _():
        norm = jnp.zeros((1, 2 * HY_W), F32)
        for a in range(na):
            n_idx = a * p + lax.broadcasted_iota(jnp.int32, (p, 1), 0)
            is_fwd = n_idx < length
            pos = jnp.where(is_fwd, n_idx, m - n_idx).astype(F32)
            t = pos / float(max(length - 1, 1))
            ang = (2.0 * math.pi / length) * pos * bands_ref[...]
            pre = (t * w1_ref[0:1, :] + hdot(jnp.cos(ang), w1_ref[1:1 + HY_BANDS, :])
                   + hdot(-jnp.sin(ang), w1_ref[1 + HY_BANDS:, :]) + b1_ref[...])
            hid = jnp.sin(f1_ref[...] * pre)
            hid = jnp.sin(f2_ref[...] * (hdot(hid, w2_ref[...]) + b2_ref[...]))
            taps = hdot(hid, w3_ref[...])
            wnd = jnp.exp(-t * decay_ref[...])
            live = n_idx != length
            for o in range(2):
                fwd = taps[:, o * 2 * HY_W:o * 2 * HY_W + HY_W]
                bwd = taps[:, o * 2 * HY_W + HY_W:(o + 1) * 2 * HY_W]
                kern = jnp.where(live, jnp.where(is_fwd, fwd, bwd) * wnd, 0.0)
                kern_ref[a, :, o * HY_W:(o + 1) * HY_W] = kern
            norm = norm + jnp.sum(jnp.abs(kern_ref[a]), axis=0, keepdims=True)
        norm_ref[...] = norm

    gr = jnp.zeros((p, 2 * HY_W), F32)
    gi = jnp.zeros((p, 2 * HY_W), F32)
    for a in range(na):
        slab = kern_ref[a]
        gr = gr + twr_ref[ka, a] * slab
        gi = gi + twi_ref[ka, a] * slab
    er = er_ref[...]
    ei = ei_ref[...]
    grb = gr.astype(BF16)
    gib = gi.astype(BF16)
    inv = 1.0 / norm_ref[...]
    h_ref[0] = ((_dot(er, grb) - _dot(ei, gib)) * inv).astype(h_ref.dtype)
    h_ref[1] = ((_dot(ei, grb) + _dot(er, gib)) * inv).astype(h_ref.dtype)


def _filter_spectrum(length, tables, w1, b1, f1, w2, b2, f2, w3):
    p, na = _dft_plan(length)
    twr, twi, er, ei = tables
    bands = jnp.linspace(1e-4, HY_BANDS - 1, HY_BANDS, dtype=F32).reshape(1, HY_BANDS)
    decay = jnp.abs(jnp.linspace(HY_DECAY_MIN, HY_DECAY_MAX, HY_W, dtype=F32)).reshape(1, HY_W)
    full = lambda arr: pl.BlockSpec(arr.shape, lambda ka, *_: (0,) * arr.ndim)
    vec = lambda v: v.reshape(1, -1)
    ins = [w1, vec(b1), vec(f1), w2, vec(b2), vec(f2), w3, bands, decay]
    return pl.pallas_call(
        functools.partial(_filter_kernel, length=length, p=p, na=na),
        grid_spec=pltpu.PrefetchScalarGridSpec(
            num_scalar_prefetch=2,
            grid=(na,),
            in_specs=[full(x) for x in ins] + [
                pl.BlockSpec((None, p, p), lambda ka, *_: (ka, 0, 0)),
                pl.BlockSpec((None, p, p), lambda ka, *_: (ka, 0, 0)),
            ],
            out_specs=pl.BlockSpec((None, 2, p, 2 * HY_W), lambda ka, *_: (ka, 0, 0, 0)),
            scratch_shapes=[pltpu.VMEM((na, p, 2 * HY_W), F32), pltpu.VMEM((1, 2 * HY_W), F32)],
        ),
        out_shape=jax.ShapeDtypeStruct((na, 2, p, 2 * HY_W), BF16),
        compiler_params=_cparams("arbitrary"),
        name="hyena_filter_spectrum",
    )(twr, twi, *[vec(x) if x.ndim == 1 else x for x in ins], er, ei)


def _cadd(x, y):
    return x[0] + y[0], x[1] + y[1]


def _csub(x, y):
    return x[0] - y[0], x[1] - y[1]


def _radix_mix(parts, sign):
    if len(parts) == 1:
        return parts
    p0, p1, p2, p3 = parts
    t0, t1, t2, t3 = _cadd(p0, p2), _csub(p0, p2), _cadd(p1, p3), _csub(p1, p3)
    it3 = (-t3[1], t3[0])
    if sign < 0:
        return [_cadd(t0, t2), _csub(t1, it3), _csub(t0, t2), _cadd(t1, it3)]
    return [_cadd(t0, t2), _cadd(t1, it3), _csub(t0, t2), _csub(t1, it3)]


def _conv_kernel(twr_ref, twi_ref, u_ref, gate_ref, bias_ref, er_ref, ei_ref, h_ref, *rest, length, p, radix):
    o_ref, acc_ref = rest[-2], rest[-1]
    q = pl.program_id(1)
    nz = length // p
    tdot = lambda x, y: lax.dot_general(x, y, (((0,), (0,)), ((), ())), preferred_element_type=F32)

    parts = [None] * radix
    for a in range(nz):
        wr = twr_ref[q, a]
        wi = twi_ref[q, a]
        zr = u_ref[a * p:(a + 1) * p, :].astype(F32)
        zi = u_ref[length + a * p:length + (a + 1) * p, :].astype(F32)
        term = (wr * zr - wi * zi, wr * zi + wi * zr)
        parts[a % radix] = term if parts[a % radix] is None else _cadd(parts[a % radix], term)
    zero = jnp.zeros((p, o_ref.shape[-1]), F32)
    parts = [(zero, zero) if t is None else t for t in parts]
    g = _radix_mix(parts, -1)

    v = []
    for m in range(radix):
        grb = g[m][0].astype(BF16)
        gib = g[m][1].astype(BF16)
        er = er_ref[m]
        ei = ei_ref[m]
        sr = _dot(er, grb) - _dot(ei, gib)
        si = _dot(ei, grb) + _dot(er, gib)
        hr = h_ref[m, 0].astype(F32)
        hi = h_ref[m, 1].astype(F32)
        yr = (sr * hr - si * hi).astype(BF16)
        yi = (sr * hi + si * hr).astype(BF16)
        v.append((tdot(er, yr) + tdot(ei, yi), tdot(er, yi) - tdot(ei, yr)))
    qs = _radix_mix(v, +1)

    @pl.when(q == 0)
    def _():
        acc_ref[...] = jnp.zeros(acc_ref.shape, F32)

    scale = 1.0 / (2 * length)
    for a in range(nz):
        wr = twr_ref[q, a] * scale
        wi = twi_ref[q, a] * scale
        vr, vi = qs[a % radix]
        acc_ref[a * p:(a + 1) * p, :] += wr * vr + wi * vi
        acc_ref[length + a * p:length + (a + 1) * p, :] += wr * vi - wi * vr

    @pl.when(q == pl.num_programs(1) - 1)
    def _():
        u = u_ref[...].astype(F32)
        o_ref[...] = (gate_ref[...].astype(F32) * (acc_ref[...] + u * bias_ref[...])).astype(o_ref.dtype)


def _long_conv(u_arr, u_col, gate_arr, gate_col, bias, spec, order, tables_bf, length, row_off, nb, prev_out):
    p, na = _dft_plan(length)
    radix = 4 if na % 4 == 0 and na >= 8 else 1
    nq = na // radix
    twr, twi, er, ei = tables_bf
    t_all = u_arr.shape[0]
    blk = 2 * length
    off = row_off // blk
    mat = lambda: pl.BlockSpec((radix, None, p, p), lambda j, q, *_: (0, q, 0, 0))
    args = [twr, twi, u_arr, gate_arr, bias.reshape(1, HY_W), er.reshape(radix, nq, p, p),
            ei.reshape(radix, nq, p, p), spec.reshape(radix, nq, 2, p, 2 * HY_W)]
    in_specs = [
        pl.BlockSpec((blk, HY_W), lambda j, q, *_: (off + j, u_col), pipeline_mode=pl.Buffered(1)),
        pl.BlockSpec((blk, HY_W), lambda j, q, *_: (off + j, gate_col), pipeline_mode=pl.Buffered(1)),
        pl.BlockSpec((1, HY_W), lambda j, q, *_: (0, 0)),
        mat(), mat(),
        pl.BlockSpec((radix, None, 2, p, HY_W), lambda j, q, *_: (0, q, 0, 0, order)),
    ]
    kwargs = {}
    if prev_out is not None:
        args.append(prev_out)
        in_specs.append(pl.BlockSpec(memory_space=pl.ANY))
        kwargs["input_output_aliases"] = {len(args) - 1: 0}
    return pl.pallas_call(
        functools.partial(_conv_kernel, length=length, p=p, radix=radix),
        grid_spec=pltpu.PrefetchScalarGridSpec(
            num_scalar_prefetch=2,
            grid=(nb // 2, nq),
            in_specs=in_specs,
            out_specs=pl.BlockSpec((blk, HY_W), lambda j, q, *_: (off + j, 0)),
            scratch_shapes=[pltpu.VMEM((blk, HY_W), F32)],
        ),
        out_shape=jax.ShapeDtypeStruct((t_all, HY_W), BF16),
        compiler_params=_cparams("arbitrary", "arbitrary"),
        name="hyena_long_conv",
        **kwargs,
    )(*args)


def _attn_kernel(sink_ref, q_ref, *refs, local):
    if local:
        kp_ref, kc_ref, kn_ref, vp_ref, vc_ref, vn_ref, kx_ref, vx_ref, o_ref = refs
        n = pl.program_id(1)
        last = pl.num_programs(1) - 1
        qi = lax.broadcasted_iota(jnp.int32, (QBLK, QBLK), 0)
        ki = lax.broadcasted_iota(jnp.int32, (QBLK, QBLK), 1)
        ok_prev = (ki >= qi) & (n > 0)
        ok_next = (ki <= qi) & (n < last)
    else:
        kx_ref, vx_ref, o_ref = refs
    nt = (((1,), (1,)), ((), ()))
    nq = q_ref.shape[0]
    if local:
        bias1 = jnp.concatenate([jnp.where(ok_prev, 0.0, NEG_INF), jnp.zeros((QBLK, QBLK), F32),
                                 jnp.where(ok_next, 0.0, NEG_INF), jnp.zeros((QBLK, kx_ref.shape[0]), F32)], axis=1)
        bias = jnp.concatenate([bias1] * Q_PER_KV, axis=0)
    for kv in range(N_KV_HEADS):
        cs = slice(kv * HEAD_DIM, (kv + 1) * HEAD_DIM)
        heads = range(kv * Q_PER_KV, (kv + 1) * Q_PER_KV)
        qs = jnp.concatenate([q_ref[:, h * HEAD_DIM:(h + 1) * HEAD_DIM] for h in heads], axis=0)
        sink = jnp.concatenate([jnp.full((nq, 1), sink_ref[h], F32) for h in heads], axis=0)
        if local:
            keys = jnp.concatenate([kp_ref[:, cs], kc_ref[:, cs], kn_ref[:, cs], kx_ref[:, cs]], axis=0)
            vals = jnp.concatenate([vp_ref[:, cs], vc_ref[:, cs], vn_ref[:, cs], vx_ref[:, cs]], axis=0)
        else:
            keys, vals = kx_ref[:, cs], vx_ref[:, cs]
        s = lax.dot_general(qs, keys, nt, preferred_element_type=F32)
        if local:
            s = s + bias
        mx = jnp.maximum(jnp.max(s, axis=-1, keepdims=True), sink)
        p = jnp.exp((s - mx).astype(BF16))
        vals1 = jnp.concatenate([vals, jnp.ones(vals.shape, BF16)], axis=1)
        pv = _dot(p, vals1)
        den = pv[:, HEAD_DIM:HEAD_DIM + 1] + jnp.exp(sink - mx)
        out = pv[:, :HEAD_DIM] / den
        for g, h in enumerate(heads):
            o_ref[:, h * HEAD_DIM:(h + 1) * HEAD_DIM] = out[g * nq:(g + 1) * nq]


def _attention(q, k, v, sink, nb, n, c):
    t_all = q.shape[0]
    nqb = n // QBLK
    ctx_blk0 = nb * n // c
    kvspec = lambda fn: pl.BlockSpec((QBLK, KV_W), fn)
    prev = lambda b, j: (b * nqb + jnp.maximum(j - 1, 0), 0)
    cur = lambda b, j: (b * nqb + j, 0)
    nxt = lambda b, j: (b * nqb + jnp.minimum(j + 1, nqb - 1), 0)
    ctxs = pl.BlockSpec((c, KV_W), lambda b, j: (ctx_blk0 + b, 0))
    smem = pl.BlockSpec(memory_space=pltpu.SMEM)
    lat = pl.pallas_call(
        functools.partial(_attn_kernel, local=True),
        grid=(nb, nqb),
        in_specs=[smem, pl.BlockSpec((QBLK, ATT_W), cur),
                  kvspec(prev), kvspec(cur), kvspec(nxt), kvspec(prev), kvspec(cur), kvspec(nxt), ctxs, ctxs],
        out_specs=pl.BlockSpec((QBLK, ATT_W), cur),
        out_shape=jax.ShapeDtypeStruct((t_all, ATT_W), F32),
        compiler_params=_cparams("arbitrary", "arbitrary"),
        name="banded_attention",
    )(sink, q, k, k, k, v, v, v, k, v)
    ctx1 = pl.BlockSpec((c, KV_W), lambda b: (ctx_blk0 + b, 0))
    return pl.pallas_call(
        lambda s, qq, kx, vx, prev_o, o: _attn_kernel(s, qq, kx, vx, o, local=False),
        grid=(nb,),
        in_specs=[smem, pl.BlockSpec((c, ATT_W), lambda b: (ctx_blk0 + b, 0)), ctx1, ctx1,
                  pl.BlockSpec(memory_space=pl.ANY)],
        out_specs=pl.BlockSpec((c, ATT_W), lambda b: (ctx_blk0 + b, 0)),
        out_shape=jax.ShapeDtypeStruct((t_all, ATT_W), F32),
        input_output_aliases={4: 0},
        compiler_params=_cparams("arbitrary"),
        name="context_attention",
    )(sink, q, k, v, lat)


def _s5disc_kernel(lre_ref, lim_ref, dt_ref, bre_ref, bim_ref, are_ref, aim_ref, bbre_ref, bbim_ref):
    lam_re = lre_ref[...]
    lam_im = lim_ref[...]
    dt = jnp.exp(dt_ref[...])
    mag = jnp.exp(lam_re * dt)
    ab_re = mag * jnp.cos(lam_im * dt)
    ab_im = mag * jnp.sin(lam_im * dt)
    num_re = ab_re - 1.0
    num_im = ab_im
    den = lam_re * lam_re + lam_im * lam_im
    co_re = (num_re * lam_re + num_im * lam_im) / den
    co_im = (num_im * lam_re - num_re * lam_im) / den
    b_re = bre_ref[...]
    b_im = bim_ref[...]
    are_ref[...] = ab_re
    aim_ref[...] = ab_im
    bbre_ref[...] = co_re * b_re - co_im * b_im
    bbim_ref[...] = co_re * b_im + co_im * b_re


def _s5_discretize(lam_re, lam_im, log_dt, b_re, b_im):
    lead = lam_re.shape[:-1]
    rep = lambda a: jnp.repeat(a.reshape(-1, 1, S5_STATE), S5_CPG, axis=1).reshape(-1, S5_STATE)
    dt = jnp.broadcast_to(log_dt.reshape(-1, 1, 1), (math.prod(lead), S5_CPG, S5_STATE)).reshape(-1, S5_STATE)
    tr = lambda b: jnp.swapaxes(b, -1, -2).reshape(-1, S5_STATE)
    rows = math.prod(lead) * S5_CPG
    shp = jax.ShapeDtypeStruct((rows, S5_STATE), F32)
    ab_re, ab_im, bb_re, bb_im = pl.pallas_call(
        _s5disc_kernel, out_shape=[shp] * 4, name="s5_discretize",
    )(rep(lam_re), rep(lam_im), dt, tr(b_re), tr(b_im))
    full = lead + (S5_CPG, S5_STATE)
    return (ab_re.reshape(full)[..., 0, :], ab_im.reshape(full)[..., 0, :],
            bb_re.reshape(full), bb_im.reshape(full))


def _s5_kernel(*refs, tc, nbatch, n_ctx_steps):
    u_refs = (refs[:nbatch], refs[nbatch:2 * nbatch])
    bmat_ref, cmat_ref, a_ref = refs[2 * nbatch:2 * nbatch + 3]
    y_refs = (refs[2 * nbatch + 3:2 * nbatch + 5], refs[2 * nbatch + 5:2 * nbatch + 7])
    st_ref, bu_ref, tm_ref = refs[2 * nbatch + 7:]
    i = pl.program_id(0)
    gp = S5_GROUPS * S5_STATE
    halves = S5_W // 128

    @pl.when(i == 0)
    def _():
        st_ref[...] = jnp.zeros(st_ref.shape, F32)

    for d in range(2):
        for b in range(nbatch):
            for j in range(halves):
                tm_ref[j, pl.ds(b, tc, stride=nbatch), :] = u_refs[d][b][:, j * 128:(j + 1) * 128]
        u_tm = jnp.concatenate([tm_ref[j] for j in range(halves)], axis=1)
        bu_ref[...] = _dot(u_tm.astype(BF16), bmat_ref[d])
        ar = a_ref[d, 0]
        ai = a_ref[d, 1]

        def body(j, carry, d=d, ar=ar, ai=ai):
            hr, hi = carry
            t = j if d == 0 else tc - 1 - j
            r0 = pl.multiple_of(t * nbatch, nbatch)
            nr = ar * hr - ai * hi + bu_ref[pl.ds(r0, nbatch), 0:gp]
            ni = ar * hi + ai * hr + bu_ref[pl.ds(r0, nbatch), gp:2 * gp]
            bu_ref[pl.ds(r0, nbatch), 0:gp] = nr
            bu_ref[pl.ds(r0, nbatch), gp:2 * gp] = ni
            return nr, ni

        hr, hi = lax.fori_loop(0, tc, body, (st_ref[d, 0], st_ref[d, 1]))
        st_ref[d, 0] = hr
        st_ref[d, 1] = hi
        y = _dot(bu_ref[...].astype(BF16), cmat_ref[d])
        for j in range(halves):
            tm_ref[j] = y[:, j * 128:(j + 1) * 128]

        def emit(y_ref):
            for b in range(nbatch):
                for j in range(halves):
                    y_ref[b, :, j * 128:(j + 1) * 128] = tm_ref[j, pl.ds(b, tc, stride=nbatch), :]

        pl.when(i >= n_ctx_steps)(functools.partial(emit, y_refs[d][0]))
        pl.when(i < n_ctx_steps)(functools.partial(emit, y_refs[d][1]))


def _s5_scan(u, bmat, cmat, a_bc, nb, n, c):
    tc = S5_TC
    rows = tc * nb
    nl, nc = n // tc, c // tc
    gp = S5_GROUPS * S5_STATE
    ctx0 = nb * nl

    def chunk(step, b, backward):
        in_ctx = step < nc
        ctx_chunk = (nc - 1 - step) if backward else step
        lat_chunk = (nl - 1 - (step - nc)) if backward else (step - nc)
        return jnp.where(in_ctx, ctx0 + b * nc + ctx_chunk, b * nl + lat_chunk)

    specs = [pl.BlockSpec((tc, S5_W), lambda i, b=b, bw=bw: (chunk(i, b, bw), 0))
             for bw in (False, True) for b in range(nb)]
    full = lambda arr: pl.BlockSpec(arr.shape, lambda i: (0,) * arr.ndim)
    oblk = lambda fn: pl.BlockSpec((nb, tc, S5_W), lambda i: (0, fn(i), 0))
    out_specs = [oblk(lambda i: jnp.maximum(i - nc, 0)), oblk(lambda i: jnp.minimum(i, nc - 1)),
                 oblk(lambda i: jnp.minimum(nl + nc - 1 - i, nl - 1)), oblk(lambda i: jnp.maximum(nc - 1 - i, 0))]
    lat = jax.ShapeDtypeStruct((nb, n, S5_W), F32)
    ctx = jax.ShapeDtypeStruct((nb, c, S5_W), F32)
    return pl.pallas_call(
        functools.partial(_s5_kernel, tc=tc, nbatch=nb, n_ctx_steps=nc),
        grid=(nl + nc,),
        in_specs=specs + [full(bmat), full(cmat), full(a_bc)],
        out_specs=out_specs,
        out_shape=[lat, ctx, lat, ctx],
        scratch_shapes=[pltpu.VMEM((2, 2, nb, gp), F32), pltpu.VMEM((rows, 2 * gp), F32),
                        pltpu.VMEM((S5_W // 128, rows, 128), F32)],
        compiler_params=_cparams("arbitrary"),
        name="s5_scan",
    )(*([u] * (2 * nb)), bmat, cmat, a_bc)


def _s5_matrices(ab_re, ab_im, bb_re, bb_im, c_re, c_im, nb):
    eye = jnp.eye(S5_GROUPS, dtype=F32)
    bd_in = lambda b: jnp.einsum("dgcp,gh->dgchp", b, eye).reshape(2, S5_W, S5_GROUPS * S5_STATE)
    bd_out = lambda cc: jnp.einsum("dgcp,gh->dgphc", cc, eye).reshape(2, S5_GROUPS * S5_STATE, S5_W)
    bmat = jnp.concatenate([bd_in(bb_re), bd_in(bb_im)], axis=-1).astype(BF16)
    cmat = jnp.concatenate([bd_out(c_re), -bd_out(c_im)], axis=1).astype(BF16)
    a = jnp.stack([ab_re.reshape(2, -1), ab_im.reshape(2, -1)], axis=1)
    a_bc = jnp.broadcast_to(a[:, :, None, :], (2, 2, nb, S5_GROUPS * S5_STATE))
    return bmat, cmat, a_bc


def _out_kernel(x_ref, hy_ref, att_ref, yfl_ref, yfc_ref, ybl_ref, ybc_ref, u_ref, d_ref, gluw_ref, glub_ref,
                ng_ref, wout_ref, g1_ref, sh2_ref, sc2_ref, n2g_ref, rw_ref, rb_ref, tri_ref,
                xo_ref, h2_ref, eidx_ref, gate_ref, rank_ref, cnt_ref, *, n_lat_tiles):
    is_ctx = pl.program_id(0) >= n_lat_tiles
    y = jnp.where(is_ctx, yfc_ref[...] + ybc_ref[...], yfl_ref[...] + ybl_ref[...])
    y = y + u_ref[...] * d_ref[...]
    g = jax.nn.gelu(y)
    s5 = g * jax.nn.sigmoid(_dot(g.astype(BF16), gluw_ref[...]) + glub_ref[...])

    def nrm(part):
        return part * lax.rsqrt(jnp.mean(part * part, axis=-1, keepdims=True) + EPS)

    mix = jnp.concatenate([nrm(hy_ref[...].astype(F32)), nrm(att_ref[...]), nrm(s5)], axis=-1) * ng_ref[...]
    x = x_ref[...] + g1_ref[...] * _dot(mix.astype(BF16), wout_ref[...])
    xo_ref[...] = x
    h2 = _rms(x, n2g_ref[...]) * (1.0 + sc2_ref[...]) + sh2_ref[...]
    h2_ref[...] = h2.astype(h2_ref.dtype)

    logits = lax.dot_general(rw_ref[...], h2, (((1,), (1,)), ((), ())), precision=HIGHEST,
                             preferred_element_type=F32)
    e = jnp.exp(logits - jnp.max(logits, axis=0, keepdims=True))
    probs = e / jnp.sum(e, axis=0, keepdims=True)
    sel = probs + rb_ref[...]
    rows = [sel[r:r + 1, :] for r in range(N_EXPERTS)]
    best = None
    for grp in range(N_EXPERT_GROUPS):
        a = rows[grp * EXPERTS_PER_GROUP:(grp + 1) * EXPERTS_PER_GROUP]
        score = None
        for p0 in range(EXPERTS_PER_GROUP):
            for p1 in range(p0 + 1, EXPERTS_PER_GROUP):
                pair = a[p0] + a[p1]
                score = pair if score is None else jnp.maximum(score, pair)
        if best is None:
            best, gidx = score, jnp.zeros(score.shape, jnp.int32)
        else:
            better = score > best
            gidx = jnp.where(better, grp, gidx)
            best = jnp.where(better, score, best)
    ing = []
    for j in range(EXPERTS_PER_GROUP):
        v = rows[j]
        for grp in range(1, N_EXPERT_GROUPS):
            v = jnp.where(gidx == grp, rows[grp * EXPERTS_PER_GROUP + j], v)
        ing.append(v)
    first_v, first_i = ing[0], jnp.zeros(gidx.shape, jnp.int32)
    for j in range(1, EXPERTS_PER_GROUP):
        better = ing[j] > first_v
        first_i = jnp.where(better, j, first_i)
        first_v = jnp.where(better, ing[j], first_v)
    second_v, second_i = None, None
    for j in range(EXPERTS_PER_GROUP):
        cand = jnp.where(first_i == j, -jnp.inf, ing[j])
        if second_v is None:
            second_v, second_i = cand, jnp.zeros(gidx.shape, jnp.int32)
        else:
            better = cand > second_v
            second_i = jnp.where(better, j, second_i)
            second_v = jnp.where(better, cand, second_v)
    e0 = gidx * EXPERTS_PER_GROUP + first_i
    e1 = gidx * EXPERTS_PER_GROUP + second_i
    eid = lax.broadcasted_iota(jnp.int32, probs.shape, 0)
    oh0 = eid == e0
    oh1 = eid == e1
    p0v = jnp.sum(jnp.where(oh0, probs, 0.0), axis=0, keepdims=True)
    p1v = jnp.sum(jnp.where(oh1, probs, 0.0), axis=0, keepdims=True)
    tot = p0v + p1v
    member = jnp.where(oh0 | oh1, 1.0, 0.0)
    before = _dot(member.astype(BF16), tri_ref[...])
    r0 = jnp.sum(jnp.where(oh0, before, 0.0), axis=0, keepdims=True)
    r1 = jnp.sum(jnp.where(oh1, before, 0.0), axis=0, keepdims=True)
    cnt_ref[...] = jnp.sum(member, axis=1, keepdims=True)
    eidx_ref[...] = jnp.concatenate([e0, e1], axis=0)
    gate_ref[...] = jnp.concatenate([p0v / tot, p1v / tot], axis=0)
    rank_ref[...] = jnp.concatenate([r0, r1], axis=0).astype(jnp.int32)


def _mixer_out(x_all, hy, att, y_scan, u, s5_d, glu_w_bf, glu_b, mix_g, w_out_bf, mod4, n2g,
               router_wt, router_b, nb, n, layer):
    t_all = x_all.shape[0]
    n_lat_tiles = nb * n // TM
    tpb = n // TM
    assert y_scan[1].shape[1] == TM
    mrow = lambda i: jnp.where(i < n_lat_tiles, i // tpb, nb)
    row = lambda w: pl.BlockSpec((TM, w), lambda i: (i, 0))
    full = lambda arr: pl.BlockSpec(arr.shape, lambda i: (0,) * arr.ndim)
    layer_mat = lambda arr: pl.BlockSpec((None,) + arr.shape[1:], lambda i: (layer, 0, 0))
    modspec = lambda k: pl.BlockSpec((None, None, 1, D_MODEL), lambda i, k=k: (mrow(i), k, 0, 0))
    lat_y = pl.BlockSpec((None, TM, S5_W), lambda i: (jnp.minimum(i, n_lat_tiles - 1) // tpb,
                                                       jnp.minimum(i, n_lat_tiles - 1) % tpb, 0))
    ctx_y = pl.BlockSpec((None, TM, S5_W), lambda i: (jnp.maximum(i - n_lat_tiles, 0), 0, 0))
    tri = (lax.broadcasted_iota(jnp.int32, (TM, TM), 0) < lax.broadcasted_iota(jnp.int32, (TM, TM), 1)).astype(BF16)
    v1 = lambda a: a.reshape(1, -1)
    ins = [v1(s5_d), glu_w_bf, v1(glu_b), v1(mix_g), w_out_bf]
    in_w_specs = [full(ins[0]), layer_mat(glu_w_bf), full(ins[2]), full(ins[3]), layer_mat(w_out_bf)]
    tail = [v1(n2g), router_wt, router_b.reshape(N_EXPERTS, 1), tri]
    tok = lambda: pl.BlockSpec((2, TM), lambda i: (0, i))
    return pl.pallas_call(
        functools.partial(_out_kernel, n_lat_tiles=n_lat_tiles),
        grid=(t_all // TM,),
        in_specs=[row(D_MODEL), row(HY_W), row(ATT_W), lat_y, ctx_y, lat_y, ctx_y, row(S5_W)] + in_w_specs
                 + [modspec(2), modspec(3), modspec(4)] + [full(a) for a in tail],
        out_specs=[row(D_MODEL), row(D_MODEL), tok(), tok(), tok(),
                   pl.BlockSpec((None, N_EXPERTS, 1), lambda i: (i, 0, 0))],
        out_shape=[
            jax.ShapeDtypeStruct((t_all, D_MODEL), F32),
            jax.ShapeDtypeStruct((t_all, D_MODEL), BF16),
            jax.ShapeDtypeStruct((2, t_all), jnp.int32),
            jax.ShapeDtypeStruct((2, t_all), F32),
            jax.ShapeDtypeStruct((2, t_all), jnp.int32),
            jax.ShapeDtypeStruct((t_all // TM, N_EXPERTS, 1), F32),
        ],
        compiler_params=_cparams("arbitrary"),
        name="mixer_out_router",
    )(x_all, hy, att, *y_scan, u, *ins, mod4, mod4, mod4, *tail)


SORT_ROWS = 2 * TM + 8 * N_EXPERTS
CHUNK_BITS = 6
TOT_BITS = 7
TAIL_BITS = 6


def _for_chunks(n8, bits, fn):
    for b in range(bits - 1, -1, -1):
        done = ((n8 >> (b + 1)) << (b + 1)) * 8

        @pl.when(((n8 >> b) & 1) == 1)
        def _(done=done, rows=8 << b):
            fn(pl.multiple_of(done, 8), rows)


def _dispatch_kernel(loc_ref, dst_ref, n8_ref, tot_ref, tail_ref, h2_ref, eidx_ref, lrank_ref, adj_ref,
                     xb_ref, spos_ref, sorted_ref, zero_ref, sem, zsem):
    i = pl.program_id(0)
    n_tiles = pl.num_programs(0)
    slot = i % 2

    def wait_tile(tile, buf):
        def wait(off, rows):
            pltpu.make_async_copy(sorted_ref.at[buf, pl.ds(0, rows)], xb_ref.at[pl.ds(0, rows)], sem.at[buf]).wait()
        _for_chunks(tot_ref[tile], TOT_BITS, wait)

    @pl.when(i >= 2)
    def _():
        wait_tile(i - 2, slot)

    @pl.when(i == 0)
    def _():
        zero_ref[...] = jnp.zeros(zero_ref.shape, F32)
        for start_wait in (True, False):
            for e in range(N_EXPERTS):
                def tail(off, rows, e=e, start_wait=start_wait):
                    cp = pltpu.make_async_copy(zero_ref.at[pl.ds(0, rows)],
                                               xb_ref.at[pl.ds(pl.multiple_of(tail_ref[e] + off, 8), rows)], zsem)
                    cp.start() if start_wait else cp.wait()
                _for_chunks(tail_ref[N_EXPERTS + e], TAIL_BITS, tail)

    eid = lax.broadcasted_iota(jnp.int32, (N_EXPERTS, TM), 0)
    adj = adj_ref[...]
    spos = []
    for k in range(2):
        chunk_start = jnp.sum(jnp.where(eid == eidx_ref[k:k + 1, :], adj, 0.0), axis=0, keepdims=True)
        spos.append(lrank_ref[k:k + 1, :] + chunk_start.astype(jnp.int32))
    spos_ref[...] = jnp.concatenate(spos, axis=0)
    rows = lax.broadcasted_iota(jnp.int32, (SORT_ROWS, TM), 0)
    perm = jnp.where((rows == spos[0]) | (rows == spos[1]), 1.0, 0.0).astype(BF16)
    sorted_ref[slot] = _dot(perm, h2_ref[...])

    for e in range(N_EXPERTS):
        idx = i * N_EXPERTS + e

        def send(off, rows, idx=idx):
            pltpu.make_async_copy(sorted_ref.at[slot, pl.ds(pl.multiple_of(loc_ref[idx] + off, 8), rows)],
                                  xb_ref.at[pl.ds(pl.multiple_of(dst_ref[idx] + off, 8), rows)], sem.at[slot]).start()
        _for_chunks(n8_ref[idx], CHUNK_BITS, send)

    @pl.when(i == n_tiles - 1)
    def _():
        wait_tile(i - 1, 1 - slot)
        wait_tile(i, slot)


def _dispatch(tables, h2, eidx, lrank, adj, n_slots):
    t_all = h2.shape[0]
    tok = lambda: pl.BlockSpec((2, TM), lambda i, *_: (0, i))
    return pl.pallas_call(
        _dispatch_kernel,
        grid_spec=pltpu.PrefetchScalarGridSpec(
            num_scalar_prefetch=5,
            grid=(t_all // TM,),
            in_specs=[pl.BlockSpec((TM, D_MODEL), lambda i, *_: (i, 0)), tok(), tok(),
                      pl.BlockSpec((None, N_EXPERTS, 1), lambda i, *_: (i, 0, 0))],
            out_specs=[pl.BlockSpec(memory_space=pl.ANY), tok()],
            scratch_shapes=[pltpu.VMEM((2, SORT_ROWS, D_MODEL), F32), pltpu.VMEM((MOE_ROWS // 2, D_MODEL), F32),
                            pltpu.SemaphoreType.DMA((2,)), pltpu.SemaphoreType.DMA(())],
        ),
        out_shape=[jax.ShapeDtypeStruct((n_slots, D_MODEL), F32), jax.ShapeDtypeStruct((2, t_all), jnp.int32)],
        compiler_params=_cparams("arbitrary"),
        name="moe_dispatch",
    )(*tables, h2, eidx, lrank, adj)


def _ffn_kernel(be_ref, nu_ref, x_ref, wg_ref, wu_ref, wd_ref, o_ref, wbf_ref):
    j = pl.program_id(0)

    @pl.when(j < nu_ref[0])
    def _():
        @pl.when((j == 0) | (be_ref[j] != be_ref[jnp.maximum(j - 1, 0)]))
        def _():
            wbf_ref[0] = wg_ref[...].astype(BF16)
            wbf_ref[1] = wu_ref[...].astype(BF16)
            wbf_ref[2] = wd_ref[...].astype(BF16)

        x = x_ref[...].astype(BF16)
        a = _dot(x, wbf_ref[0])
        u = _dot(x, wbf_ref[1])
        hidden = (a * jax.nn.sigmoid(a)) * u
        o_ref[...] = _dot(hidden.astype(BF16), wbf_ref[2])


def _expert_ffn(block_e, n_used, xb, wg, wu, wd, layer):
    n_slots = xb.shape[0]
    blk = lambda j, be, nu: jnp.minimum(j, nu[0] - 1)
    wspec = lambda: pl.BlockSpec((None, None, D_MODEL, D_MODEL),
                                 lambda j, be, nu: (layer, be[blk(j, be, nu)], 0, 0))
    return pl.pallas_call(
        _ffn_kernel,
        grid_spec=pltpu.PrefetchScalarGridSpec(
            num_scalar_prefetch=2,
            grid=(n_slots // MOE_ROWS,),
            in_specs=[pl.BlockSpec((MOE_ROWS, D_MODEL), lambda j, be, nu: (blk(j, be, nu), 0)),
                      wspec(), wspec(), wspec()],
            out_specs=pl.BlockSpec((MOE_ROWS, D_MODEL), lambda j, be, nu: (blk(j, be, nu), 0)),
            scratch_shapes=[pltpu.VMEM((3, D_MODEL, D_MODEL), BF16)],
        ),
        out_shape=jax.ShapeDtypeStruct((n_slots, D_MODEL), F32),
        compiler_params=_cparams("arbitrary"),
        name="moe_expert_ffn",
    )(block_e, n_used, xb, wg, wu, wd)


def _combine_kernel(loc_ref, dst_ref, n8_ref, tot_ref, x_ref, spos_ref, gates_ref, g2_ref, yb_ref, *rest,
                    final):
    if final:
        fg_ref, xo_ref, ybuf_ref, sem = rest
    else:
        xo_ref, ybuf_ref, sem = rest
    i = pl.program_id(0)
    n_tiles = pl.num_programs(0)
    slot = i % 2

    def fetch_tile(tile, buf):
        for e in range(N_EXPERTS):
            idx = tile * N_EXPERTS + e

            def fetch(off, rows, idx=idx):
                pltpu.make_async_copy(yb_ref.at[pl.ds(pl.multiple_of(dst_ref[idx] + off, 8), rows)],
                                      ybuf_ref.at[buf, pl.ds(pl.multiple_of(loc_ref[idx] + off, 8), rows)],
                                      sem.at[buf]).start()
            _for_chunks(n8_ref[idx], CHUNK_BITS, fetch)

    @pl.when(i == 0)
    def _():
        ybuf_ref[...] = jnp.zeros(ybuf_ref.shape, F32)
        fetch_tile(0, 0)

    @pl.when(i + 1 < n_tiles)
    def _():
        fetch_tile(i + 1, 1 - slot)

    def wait(off, rows):
        pltpu.make_async_copy(yb_ref.at[pl.ds(0, rows)], ybuf_ref.at[slot, pl.ds(0, rows)], sem.at[slot]).wait()
    _for_chunks(tot_ref[i], TOT_BITS, wait)

    y_sorted = ybuf_ref[slot].astype(BF16)
    lane = lax.broadcasted_iota(jnp.int32, (TM, SORT_ROWS), 1)
    gates = gates_ref[...]
    y = jnp.zeros((TM, D_MODEL), F32)
    for k in range(2):
        pick = jnp.where(lane == spos_ref[:, k:k + 1], 1.0, 0.0).astype(BF16)
        y = y + gates[:, k:k + 1] * _dot(pick, y_sorted)
    x = x_ref[...] + g2_ref[...] * y
    xo_ref[...] = _rms(x, fg_ref[...]) if final else x


def _combine(tables, x_all, spos_t, gates_t, mod4, yb, nb, n, final_g=None):
    final = final_g is not None
    t_all = nb * n if final else x_all.shape[0]
    n_lat_tiles = nb * n // TM
    tpb = n // TM
    mrow = lambda i: jnp.where(i < n_lat_tiles, i // tpb, nb)
    extra_specs = [pl.BlockSpec((1, D_MODEL), lambda i, *_: (0, 0))] if final else []
    extra_args = [final_g.reshape(1, D_MODEL)] if final else []
    return pl.pallas_call(
        functools.partial(_combine_kernel, final=final),
        grid_spec=pltpu.PrefetchScalarGridSpec(
            num_scalar_prefetch=4,
            grid=(t_all // TM,),
            in_specs=[pl.BlockSpec((TM, D_MODEL), lambda i, *_: (i, 0)),
                      pl.BlockSpec((TM, 2), lambda i, *_: (i, 0)),
                      pl.BlockSpec((TM, 2), lambda i, *_: (i, 0)),
                      pl.BlockSpec((None, None, 1, D_MODEL), lambda i, *_: (mrow(i), 5, 0, 0)),
                      pl.BlockSpec(memory_space=pl.ANY)] + extra_specs,
            out_specs=pl.BlockSpec((TM, D_MODEL), lambda i, *_: (i, 0)),
            scratch_shapes=[pltpu.VMEM((2, SORT_ROWS, D_MODEL), F32), pltpu.SemaphoreType.DMA((2,))],
        ),
        out_shape=jax.ShapeDtypeStruct((t_all, D_MODEL), F32),
        compiler_params=_cparams("arbitrary"),
        name="moe_combine",
    )(*tables, x_all, spos_t, gates_t, mod4, yb, *extra_args)


def kernel(x, c, ctx, c_ctx, norm1_g, norm2_g, ada_w, ada_b, w_in, w_out, mix_norm_g, hy_conv_w, hy_conv_b, hy_f_w1, hy_f_b1, hy_f_freq1, hy_f_w2, hy_f_b2, hy_f_freq2, hy_f_w3, hy_bias, attn_sink, s5_lam_re, s5_lam_im, s5_log_dt, s5_b_re, s5_b_im, s5_c_re, s5_c_im, s5_d, s5_glu_w, s5_glu_b, router_w, router_b, moe_w_gate, moe_w_up, moe_w_down, final_g):
    nb, n, d = x.shape
    cl = ctx.shape[1]
    depth = w_in.shape[0]
    t_all = nb * (n + cl)

    cc = jnp.zeros((16, d), F32).at[:nb].set(c).at[nb].set(c_ctx)
    mod_all = _modulation(cc, ada_w, ada_b).reshape(depth, 16, 6, 1, d)
    cos, sin = _rope_tables(n)
    ab_re, ab_im, bb_re, bb_im = _s5_discretize(s5_lam_re, s5_lam_im, s5_log_dt, s5_b_re, s5_b_im)

    tabs = {}
    for length in (n, cl):
        twr, twi, er, ei = _dft_tables(length)
        tabs[length] = (twr, twi, er.astype(BF16), ei.astype(BF16))

    router_wt = router_w.T
    w_in_bf, w_out_bf, glu_w_bf = w_in.astype(BF16), w_out.astype(BF16), s5_glu_w.astype(BF16)
    expert_ids = jnp.arange(N_EXPERTS, dtype=jnp.int32)
    n_tiles = t_all // TM
    tile_ids = jnp.arange(n_tiles, dtype=jnp.int32)
    n_slots = (-(-(2 * t_all + 7 * N_EXPERTS * n_tiles) // MOE_ROWS) + N_EXPERTS) * MOE_ROWS
    x_all = jnp.concatenate([x.reshape(nb * n, d), ctx.reshape(nb * cl, d)], axis=0)

    for l in range(depth):
        mod4 = mod_all[l]
        hy, q, k, v, u = _input_proj(x_all, norm1_g[l], mod4, w_in_bf, cos, sin, nb, n, l)

        zc = _short_conv(hy, hy_conv_w[l], hy_conv_b[l], nb, n, cl)
        hy_out = None
        z1 = None
        specs = {}
        for length in (n, cl):
            specs[length] = _filter_spectrum(length, tabs[length], hy_f_w1[l], hy_f_b1[l], hy_f_freq1[l],
                                             hy_f_w2[l], hy_f_b2[l], hy_f_freq2[l], hy_f_w3[l])
        for length, off in ((n, 0), (cl, nb * n)):
            z1 = _long_conv(zc, 0, zc, 1, hy_bias[l, 0], specs[length], 0, tabs[length], length, off, nb, z1)
        for length, off in ((n, 0), (cl, nb * n)):
            hy_out = _long_conv(z1, 0, zc, 2, hy_bias[l, 1], specs[length], 1, tabs[length], length, off, nb,
                                hy_out)

        att = _attention(q, k, v, attn_sink[l], nb, n, cl)

        bmat, cmat, a_bc = _s5_matrices(ab_re[l], ab_im[l], bb_re[l], bb_im[l], s5_c_re[l], s5_c_im[l], nb)
        y_scan = _s5_scan(u, bmat, cmat, a_bc, nb, n, cl)

        x_all, h2, eidx, gates, lrank, counts = _mixer_out(
            x_all, hy_out, att, y_scan, u, s5_d[l], glu_w_bf, s5_glu_b[l], mix_norm_g[l],
            w_out_bf, mod4, norm2_g[l], router_wt, router_b, nb, n, l)

        cnt = counts.reshape(n_tiles, N_EXPERTS).astype(jnp.int32)
        c8 = (cnt + 7) // 8 * 8
        loc = jnp.sum(jnp.where(expert_ids[None, :] < expert_ids[:, None], c8[:, None, :], 0), axis=2)
        seg_off = jnp.sum(jnp.where((tile_ids[None, :] < tile_ids[:, None])[:, :, None], c8[None, :, :], 0), axis=1)
        tot8 = jnp.sum(c8, axis=0)
        padded = (tot8 + MOE_ROWS - 1) // MOE_ROWS * MOE_ROWS
        ends = jnp.sum(jnp.where(expert_ids[None, :] <= expert_ids[:, None], padded[None, :], 0), axis=1)
        pstart = ends - padded
        dst = pstart[None, :] + seg_off
        tables = (loc.reshape(-1), dst.reshape(-1), (c8 // 8).reshape(-1), jnp.sum(c8, axis=1) // 8)
        tail = jnp.concatenate([pstart + tot8, (padded - tot8) // 8])
        n_blocks = n_slots // MOE_ROWS
        block_start = jnp.arange(n_blocks, dtype=jnp.int32) * MOE_ROWS
        block_e = jnp.minimum(jnp.sum((ends[None, :] <= block_start[:, None]).astype(jnp.int32), axis=1),
                              N_EXPERTS - 1)
        n_used = (ends[-1] // MOE_ROWS).reshape(1)

        xb, spos = _dispatch(tables + (tail,), h2, eidx, lrank, loc.astype(F32).reshape(n_tiles, N_EXPERTS, 1),
                             n_slots)
        yb = _expert_ffn(block_e, n_used, xb, moe_w_gate, moe_w_up, moe_w_down, l)
        x_all = _combine(tables, x_all, spos.T, gates.T, mod4, yb, nb, n, final_g if l == depth - 1 else None)

    return x_all.reshape(nb, n, d)
```

```python
import functools
import math

import jax
import jax.numpy as jnp
from jax import lax
from jax.experimental import pallas as pl
from jax.experimental.pallas import tpu as pltpu

F32 = jnp.float32
BF16 = jnp.bfloat16
HIGHEST = lax.Precision.HIGHEST

D_MODEL = 1024
GRID_W = 64
HY_W = 256
ATT_W = 512
S5_W = 256
HEAD_DIM = 64
N_HEADS = 8
N_KV_HEADS = 2
Q_PER_KV = 4
KV_W = 128
WINDOW = 128
QBLK = 128
ROPE_BASE = 10000.0
NEG_INF = -1e30
HY_BANDS = 16
HY_DECAY_MIN = math.log(1e-2) / 1.5
HY_DECAY_MAX = math.log(1e-2) / 0.3
S5_CPG = 16
S5_GROUPS = 16
S5_STATE = 64
N_EXPERTS = 16
N_EXPERT_GROUPS = 4
EXPERTS_PER_GROUP = 4
HY_END = 768
Q_END = 1280
K_END = 1408
V_END = 1536
IN_W = 1792
EPS = 1e-6

TM = 256
MOE_ROWS = 512
S5_TC = 64
DFT_P = 512
VMEM_LIMIT = 56 * 1024 * 1024


def _cparams(*sem):
    return pltpu.CompilerParams(dimension_semantics=sem, vmem_limit_bytes=VMEM_LIMIT)


def _dot(a, b):
    return jnp.dot(a, b, preferred_element_type=F32)


def _mod_kernel(c_ref, w_ref, b_ref, o_ref):
    c = c_ref[...]
    a = c * jax.nn.sigmoid(c)
    o_ref[...] = _dot(a.astype(BF16), w_ref[...].astype(BF16)) + b_ref[...]


def _modulation(cc, ada_w, ada_b):
    depth, d, w6 = ada_w.shape
    nb = 1536
    return pl.pallas_call(
        _mod_kernel,
        grid=(depth, w6 // nb),
        in_specs=[
            pl.BlockSpec((16, d), lambda l, j: (0, 0)),
            pl.BlockSpec((None, d, nb), lambda l, j: (l, 0, j)),
            pl.BlockSpec((None, 1, nb), lambda l, j: (l, 0, j)),
        ],
        out_specs=pl.BlockSpec((None, 16, nb), lambda l, j: (l, 0, j)),
        out_shape=jax.ShapeDtypeStruct((depth, 16, w6), F32),
        compiler_params=_cparams("arbitrary", "arbitrary"),
        name="adaln_mod",
    )(cc, ada_w, ada_b.reshape(depth, 1, w6))


def _rms(x, g):
    return x * lax.rsqrt(jnp.mean(x * x, axis=-1, keepdims=True) + EPS) * g


def _in_kernel(x_ref, g_ref, sh_ref, sc_ref, w_ref, cos_ref, sin_ref,
               hy_ref, q_ref, k_ref, v_ref, u_ref):
    h = _rms(x_ref[...], g_ref[...]) * (1.0 + sc_ref[...]) + sh_ref[...]
    p = _dot(h.astype(BF16), w_ref[...])
    hy_ref[...] = p[:, :HY_END].astype(hy_ref.dtype)
    cos = cos_ref[...]
    sin = sin_ref[...]
    lane = lax.broadcasted_iota(jnp.int32, cos.shape, 1)
    first = (lane % 32) < 16

    def rope(z):
        swapped = jnp.where(first, pltpu.roll(z, 112, 1), pltpu.roll(z, 16, 1))
        return z * cos + swapped * sin

    for j in range(ATT_W // 128):
        qj = rope(p[:, HY_END + j * 128:HY_END + (j + 1) * 128]) * (HEAD_DIM ** -0.5)
        q_ref[:, j * 128:(j + 1) * 128] = qj.astype(BF16)
    k_ref[...] = rope(p[:, Q_END:K_END]).astype(BF16)
    v_ref[...] = p[:, K_END:V_END].astype(BF16)
    u_ref[...] = p[:, V_END:]


def _rope_tables(n):
    quarter = HEAD_DIM // 4
    inv_freq = ROPE_BASE ** (-jnp.arange(quarter, dtype=F32) / quarter)
    t = jnp.arange(n)
    rows = (t // GRID_W).astype(F32)
    cols = (t % GRID_W).astype(F32)
    ang_r = rows[:, None] * inv_freq[None, :]
    ang_c = cols[:, None] * inv_freq[None, :]
    cos64 = jnp.concatenate([jnp.cos(ang_r)] * 2 + [jnp.cos(ang_c)] * 2, axis=-1)
    sin64 = jnp.concatenate([-jnp.sin(ang_r), jnp.sin(ang_r), -jnp.sin(ang_c), jnp.sin(ang_c)], axis=-1)
    cos = jnp.concatenate([jnp.tile(cos64, (1, 2)), jnp.ones((TM, 128), F32)], axis=0)
    sin = jnp.concatenate([jnp.tile(sin64, (1, 2)), jnp.zeros((TM, 128), F32)], axis=0)
    return cos, sin


def _input_proj(x_all, g, mod4, w_in_bf, cos, sin, nb, n, layer):
    t_all = x_all.shape[0]
    n_lat_tiles = nb * n // TM
    tpb = n // TM

    def mrow(i):
        return jnp.where(i < n_lat_tiles, i // tpb, nb)

    def trow(i):
        return jnp.where(i < n_lat_tiles, i % tpb, tpb)

    row = lambda w: pl.BlockSpec((TM, w), lambda i: (i, 0))
    return pl.pallas_call(
        _in_kernel,
        grid=(t_all // TM,),
        in_specs=[
            row(D_MODEL),
            pl.BlockSpec((1, D_MODEL), lambda i: (0, 0)),
            pl.BlockSpec((None, None, 1, D_MODEL), lambda i: (mrow(i), 0, 0, 0)),
            pl.BlockSpec((None, None, 1, D_MODEL), lambda i: (mrow(i), 1, 0, 0)),
            pl.BlockSpec((None, D_MODEL, IN_W), lambda i: (layer, 0, 0)),
            pl.BlockSpec((TM, 128), lambda i: (trow(i), 0)),
            pl.BlockSpec((TM, 128), lambda i: (trow(i), 0)),
        ],
        out_specs=[row(HY_END), row(ATT_W), row(KV_W), row(KV_W), row(S5_W)],
        out_shape=[
            jax.ShapeDtypeStruct((t_all, HY_END), BF16),
            jax.ShapeDtypeStruct((t_all, ATT_W), BF16),
            jax.ShapeDtypeStruct((t_all, KV_W), BF16),
            jax.ShapeDtypeStruct((t_all, KV_W), BF16),
            jax.ShapeDtypeStruct((t_all, S5_W), F32),
        ],
        compiler_params=_cparams("arbitrary"),
        name="norm_mod_inproj",
    )(x_all, g.reshape(1, D_MODEL), mod4, mod4, w_in_bf, cos, sin)


def _sconv_kernel(z_ref, w_ref, b_ref, o_ref):
    z = z_ref[...].astype(F32)
    length = z.shape[0]
    row = lax.broadcasted_iota(jnp.int32, z.shape, 0)
    zm = jnp.where(row == 0, 0.0, pltpu.roll(z, 1, 0))
    zp = jnp.where(row == length - 1, 0.0, pltpu.roll(z, length - 1, 0))
    o_ref[...] = (b_ref[...] + zm * w_ref[0:1, :] + z * w_ref[1:2, :] + zp * w_ref[2:3, :]).astype(o_ref.dtype)


def _short_conv(hy, conv_w, conv_b, nb, n, c):
    t_all = hy.shape[0]
    out = None
    for length, off in ((n, 0), (c, nb * n // c)):
        kwargs = {}
        args = [hy, conv_w, conv_b.reshape(1, HY_END)]
        in_specs = [
            pl.BlockSpec((length, 256), lambda b, j, off=off: (off + b, j)),
            pl.BlockSpec((3, 256), lambda b, j: (0, j)),
            pl.BlockSpec((1, 256), lambda b, j: (0, j)),
        ]
        kern = _sconv_kernel
        if out is not None:
            args.append(out)
            in_specs.append(pl.BlockSpec(memory_space=pl.ANY))
            kwargs["input_output_aliases"] = {3: 0}
            kern = lambda z, w, b, prev, o: _sconv_kernel(z, w, b, o)
        out = pl.pallas_call(
            kern,
            grid=(nb, 3),
            in_specs=in_specs,
            out_specs=pl.BlockSpec((length, 256), lambda b, j, off=off: (off + b, j)),
            out_shape=jax.ShapeDtypeStruct((t_all, HY_END), BF16),
            compiler_params=_cparams("arbitrary", "arbitrary"),
            name="hyena_short_conv",
            **kwargs,
        )(*args)
    return out


def _dft_plan(length):
    p = min(DFT_P, length)
    return p, 2 * length // p


def _dft_tables(length):
    p, na = _dft_plan(length)
    m = 2 * length
    a = jnp.arange(na, dtype=jnp.int32)
    ang_w = (2.0 * math.pi / na) * ((a[:, None] * a[None, :]) % na).astype(F32)
    twr, twi = jnp.cos(ang_w), -jnp.sin(ang_w)
    r = jnp.arange(p, dtype=jnp.int32)
    k = a[:, None, None] + na * r[None, :, None]
    ang = (2.0 * math.pi / m) * ((k * r[None, None, :]) % m).astype(F32)
    er, ei = jnp.cos(ang), -jnp.sin(ang)
    return twr, twi, er, ei


def _filter_kernel(twr_ref, twi_ref, w1_ref, b1_ref, f1_ref, w2_ref, b2_ref, f2_ref, w3_ref,
                   bands_ref, decay_ref, er_ref, ei_ref, h_ref, kern_ref, norm_ref, *, length, p, na):
    ka = pl.program_id(0)
    m = 2 * length
    hdot = functools.partial(jnp.dot, precision=HIGHEST, preferred_element_type=F32)

    @pl.when(ka == 0)
    def _():
        norm = jnp.zeros((1, 2 * HY_W), F32)
        for a in range(na):
            n_idx = a * p + lax.broadcasted_iota(jnp.int32, (p, 1), 0)
            is_fwd = n_idx < length
            pos = jnp.where(is_fwd, n_idx, m - n_idx).astype(F32)
            t = pos / float(max(length - 1, 1))
            ang = (2.0 * math.pi / length) * pos * bands_ref[...]
            pre = (t * w1_ref[0:1, :] + hdot(jnp.cos(ang), w1_ref[1:1 + HY_BANDS, :])
                   + hdot(-jnp.sin(ang), w1_ref[1 + HY_BANDS:, :]) + b1_ref[...])
            hid = jnp.sin(f1_ref[...] * pre)
            hid = jnp.sin(f2_ref[...] * (hdot(hid, w2_ref[...]) + b2_ref[...]))
            taps = hdot(hid, w3_ref[...])
            wnd = jnp.exp(-t * decay_ref[...])
            live = n_idx != length
            for o in range(2):
                fwd = taps[:, o * 2 * HY_W:o * 2 * HY_W + HY_W]
                bwd = taps[:, o * 2 * HY_W + HY_W:(o + 1) * 2 * HY_W]
                kern = jnp.where(live, jnp.where(is_fwd, fwd, bwd) * wnd, 0.0)
                kern_ref[a, :, o * HY_W:(o + 1) * HY_W] = kern
            norm = norm + jnp.sum(jnp.abs(kern_ref[a]), axis=0, keepdims=True)
        norm_ref[...] = norm

    gr = jnp.zeros((p, 2 * HY_W), F32)
    gi = jnp.zeros((p, 2 * HY_W), F32)
    for a in range(na):
        slab = kern_ref[a]
        gr = gr + twr_ref[ka, a] * slab
        gi = gi + twi_ref[ka, a] * slab
    er = er_ref[...]
    ei = ei_ref[...]
    grb = gr.astype(BF16)
    gib = gi.astype(BF16)
    inv = 1.0 / norm_ref[...]
    h_ref[0] = ((_dot(er, grb) - _dot(ei, gib)) * inv).astype(h_ref.dtype)
    h_ref[1] = ((_dot(ei, grb) + _dot(er, gib)) * inv).astype(h_ref.dtype)


def _filter_spectrum(length, tables, w1, b1, f1, w2, b2, f2, w3):
    p, na = _dft_plan(length)
    twr, twi, er, ei = tables
    bands = jnp.linspace(1e-4, HY_BANDS - 1, HY_BANDS, dtype=F32).reshape(1, HY_BANDS)
    decay = jnp.abs(jnp.linspace(HY_DECAY_MIN, HY_DECAY_MAX, HY_W, dtype=F32)).reshape(1, HY_W)
    full = lambda arr: pl.BlockSpec(arr.shape, lambda ka, *_: (0,) * arr.ndim)
    vec = lambda v: v.reshape(1, -1)
    ins = [w1, vec(b1), vec(f1), w2, vec(b2), vec(f2), w3, bands, decay]
    return pl.pallas_call(
        functools.partial(_filter_kernel, length=length, p=p, na=na),
        grid_spec=pltpu.PrefetchScalarGridSpec(
            num_scalar_prefetch=2,
            grid=(na,),
            in_specs=[full(x) for x in ins] + [
                pl.BlockSpec((None, p, p), lambda ka, *_: (ka, 0, 0)),
                pl.BlockSpec((None, p, p), lambda ka, *_: (ka, 0, 0)),
            ],
            out_specs=pl.BlockSpec((None, 2, p, 2 * HY_W), lambda ka, *_: (ka, 0, 0, 0)),
            scratch_shapes=[pltpu.VMEM((na, p, 2 * HY_W), F32), pltpu.VMEM((1, 2 * HY_W), F32)],
        ),
        out_shape=jax.ShapeDtypeStruct((na, 2, p, 2 * HY_W), BF16),
        compiler_params=_cparams("arbitrary"),
        name="hyena_filter_spectrum",
    )(twr, twi, *[vec(x) if x.ndim == 1 else x for x in ins], er, ei)


def _cadd(x, y):
    return x[0] + y[0], x[1] + y[1]


def _csub(x, y):
    return x[0] - y[0], x[1] - y[1]


def _radix_mix(parts, sign):
    if len(parts) == 1:
        return parts
    p0, p1, p2, p3 = parts
    t0, t1, t2, t3 = _cadd(p0, p2), _csub(p0, p2), _cadd(p1, p3), _csub(p1, p3)
    it3 = (-t3[1], t3[0])
    if sign < 0:
        return [_cadd(t0, t2), _csub(t1, it3), _csub(t0, t2), _cadd(t1, it3)]
    return [_cadd(t0, t2), _cadd(t1, it3), _csub(t0, t2), _csub(t1, it3)]


def _conv_kernel(twr_ref, twi_ref, u_ref, gate_ref, bias_ref, er_ref, ei_ref, h_ref, *rest, length, p, radix):
    o_ref, acc_ref = rest[-2], rest[-1]
    q = pl.program_id(1)
    nz = length // p
    tdot = lambda x, y: lax.dot_general(x, y, (((0,), (0,)), ((), ())), preferred_element_type=F32)

    parts = [None] * radix
    for a in range(nz):
        wr = twr_ref[q, a]
        wi = twi_ref[q, a]
        zr = u_ref[a * p:(a + 1) * p, :].astype(F32)
        zi = u_ref[length + a * p:length + (a + 1) * p, :].astype(F32)
        term = (wr * zr - wi * zi, wr * zi + wi * zr)
        parts[a % radix] = term if parts[a % radix] is None else _cadd(parts[a % radix], term)
    zero = jnp.zeros((p, o_ref.shape[-1]), F32)
    parts = [(zero, zero) if t is None else t for t in parts]
    g = _radix_mix(parts, -1)

    v = []
    for m in range(radix):
        grb = g[m][0].astype(BF16)
        gib = g[m][1].astype(BF16)
        er = er_ref[m]
        ei = ei_ref[m]
        sr = _dot(er, grb) - _dot(ei, gib)
        si = _dot(ei, grb) + _dot(er, gib)
        hr = h_ref[m, 0].astype(F32)
        hi = h_ref[m, 1].astype(F32)
        yr = (sr * hr - si * hi).astype(BF16)
        yi = (sr * hi + si * hr).astype(BF16)
        v.append((tdot(er, yr) + tdot(ei, yi), tdot(er, yi) - tdot(ei, yr)))
    qs = _radix_mix(v, +1)

    @pl.when(q == 0)
    def _():
        acc_ref[...] = jnp.zeros(acc_ref.shape, F32)

    scale = 1.0 / (2 * length)
    for a in range(nz):
        wr = twr_ref[q, a] * scale
        wi = twi_ref[q, a] * scale
        vr, vi = qs[a % radix]
        acc_ref[a * p:(a + 1) * p, :] += wr * vr + wi * vi
        acc_ref[length + a * p:length + (a + 1) * p, :] += wr * vi - wi * vr

    @pl.when(q == pl.num_programs(1) - 1)
    def _():
        u = u_ref[...].astype(F32)
        o_ref[...] = (gate_ref[...].astype(F32) * (acc_ref[...] + u * bias_ref[...])).astype(o_ref.dtype)


def _long_conv(u_arr, u_col, gate_arr, gate_col, bias, spec, order, tables_bf, length, row_off, nb, prev_out):
    p, na = _dft_plan(length)
    radix = 4 if na % 4 == 0 and na >= 8 else 1
    nq = na // radix
    twr, twi, er, ei = tables_bf
    t_all = u_arr.shape[0]
    blk = 2 * length
    off = row_off // blk
    mat = lambda: pl.BlockSpec((radix, None, p, p), lambda j, q, *_: (0, q, 0, 0))
    args = [twr, twi, u_arr, gate_arr, bias.reshape(1, HY_W), er.reshape(radix, nq, p, p),
            ei.reshape(radix, nq, p, p), spec.reshape(radix, nq, 2, p, 2 * HY_W)]
    in_specs = [
        pl.BlockSpec((blk, HY_W), lambda j, q, *_: (off + j, u_col), pipeline_mode=pl.Buffered(1)),
        pl.BlockSpec((blk, HY_W), lambda j, q, *_: (off + j, gate_col), pipeline_mode=pl.Buffered(1)),
        pl.BlockSpec((1, HY_W), lambda j, q, *_: (0, 0)),
        mat(), mat(),
        pl.BlockSpec((radix, None, 2, p, HY_W), lambda j, q, *_: (0, q, 0, 0, order)),
    ]
    kwargs = {}
    if prev_out is not None:
        args.append(prev_out)
        in_specs.append(pl.BlockSpec(memory_space=pl.ANY))
        kwargs["input_output_aliases"] = {len(args) - 1: 0}
    return pl.pallas_call(
        functools.partial(_conv_kernel, length=length, p=p, radix=radix),
        grid_spec=pltpu.PrefetchScalarGridSpec(
            num_scalar_prefetch=2,
            grid=(nb // 2, nq),
            in_specs=in_specs,
            out_specs=pl.BlockSpec((blk, HY_W), lambda j, q, *_: (off + j, 0)),
            scratch_shapes=[pltpu.VMEM((blk, HY_W), F32)],
        ),
        out_shape=jax.ShapeDtypeStruct((t_all, HY_W), BF16),
        compiler_params=_cparams("arbitrary", "arbitrary"),
        name="hyena_long_conv",
        **kwargs,
    )(*args)


def _attn_kernel(sink_ref, q_ref, *refs, local):
    if local:
        kp_ref, kc_ref, kn_ref, vp_ref, vc_ref, vn_ref, kx_ref, vx_ref, o_ref = refs
        n = pl.program_id(1)
        last = pl.num_programs(1) - 1
        ki = lax.broadcasted_iota(jnp.int32, (QBLK, QBLK), 0)
        qi = lax.broadcasted_iota(jnp.int32, (QBLK, QBLK), 1)
        ok_prev = (ki >= qi) & (n > 0)
        ok_next = (ki <= qi) & (n < last)
    else:
        kx_ref, vx_ref, o_ref = refs
    nt = (((1,), (1,)), ((), ()))
    nq = q_ref.shape[0]
    if local:
        bias1 = jnp.concatenate([jnp.where(ok_prev, 0.0, NEG_INF), jnp.zeros((QBLK, QBLK), F32),
                                 jnp.where(ok_next, 0.0, NEG_INF), jnp.zeros((kx_ref.shape[0], QBLK), F32)], axis=0)
        bias = jnp.concatenate([bias1] * Q_PER_KV, axis=1)
    for kv in range(N_KV_HEADS):
        cs = slice(kv * HEAD_DIM, (kv + 1) * HEAD_DIM)
        heads = range(kv * Q_PER_KV, (kv + 1) * Q_PER_KV)
        qs = jnp.concatenate([q_ref[:, h * HEAD_DIM:(h + 1) * HEAD_DIM] for h in heads], axis=0)
        sink = jnp.concatenate([jnp.full((1, nq), sink_ref[h], F32) for h in heads], axis=1)
        if local:
            keys = jnp.concatenate([kp_ref[:, cs], kc_ref[:, cs], kn_ref[:, cs], kx_ref[:, cs]], axis=0)
            vals = jnp.concatenate([vp_ref[:, cs], vc_ref[:, cs], vn_ref[:, cs], vx_ref[:, cs]], axis=0)
        else:
            keys, vals = kx_ref[:, cs], vx_ref[:, cs]
        s = lax.dot_general(keys, qs, nt, preferred_element_type=F32)
        if local:
            s = s + bias
        mx = jnp.maximum(jnp.max(s, axis=0, keepdims=True), sink)
        p = jnp.exp((s - mx).astype(BF16))
        vals1 = jnp.concatenate([vals, jnp.ones(vals.shape, BF16)], axis=1)
        pv = lax.dot_general(vals1, p, (((0,), (0,)), ((), ())), preferred_element_type=F32)
        den = pv[HEAD_DIM:HEAD_DIM + 1, :] + jnp.exp(sink - mx)
        out = pv[:HEAD_DIM, :] / den
        for g, h in enumerate(heads):
            o_ref[:, h * HEAD_DIM:(h + 1) * HEAD_DIM] = out[:, g * nq:(g + 1) * nq].T.astype(o_ref.dtype)


def _attention(q, k, v, sink, nb, n, c):
    t_all = q.shape[0]
    nqb = n // QBLK
    ctx_blk0 = nb * n // c
    kvspec = lambda fn: pl.BlockSpec((QBLK, KV_W), fn)
    prev = lambda b, j: (b * nqb + jnp.maximum(j - 1, 0), 0)
    cur = lambda b, j: (b * nqb + j, 0)
    nxt = lambda b, j: (b * nqb + jnp.minimum(j + 1, nqb - 1), 0)
    ctxs = pl.BlockSpec((c, KV_W), lambda b, j: (ctx_blk0 + b, 0))
    smem = pl.BlockSpec(memory_space=pltpu.SMEM)
    lat = pl.pallas_call(
        functools.partial(_attn_kernel, local=True),
        grid=(nb, nqb),
        in_specs=[smem, pl.BlockSpec((QBLK, ATT_W), cur),
                  kvspec(prev), kvspec(cur), kvspec(nxt), kvspec(prev), kvspec(cur), kvspec(nxt), ctxs, ctxs],
        out_specs=pl.BlockSpec((QBLK, ATT_W), cur),
        out_shape=jax.ShapeDtypeStruct((t_all, ATT_W), BF16),
        compiler_params=_cparams("arbitrary", "arbitrary"),
        name="banded_attention",
    )(sink, q, k, k, k, v, v, v, k, v)
    ctx1 = pl.BlockSpec((c, KV_W), lambda b: (ctx_blk0 + b, 0))
    return pl.pallas_call(
        lambda s, qq, kx, vx, prev_o, o: _attn_kernel(s, qq, kx, vx, o, local=False),
        grid=(nb,),
        in_specs=[smem, pl.BlockSpec((c, ATT_W), lambda b: (ctx_blk0 + b, 0)), ctx1, ctx1,
                  pl.BlockSpec(memory_space=pl.ANY)],
        out_specs=pl.BlockSpec((c, ATT_W), lambda b: (ctx_blk0 + b, 0)),
        out_shape=jax.ShapeDtypeStruct((t_all, ATT_W), BF16),
        input_output_aliases={4: 0},
        compiler_params=_cparams("arbitrary"),
        name="context_attention",
    )(sink, q, k, v, lat)


def _s5disc_kernel(lre_ref, lim_ref, dt_ref, bre_ref, bim_ref, are_ref, aim_ref, bbre_ref, bbim_ref):
    lam_re = lre_ref[...]
    lam_im = lim_ref[...]
    dt = jnp.exp(dt_ref[...])
    mag = jnp.exp(lam_re * dt)
    ab_re = mag * jnp.cos(lam_im * dt)
    ab_im = mag * jnp.sin(lam_im * dt)
    num_re = ab_re - 1.0
    num_im = ab_im
    den = lam_re * lam_re + lam_im * lam_im
    co_re = (num_re * lam_re + num_im * lam_im) / den
    co_im = (num_im * lam_re - num_re * lam_im) / den
    b_re = bre_ref[...]
    b_im = bim_ref[...]
    are_ref[...] = ab_re
    aim_ref[...] = ab_im
    bbre_ref[...] = co_re * b_re - co_im * b_im
    bbim_ref[...] = co_re * b_im + co_im * b_re


def _s5_discretize(lam_re, lam_im, log_dt, b_re, b_im):
    lead = lam_re.shape[:-1]
    rep = lambda a: jnp.repeat(a.reshape(-1, 1, S5_STATE), S5_CPG, axis=1).reshape(-1, S5_STATE)
    dt = jnp.broadcast_to(log_dt.reshape(-1, 1, 1), (math.prod(lead), S5_CPG, S5_STATE)).reshape(-1, S5_STATE)
    tr = lambda b: jnp.swapaxes(b, -1, -2).reshape(-1, S5_STATE)
    rows = math.prod(lead) * S5_CPG
    shp = jax.ShapeDtypeStruct((rows, S5_STATE), F32)
    ab_re, ab_im, bb_re, bb_im = pl.pallas_call(
        _s5disc_kernel, out_shape=[shp] * 4, name="s5_discretize",
    )(rep(lam_re), rep(lam_im), dt, tr(b_re), tr(b_im))
    full = lead + (S5_CPG, S5_STATE)
    return (ab_re.reshape(full)[..., 0, :], ab_im.reshape(full)[..., 0, :],
            bb_re.reshape(full), bb_im.reshape(full))


def _s5_kernel(*refs, tc, nbatch, n_ctx_steps):
    u_refs = (refs[:nbatch], refs[nbatch:2 * nbatch])
    bmat_ref, cmat_ref, a_ref = refs[2 * nbatch:2 * nbatch + 3]
    y_refs = (refs[2 * nbatch + 3:2 * nbatch + 5], refs[2 * nbatch + 5:2 * nbatch + 7])
    st_ref, bu_ref, tm_ref = refs[2 * nbatch + 7:]
    i = pl.program_id(0)
    gp = S5_GROUPS * S5_STATE
    halves = S5_W // 128

    @pl.when(i == 0)
    def _():
        st_ref[...] = jnp.zeros(st_ref.shape, F32)

    for d in range(2):
        for b in range(nbatch):
            for j in range(halves):
                tm_ref[j, pl.ds(b, tc, stride=nbatch), :] = u_refs[d][b][:, j * 128:(j + 1) * 128]
        u_tm = jnp.concatenate([tm_ref[j] for j in range(halves)], axis=1)
        bu_ref[...] = _dot(u_tm.astype(BF16), bmat_ref[d])
        ar = a_ref[d, 0]
        ai = a_ref[d, 1]

        def body(j, carry, d=d, ar=ar, ai=ai):
            hr, hi = carry
            t = j if d == 0 else tc - 1 - j
            r0 = pl.multiple_of(t * nbatch, nbatch)
            nr = ar * hr - ai * hi + bu_ref[pl.ds(r0, nbatch), 0:gp]
            ni = ar * hi + ai * hr + bu_ref[pl.ds(r0, nbatch), gp:2 * gp]
            bu_ref[pl.ds(r0, nbatch), 0:gp] = nr
            bu_ref[pl.ds(r0, nbatch), gp:2 * gp] = ni
            return nr, ni

        hr, hi = lax.fori_loop(0, tc, body, (st_ref[d, 0], st_ref[d, 1]))
        st_ref[d, 0] = hr
        st_ref[d, 1] = hi
        y = _dot(bu_ref[...].astype(BF16), cmat_ref[d])
        for j in range(halves):
            tm_ref[j] = y[:, j * 128:(j + 1) * 128]

        def emit(y_ref):
            for b in range(nbatch):
                for j in range(halves):
                    y_ref[b, :, j * 128:(j + 1) * 128] = tm_ref[j, pl.ds(b, tc, stride=nbatch), :]

        pl.when(i >= n_ctx_steps)(functools.partial(emit, y_refs[d][0]))
        pl.when(i < n_ctx_steps)(functools.partial(emit, y_refs[d][1]))


def _s5_scan(u, bmat, cmat, a_bc, nb, n, c):
    tc = S5_TC
    rows = tc * nb
    nl, nc = n // tc, c // tc
    gp = S5_GROUPS * S5_STATE
    ctx0 = nb * nl

    def chunk(step, b, backward):
        in_ctx = step < nc
        ctx_chunk = (nc - 1 - step) if backward else step
        lat_chunk = (nl - 1 - (step - nc)) if backward else (step - nc)
        return jnp.where(in_ctx, ctx0 + b * nc + ctx_chunk, b * nl + lat_chunk)

    specs = [pl.BlockSpec((tc, S5_W), lambda i, b=b, bw=bw: (chunk(i, b, bw), 0))
             for bw in (False, True) for b in range(nb)]
    full = lambda arr: pl.BlockSpec(arr.shape, lambda i: (0,) * arr.ndim)
    oblk = lambda fn: pl.BlockSpec((nb, tc, S5_W), lambda i: (0, fn(i), 0))
    out_specs = [oblk(lambda i: jnp.maximum(i - nc, 0)), oblk(lambda i: jnp.minimum(i, nc - 1)),
                 oblk(lambda i: jnp.minimum(nl + nc - 1 - i, nl - 1)), oblk(lambda i: jnp.maximum(nc - 1 - i, 0))]
    lat = jax.ShapeDtypeStruct((nb, n, S5_W), F32)
    ctx = jax.ShapeDtypeStruct((nb, c, S5_W), F32)
    return pl.pallas_call(
        functools.partial(_s5_kernel, tc=tc, nbatch=nb, n_ctx_steps=nc),
        grid=(nl + nc,),
        in_specs=specs + [full(bmat), full(cmat), full(a_bc)],
        out_specs=out_specs,
        out_shape=[lat, ctx, lat, ctx],
        scratch_shapes=[pltpu.VMEM((2, 2, nb, gp), F32), pltpu.VMEM((rows, 2 * gp), F32),
                        pltpu.VMEM((S5_W // 128, rows, 128), F32)],
        compiler_params=_cparams("arbitrary"),
        name="s5_scan",
    )(*([u] * (2 * nb)), bmat, cmat, a_bc)


def _s5_matrices(ab_re, ab_im, bb_re, bb_im, c_re, c_im, nb):
    eye = jnp.eye(S5_GROUPS, dtype=F32)
    bd_in = lambda b: jnp.einsum("dgcp,gh->dgchp", b, eye).reshape(2, S5_W, S5_GROUPS * S5_STATE)
    bd_out = lambda cc: jnp.einsum("dgcp,gh->dgphc", cc, eye).reshape(2, S5_GROUPS * S5_STATE, S5_W)
    bmat = jnp.concatenate([bd_in(bb_re), bd_in(bb_im)], axis=-1).astype(BF16)
    cmat = jnp.concatenate([bd_out(c_re), -bd_out(c_im)], axis=1).astype(BF16)
    a = jnp.stack([ab_re.reshape(2, -1), ab_im.reshape(2, -1)], axis=1)
    a_bc = jnp.broadcast_to(a[:, :, None, :], (2, 2, nb, S5_GROUPS * S5_STATE))
    return bmat, cmat, a_bc


def _out_kernel(x_ref, hy_ref, att_ref, yfl_ref, yfc_ref, ybl_ref, ybc_ref, u_ref, d_ref, gluw_ref, glub_ref,
                ng_ref, wout_ref, g1_ref, sh2_ref, sc2_ref, n2g_ref, rw_ref, rb_ref, tri_ref,
                xo_ref, h2_ref, eidx_ref, gate_ref, rank_ref, cnt_ref, *, n_lat_tiles):
    is_ctx = pl.program_id(0) >= n_lat_tiles
    y = jnp.where(is_ctx, yfc_ref[...] + ybc_ref[...], yfl_ref[...] + ybl_ref[...])
    y = y + u_ref[...] * d_ref[...]
    g = jax.nn.gelu(y)
    s5 = g * jax.nn.sigmoid(_dot(g.astype(BF16), gluw_ref[...]) + glub_ref[...])

    def nrm(part):
        return part * lax.rsqrt(jnp.mean(part * part, axis=-1, keepdims=True) + EPS)

    mix = jnp.concatenate([nrm(hy_ref[...].astype(F32)), nrm(att_ref[...].astype(F32)), nrm(s5)], axis=-1) * ng_ref[...]
    x = x_ref[...] + g1_ref[...] * _dot(mix.astype(BF16), wout_ref[...])
    xo_ref[...] = x
    h2 = _rms(x, n2g_ref[...]) * (1.0 + sc2_ref[...]) + sh2_ref[...]
    h2_hi = h2.astype(BF16)
    h2_ref[...] = h2_hi

    h2_lo = (h2 - h2_hi.astype(F32)).astype(BF16)
    logits = (_dot(h2_hi, rw_ref[0]) + _dot(h2_lo, rw_ref[0]) + _dot(h2_hi, rw_ref[1])).T
    e = jnp.exp(logits - jnp.max(logits, axis=0, keepdims=True))
    probs = e / jnp.sum(e, axis=0, keepdims=True)
    sel = probs + rb_ref[...]
    rows = [sel[r:r + 1, :] for r in range(N_EXPERTS)]
    best = None
    for grp in range(N_EXPERT_GROUPS):
        a = rows[grp * EXPERTS_PER_GROUP:(grp + 1) * EXPERTS_PER_GROUP]
        score = None
        for p0 in range(EXPERTS_PER_GROUP):
            for p1 in range(p0 + 1, EXPERTS_PER_GROUP):
                pair = a[p0] + a[p1]
                score = pair if score is None else jnp.maximum(score, pair)
        if best is None:
            best, gidx = score, jnp.zeros(score.shape, jnp.int32)
        else:
            better = score > best
            gidx = jnp.where(better, grp, gidx)
            best = jnp.where(better, score, best)
    ing = []
    for j in range(EXPERTS_PER_GROUP):
        v = rows[j]
        for grp in range(1, N_EXPERT_GROUPS):
            v = jnp.where(gidx == grp, rows[grp * EXPERTS_PER_GROUP + j], v)
        ing.append(v)
    first_v, first_i = ing[0], jnp.zeros(gidx.shape, jnp.int32)
    for j in range(1, EXPERTS_PER_GROUP):
        better = ing[j] > first_v
        first_i = jnp.where(better, j, first_i)
        first_v = jnp.where(better, ing[j], first_v)
    second_v, second_i = None, None
    for j in range(EXPERTS_PER_GROUP):
        cand = jnp.where(first_i == j, -jnp.inf, ing[j])
        if second_v is None:
            second_v, second_i = cand, jnp.zeros(gidx.shape, jnp.int32)
        else:
            better = cand > second_v
            second_i = jnp.where(better, j, second_i)
            second_v = jnp.where(better, cand, second_v)
    e0 = gidx * EXPERTS_PER_GROUP + first_i
    e1 = gidx * EXPERTS_PER_GROUP + second_i
    eid = lax.broadcasted_iota(jnp.int32, probs.shape, 0)
    oh0 = eid == e0
    oh1 = eid == e1
    p0v = jnp.sum(jnp.where(oh0, probs, 0.0), axis=0, keepdims=True)
    p1v = jnp.sum(jnp.where(oh1, probs, 0.0), axis=0, keepdims=True)
    tot = p0v + p1v
    member = jnp.where(oh0 | oh1, 1.0, 0.0)
    before = _dot(member.astype(BF16), tri_ref[...])
    r0 = jnp.sum(jnp.where(oh0, before, 0.0), axis=0, keepdims=True)
    r1 = jnp.sum(jnp.where(oh1, before, 0.0), axis=0, keepdims=True)
    cnt_ref[...] = jnp.sum(member, axis=1, keepdims=True)
    eidx_ref[...] = jnp.concatenate([e0, e1], axis=0)
    gate_ref[...] = jnp.concatenate([p0v / tot, p1v / tot], axis=0)
    rank_ref[...] = jnp.concatenate([r0, r1], axis=0).astype(jnp.int32)


def _mixer_out(x_all, hy, att, y_scan, u, s5_d, glu_w_bf, glu_b, mix_g, w_out_bf, mod4, n2g,
               router_hl, router_b, nb, n, layer):
    t_all = x_all.shape[0]
    n_lat_tiles = nb * n // TM
    tpb = n // TM
    assert y_scan[1].shape[1] == TM
    mrow = lambda i: jnp.where(i < n_lat_tiles, i // tpb, nb)
    row = lambda w: pl.BlockSpec((TM, w), lambda i: (i, 0))
    full = lambda arr: pl.BlockSpec(arr.shape, lambda i: (0,) * arr.ndim)
    layer_mat = lambda arr: pl.BlockSpec((None,) + arr.shape[1:], lambda i: (layer, 0, 0))
    modspec = lambda k: pl.BlockSpec((None, None, 1, D_MODEL), lambda i, k=k: (mrow(i), k, 0, 0))
    lat_y = pl.BlockSpec((None, TM, S5_W), lambda i: (jnp.minimum(i, n_lat_tiles - 1) // tpb,
                                                       jnp.minimum(i, n_lat_tiles - 1) % tpb, 0))
    ctx_y = pl.BlockSpec((None, TM, S5_W), lambda i: (jnp.maximum(i - n_lat_tiles, 0), 0, 0))
    tri = (lax.broadcasted_iota(jnp.int32, (TM, TM), 0) < lax.broadcasted_iota(jnp.int32, (TM, TM), 1)).astype(BF16)
    v1 = lambda a: a.reshape(1, -1)
    ins = [v1(s5_d), glu_w_bf, v1(glu_b), v1(mix_g), w_out_bf]
    in_w_specs = [full(ins[0]), layer_mat(glu_w_bf), full(ins[2]), full(ins[3]), layer_mat(w_out_bf)]
    tail = [v1(n2g), router_hl, router_b.reshape(N_EXPERTS, 1), tri]
    tok = lambda: pl.BlockSpec((2, TM), lambda i: (0, i))
    return pl.pallas_call(
        functools.partial(_out_kernel, n_lat_tiles=n_lat_tiles),
        grid=(t_all // TM,),
        in_specs=[row(D_MODEL), row(HY_W), row(ATT_W), lat_y, ctx_y, lat_y, ctx_y, row(S5_W)] + in_w_specs
                 + [modspec(2), modspec(3), modspec(4)] + [full(a) for a in tail],
        out_specs=[row(D_MODEL), row(D_MODEL), tok(), tok(), tok(),
                   pl.BlockSpec((None, N_EXPERTS, 1), lambda i: (i, 0, 0))],
        out_shape=[
            jax.ShapeDtypeStruct((t_all, D_MODEL), F32),
            jax.ShapeDtypeStruct((t_all, D_MODEL), BF16),
            jax.ShapeDtypeStruct((2, t_all), jnp.int32),
            jax.ShapeDtypeStruct((2, t_all), F32),
            jax.ShapeDtypeStruct((2, t_all), jnp.int32),
            jax.ShapeDtypeStruct((t_all // TM, N_EXPERTS, 1), F32),
        ],
        compiler_params=_cparams("arbitrary"),
        name="mixer_out_router",
    )(x_all, hy, att, *y_scan, u, *ins, mod4, mod4, mod4, *tail)


SORT_ROWS = 2 * TM + 8 * N_EXPERTS
CHUNK_BITS = 6
TOT_BITS = 7
TAIL_BITS = 6


def _for_chunks(n8, bits, fn):
    for b in range(bits - 1, -1, -1):
        done = ((n8 >> (b + 1)) << (b + 1)) * 8

        @pl.when(((n8 >> b) & 1) == 1)
        def _(done=done, rows=8 << b):
            fn(pl.multiple_of(done, 8), rows)


def _dispatch_kernel(loc_ref, dst_ref, n8_ref, tot_ref, tail_ref, h2_ref, eidx_ref, lrank_ref, adj_ref,
                     xb_ref, spos_ref, sorted_ref, zero_ref, sem, zsem):
    i = pl.program_id(0)
    n_tiles = pl.num_programs(0)
    slot = i % 2

    def wait_tile(tile, buf):
        def wait(off, rows):
            pltpu.make_async_copy(sorted_ref.at[buf, pl.ds(0, rows)], xb_ref.at[pl.ds(0, rows)], sem.at[buf]).wait()
        _for_chunks(tot_ref[tile], TOT_BITS, wait)

    @pl.when(i >= 2)
    def _():
        wait_tile(i - 2, slot)

    @pl.when(i == 0)
    def _():
        zero_ref[...] = jnp.zeros(zero_ref.shape, F32)
        for start_wait in (True, False):
            for e in range(N_EXPERTS):
                def tail(off, rows, e=e, start_wait=start_wait):
                    cp = pltpu.make_async_copy(zero_ref.at[pl.ds(0, rows)],
                                               xb_ref.at[pl.ds(pl.multiple_of(tail_ref[e] + off, 8), rows)], zsem)
                    cp.start() if start_wait else cp.wait()
                _for_chunks(tail_ref[N_EXPERTS + e], TAIL_BITS, tail)

    eid = lax.broadcasted_iota(jnp.int32, (N_EXPERTS, TM), 0)
    adj = adj_ref[...]
    spos = []
    for k in range(2):
        chunk_start = jnp.sum(jnp.where(eid == eidx_ref[k:k + 1, :], adj, 0.0), axis=0, keepdims=True)
        spos.append(lrank_ref[k:k + 1, :] + chunk_start.astype(jnp.int32))
    spos_ref[...] = jnp.concatenate(spos, axis=0)
    rows = lax.broadcasted_iota(jnp.int32, (SORT_ROWS, TM), 0)
    perm = jnp.where((rows == spos[0]) | (rows == spos[1]), 1.0, 0.0).astype(BF16)
    sorted_ref[slot] = _dot(perm, h2_ref[...])

    for e in range(N_EXPERTS):
        idx = i * N_EXPERTS + e

        def send(off, rows, idx=idx):
            pltpu.make_async_copy(sorted_ref.at[slot, pl.ds(pl.multiple_of(loc_ref[idx] + off, 8), rows)],
                                  xb_ref.at[pl.ds(pl.multiple_of(dst_ref[idx] + off, 8), rows)], sem.at[slot]).start()
        _for_chunks(n8_ref[idx], CHUNK_BITS, send)

    @pl.when(i == n_tiles - 1)
    def _():
        wait_tile(i - 1, 1 - slot)
        wait_tile(i, slot)


def _dispatch(tables, h2, eidx, lrank, adj, n_slots):
    t_all = h2.shape[0]
    tok = lambda: pl.BlockSpec((2, TM), lambda i, *_: (0, i))
    return pl.pallas_call(
        _dispatch_kernel,
        grid_spec=pltpu.PrefetchScalarGridSpec(
            num_scalar_prefetch=5,
            grid=(t_all // TM,),
            in_specs=[pl.BlockSpec((TM, D_MODEL), lambda i, *_: (i, 0)), tok(), tok(),
                      pl.BlockSpec((None, N_EXPERTS, 1), lambda i, *_: (i, 0, 0))],
            out_specs=[pl.BlockSpec(memory_space=pl.ANY), tok()],
            scratch_shapes=[pltpu.VMEM((2, SORT_ROWS, D_MODEL), F32), pltpu.VMEM((MOE_ROWS // 2, D_MODEL), F32),
                            pltpu.SemaphoreType.DMA((2,)), pltpu.SemaphoreType.DMA(())],
        ),
        out_shape=[jax.ShapeDtypeStruct((n_slots, D_MODEL), F32), jax.ShapeDtypeStruct((2, t_all), jnp.int32)],
        compiler_params=_cparams("arbitrary"),
        name="moe_dispatch",
    )(*tables, h2, eidx, lrank, adj)


def _ffn_kernel(be_ref, nu_ref, x_ref, wg_ref, wu_ref, wd_ref, o_ref, wbf_ref):
    j = pl.program_id(0)

    @pl.when(j < nu_ref[0])
    def _():
        @pl.when((j == 0) | (be_ref[j] != be_ref[jnp.maximum(j - 1, 0)]))
        def _():
            wbf_ref[0] = wg_ref[...].astype(BF16)
            wbf_ref[1] = wu_ref[...].astype(BF16)
            wbf_ref[2] = wd_ref[...].astype(BF16)

        x = x_ref[...].astype(BF16)
        a = _dot(x, wbf_ref[0])
        u = _dot(x, wbf_ref[1])
        hidden = (a * jax.nn.sigmoid(a)) * u
        o_ref[...] = _dot(hidden.astype(BF16), wbf_ref[2])


def _expert_ffn(block_e, n_used, xb, wg, wu, wd, layer):
    n_slots = xb.shape[0]
    blk = lambda j, be, nu: jnp.minimum(j, nu[0] - 1)
    wspec = lambda: pl.BlockSpec((None, None, D_MODEL, D_MODEL),
                                 lambda j, be, nu: (layer, be[blk(j, be, nu)], 0, 0))
    return pl.pallas_call(
        _ffn_kernel,
        grid_spec=pltpu.PrefetchScalarGridSpec(
            num_scalar_prefetch=2,
            grid=(n_slots // MOE_ROWS,),
            in_specs=[pl.BlockSpec((MOE_ROWS, D_MODEL), lambda j, be, nu: (blk(j, be, nu), 0)),
                      wspec(), wspec(), wspec()],
            out_specs=pl.BlockSpec((MOE_ROWS, D_MODEL), lambda j, be, nu: (blk(j, be, nu), 0)),
            scratch_shapes=[pltpu.VMEM((3, D_MODEL, D_MODEL), BF16)],
        ),
        out_shape=jax.ShapeDtypeStruct((n_slots, D_MODEL), F32),
        compiler_params=_cparams("arbitrary"),
        name="moe_expert_ffn",
    )(block_e, n_used, xb, wg, wu, wd)


def _combine_kernel(loc_ref, dst_ref, n8_ref, tot_ref, x_ref, spos_ref, gates_ref, g2_ref, yb_ref, *rest,
                    final):
    if final:
        fg_ref, xo_ref, ybuf_ref, sem = rest
    else:
        xo_ref, ybuf_ref, sem = rest
    i = pl.program_id(0)
    n_tiles = pl.num_programs(0)
    slot = i % 2

    def fetch_tile(tile, buf):
        for e in range(N_EXPERTS):
            idx = tile * N_EXPERTS + e

            def fetch(off, rows, idx=idx):
                pltpu.make_async_copy(yb_ref.at[pl.ds(pl.multiple_of(dst_ref[idx] + off, 8), rows)],
                                      ybuf_ref.at[buf, pl.ds(pl.multiple_of(loc_ref[idx] + off, 8), rows)],
                                      sem.at[buf]).start()
            _for_chunks(n8_ref[idx], CHUNK_BITS, fetch)

    @pl.when(i == 0)
    def _():
        ybuf_ref[...] = jnp.zeros(ybuf_ref.shape, F32)
        fetch_tile(0, 0)

    @pl.when(i + 1 < n_tiles)
    def _():
        fetch_tile(i + 1, 1 - slot)

    def wait(off, rows):
        pltpu.make_async_copy(yb_ref.at[pl.ds(0, rows)], ybuf_ref.at[slot, pl.ds(0, rows)], sem.at[slot]).wait()
    _for_chunks(tot_ref[i], TOT_BITS, wait)

    y_sorted = ybuf_ref[slot].astype(BF16)
    lane = lax.broadcasted_iota(jnp.int32, (TM, SORT_ROWS), 1)
    gates = gates_ref[...]
    y = jnp.zeros((TM, D_MODEL), F32)
    for k in range(2):
        pick = jnp.where(lane == spos_ref[:, k:k + 1], 1.0, 0.0).astype(BF16)
        y = y + gates[:, k:k + 1] * _dot(pick, y_sorted)
    x = x_ref[...] + g2_ref[...] * y
    xo_ref[...] = _rms(x, fg_ref[...]) if final else x


def _combine(tables, x_all, spos_t, gates_t, mod4, yb, nb, n, final_g=None):
    final = final_g is not None
    t_all = nb * n if final else x_all.shape[0]
    n_lat_tiles = nb * n // TM
    tpb = n // TM
    mrow = lambda i: jnp.where(i < n_lat_tiles, i // tpb, nb)
    extra_specs = [pl.BlockSpec((1, D_MODEL), lambda i, *_: (0, 0))] if final else []
    extra_args = [final_g.reshape(1, D_MODEL)] if final else []
    return pl.pallas_call(
        functools.partial(_combine_kernel, final=final),
        grid_spec=pltpu.PrefetchScalarGridSpec(
            num_scalar_prefetch=4,
            grid=(t_all // TM,),
            in_specs=[pl.BlockSpec((TM, D_MODEL), lambda i, *_: (i, 0)),
                      pl.BlockSpec((TM, 2), lambda i, *_: (i, 0)),
                      pl.BlockSpec((TM, 2), lambda i, *_: (i, 0)),
                      pl.BlockSpec((None, None, 1, D_MODEL), lambda i, *_: (mrow(i), 5, 0, 0)),
                      pl.BlockSpec(memory_space=pl.ANY)] + extra_specs,
            out_specs=pl.BlockSpec((TM, D_MODEL), lambda i, *_: (i, 0)),
            scratch_shapes=[pltpu.VMEM((2, SORT_ROWS, D_MODEL), F32), pltpu.SemaphoreType.DMA((2,))],
        ),
        out_shape=jax.ShapeDtypeStruct((t_all, D_MODEL), F32),
        compiler_params=_cparams("arbitrary"),
        name="moe_combine",
    )(*tables, x_all, spos_t, gates_t, mod4, yb, *extra_args)


def kernel(x, c, ctx, c_ctx, norm1_g, norm2_g, ada_w, ada_b, w_in, w_out, mix_norm_g, hy_conv_w, hy_conv_b, hy_f_w1, hy_f_b1, hy_f_freq1, hy_f_w2, hy_f_b2, hy_f_freq2, hy_f_w3, hy_bias, attn_sink, s5_lam_re, s5_lam_im, s5_log_dt, s5_b_re, s5_b_im, s5_c_re, s5_c_im, s5_d, s5_glu_w, s5_glu_b, router_w, router_b, moe_w_gate, moe_w_up, moe_w_down, final_g):
    nb, n, d = x.shape
    cl = ctx.shape[1]
    depth = w_in.shape[0]
    t_all = nb * (n + cl)

    cc = jnp.zeros((16, d), F32).at[:nb].set(c).at[nb].set(c_ctx)
    mod_all = _modulation(cc, ada_w, ada_b).reshape(depth, 16, 6, 1, d)
    cos, sin = _rope_tables(n)
    ab_re, ab_im, bb_re, bb_im = _s5_discretize(s5_lam_re, s5_lam_im, s5_log_dt, s5_b_re, s5_b_im)

    tabs = {}
    for length in (n, cl):
        twr, twi, er, ei = _dft_tables(length)
        tabs[length] = (twr, twi, er.astype(BF16), ei.astype(BF16))

    rw_hi = router_w.astype(BF16)
    router_hl = jnp.stack([rw_hi, (router_w - rw_hi.astype(F32)).astype(BF16)])
    w_in_bf, w_out_bf, glu_w_bf = w_in.astype(BF16), w_out.astype(BF16), s5_glu_w.astype(BF16)
    expert_ids = jnp.arange(N_EXPERTS, dtype=jnp.int32)
    n_tiles = t_all // TM
    tile_ids = jnp.arange(n_tiles, dtype=jnp.int32)
    n_slots = (-(-(2 * t_all + 7 * N_EXPERTS * n_tiles) // MOE_ROWS) + N_EXPERTS) * MOE_ROWS
    x_all = jnp.concatenate([x.reshape(nb * n, d), ctx.reshape(nb * cl, d)], axis=0)

    for l in range(depth):
        mod4 = mod_all[l]
        hy, q, k, v, u = _input_proj(x_all, norm1_g[l], mod4, w_in_bf, cos, sin, nb, n, l)

        zc = _short_conv(hy, hy_conv_w[l], hy_conv_b[l], nb, n, cl)
        hy_out = None
        z1 = None
        specs = {}
        for length in (n, cl):
            specs[length] = _filter_spectrum(length, tabs[length], hy_f_w1[l], hy_f_b1[l], hy_f_freq1[l],
                                             hy_f_w2[l], hy_f_b2[l], hy_f_freq2[l], hy_f_w3[l])
        for length, off in ((n, 0), (cl, nb * n)):
            z1 = _long_conv(zc, 0, zc, 1, hy_bias[l, 0], specs[length], 0, tabs[length], length, off, nb, z1)
        for length, off in ((n, 0), (cl, nb * n)):
            hy_out = _long_conv(z1, 0, zc, 2, hy_bias[l, 1], specs[length], 1, tabs[length], length, off, nb,
                                hy_out)

        att = _attention(q, k, v, attn_sink[l], nb, n, cl)

        bmat, cmat, a_bc = _s5_matrices(ab_re[l], ab_im[l], bb_re[l], bb_im[l], s5_c_re[l], s5_c_im[l], nb)
        y_scan = _s5_scan(u, bmat, cmat, a_bc, nb, n, cl)

        x_all, h2, eidx, gates, lrank, counts = _mixer_out(
            x_all, hy_out, att, y_scan, u, s5_d[l], glu_w_bf, s5_glu_b[l], mix_norm_g[l],
            w_out_bf, mod4, norm2_g[l], router_hl, router_b, nb, n, l)

        cnt = counts.reshape(n_tiles, N_EXPERTS).astype(jnp.int32)
        c8 = (cnt + 7) // 8 * 8
        loc = jnp.sum(jnp.where(expert_ids[None, :] < expert_ids[:, None], c8[:, None, :], 0), axis=2)
        seg_off = jnp.sum(jnp.where((tile_ids[None, :] < tile_ids[:, None])[:, :, None], c8[None, :, :], 0), axis=1)
        tot8 = jnp.sum(c8, axis=0)
        padded = (tot8 + MOE_ROWS - 1) // MOE_ROWS * MOE_ROWS
        ends = jnp.sum(jnp.where(expert_ids[None, :] <= expert_ids[:, None], padded[None, :], 0), axis=1)
        pstart = ends - padded
        dst = pstart[None, :] + seg_off
        tables = (loc.reshape(-1), dst.reshape(-1), (c8 // 8).reshape(-1), jnp.sum(c8, axis=1) // 8)
        tail = jnp.concatenate([pstart + tot8, (padded - tot8) // 8])
        n_blocks = n_slots // MOE_ROWS
        block_start = jnp.arange(n_blocks, dtype=jnp.int32) * MOE_ROWS
        block_e = jnp.minimum(jnp.sum((ends[None, :] <= block_start[:, None]).astype(jnp.int32), axis=1),
                              N_EXPERTS - 1)
        n_used = (ends[-1] // MOE_ROWS).reshape(1)

        xb, spos = _dispatch(tables + (tail,), h2, eidx, lrank, loc.astype(F32).reshape(n_tiles, N_EXPERTS, 1),
                             n_slots)
        yb = _expert_ffn(block_e, n_used, xb, moe_w_gate, moe_w_up, moe_w_down, l)
        x_all = _combine(tables, x_all, spos.T, gates.T, mod4, yb, nb, n, final_g if l == depth - 1 else None)

    return x_all.reshape(nb, n, d)
```

```python
import functools
import math

import jax
import jax.numpy as jnp
from jax import lax
from jax.experimental import pallas as pl
from jax.experimental.pallas import tpu as pltpu

F32 = jnp.float32
BF16 = jnp.bfloat16
HIGHEST = lax.Precision.HIGHEST

D_MODEL = 1024
GRID_W = 64
HY_W = 256
ATT_W = 512
S5_W = 256
HEAD_DIM = 64
N_HEADS = 8
N_KV_HEADS = 2
Q_PER_KV = 4
KV_W = 128
WINDOW = 128
QBLK = 128
ROPE_BASE = 10000.0
NEG_INF = -1e30
HY_BANDS = 16
HY_DECAY_MIN = math.log(1e-2) / 1.5
HY_DECAY_MAX = math.log(1e-2) / 0.3
S5_CPG = 16
S5_GROUPS = 16
S5_STATE = 64
N_EXPERTS = 16
N_EXPERT_GROUPS = 4
EXPERTS_PER_GROUP = 4
HY_END = 768
Q_END = 1280
K_END = 1408
V_END = 1536
IN_W = 1792
EPS = 1e-6

TM = 256
MOE_ROWS = 512
S5_TC = 128
DFT_P = 512
VMEM_LIMIT = 56 * 1024 * 1024


def _cparams(*sem):
    return pltpu.CompilerParams(dimension_semantics=sem, vmem_limit_bytes=VMEM_LIMIT)


def _dot(a, b):
    return jnp.dot(a, b, preferred_element_type=F32)


def _mod_kernel(c_ref, w_ref, b_ref, o_ref):
    c = c_ref[...]
    a = c * jax.nn.sigmoid(c)
    o_ref[...] = _dot(a.astype(BF16), w_ref[...].astype(BF16)) + b_ref[...]


def _modulation(cc, ada_w, ada_b):
    depth, d, w6 = ada_w.shape
    nb = 1536
    return pl.pallas_call(
        _mod_kernel,
        grid=(depth, w6 // nb),
        in_specs=[
            pl.BlockSpec((16, d), lambda l, j: (0, 0)),
            pl.BlockSpec((None, d, nb), lambda l, j: (l, 0, j)),
            pl.BlockSpec((None, 1, nb), lambda l, j: (l, 0, j)),
        ],
        out_specs=pl.BlockSpec((None, 16, nb), lambda l, j: (l, 0, j)),
        out_shape=jax.ShapeDtypeStruct((depth, 16, w6), F32),
        compiler_params=_cparams("arbitrary", "arbitrary"),
        name="adaln_mod",
    )(cc, ada_w, ada_b.reshape(depth, 1, w6))


def _rms(x, g):
    return x * lax.rsqrt(jnp.mean(x * x, axis=-1, keepdims=True) + EPS) * g


def _in_kernel(x_ref, g_ref, sh_ref, sc_ref, w_ref, cos_ref, sin_ref,
               hy_ref, q_ref, k_ref, v_ref, u_ref):
    h = _rms(x_ref[...], g_ref[...]) * (1.0 + sc_ref[...]) + sh_ref[...]
    p = _dot(h.astype(BF16), w_ref[...])
    hy_ref[...] = p[:, :HY_END].astype(hy_ref.dtype)
    cos = cos_ref[...]
    sin = sin_ref[...]
    lane = lax.broadcasted_iota(jnp.int32, cos.shape, 1)
    first = (lane % 32) < 16

    def rope(z):
        swapped = jnp.where(first, pltpu.roll(z, 112, 1), pltpu.roll(z, 16, 1))
        return z * cos + swapped * sin

    for j in range(ATT_W // 128):
        qj = rope(p[:, HY_END + j * 128:HY_END + (j + 1) * 128]) * (HEAD_DIM ** -0.5)
        q_ref[:, j * 128:(j + 1) * 128] = qj.astype(BF16)
    k_ref[...] = rope(p[:, Q_END:K_END]).astype(BF16)
    v_ref[...] = p[:, K_END:V_END].astype(BF16)
    u_ref[...] = p[:, V_END:]


def _rope_tables(n):
    quarter = HEAD_DIM // 4
    inv_freq = ROPE_BASE ** (-jnp.arange(quarter, dtype=F32) / quarter)
    t = jnp.arange(n)
    rows = (t // GRID_W).astype(F32)
    cols = (t % GRID_W).astype(F32)
    ang_r = rows[:, None] * inv_freq[None, :]
    ang_c = cols[:, None] * inv_freq[None, :]
    cos64 = jnp.concatenate([jnp.cos(ang_r)] * 2 + [jnp.cos(ang_c)] * 2, axis=-1)
    sin64 = jnp.concatenate([-jnp.sin(ang_r), jnp.sin(ang_r), -jnp.sin(ang_c), jnp.sin(ang_c)], axis=-1)
    cos = jnp.concatenate([jnp.tile(cos64, (1, 2)), jnp.ones((TM, 128), F32)], axis=0)
    sin = jnp.concatenate([jnp.tile(sin64, (1, 2)), jnp.zeros((TM, 128), F32)], axis=0)
    return cos, sin


def _input_proj(x_all, g, mod4, w_in_bf, cos, sin, nb, n, layer):
    t_all = x_all.shape[0]
    n_lat_tiles = nb * n // TM
    tpb = n // TM

    def mrow(i):
        return jnp.where(i < n_lat_tiles, i // tpb, nb)

    def trow(i):
        return jnp.where(i < n_lat_tiles, i % tpb, tpb)

    row = lambda w: pl.BlockSpec((TM, w), lambda i: (i, 0))
    return pl.pallas_call(
        _in_kernel,
        grid=(t_all // TM,),
        in_specs=[
            row(D_MODEL),
            pl.BlockSpec((1, D_MODEL), lambda i: (0, 0)),
            pl.BlockSpec((None, None, 1, D_MODEL), lambda i: (mrow(i), 0, 0, 0)),
            pl.BlockSpec((None, None, 1, D_MODEL), lambda i: (mrow(i), 1, 0, 0)),
            pl.BlockSpec((None, D_MODEL, IN_W), lambda i: (layer, 0, 0)),
            pl.BlockSpec((TM, 128), lambda i: (trow(i), 0)),
            pl.BlockSpec((TM, 128), lambda i: (trow(i), 0)),
        ],
        out_specs=[row(HY_END), row(ATT_W), row(KV_W), row(KV_W), row(S5_W)],
        out_shape=[
            jax.ShapeDtypeStruct((t_all, HY_END), BF16),
            jax.ShapeDtypeStruct((t_all, ATT_W), BF16),
            jax.ShapeDtypeStruct((t_all, KV_W), BF16),
            jax.ShapeDtypeStruct((t_all, KV_W), BF16),
            jax.ShapeDtypeStruct((t_all, S5_W), F32),
        ],
        compiler_params=_cparams("arbitrary"),
        name="norm_mod_inproj",
    )(x_all, g.reshape(1, D_MODEL), mod4, mod4, w_in_bf, cos, sin)


def _sconv_kernel(z_ref, w_ref, b_ref, o_ref):
    z = z_ref[...].astype(F32)
    length = z.shape[0]
    row = lax.broadcasted_iota(jnp.int32, z.shape, 0)
    zm = jnp.where(row == 0, 0.0, pltpu.roll(z, 1, 0))
    zp = jnp.where(row == length - 1, 0.0, pltpu.roll(z, length - 1, 0))
    o_ref[...] = (b_ref[...] + zm * w_ref[0:1, :] + z * w_ref[1:2, :] + zp * w_ref[2:3, :]).astype(o_ref.dtype)


def _short_conv(hy, conv_w, conv_b, nb, n, c):
    t_all = hy.shape[0]
    out = None
    for length, off in ((n, 0), (c, nb * n // c)):
        kwargs = {}
        args = [hy, conv_w, conv_b.reshape(1, HY_END)]
        in_specs = [
            pl.BlockSpec((length, 256), lambda b, j, off=off: (off + b, j)),
            pl.BlockSpec((3, 256), lambda b, j: (0, j)),
            pl.BlockSpec((1, 256), lambda b, j: (0, j)),
        ]
        kern = _sconv_kernel
        if out is not None:
            args.append(out)
            in_specs.append(pl.BlockSpec(memory_space=pl.ANY))
            kwargs["input_output_aliases"] = {3: 0}
            kern = lambda z, w, b, prev, o: _sconv_kernel(z, w, b, o)
        out = pl.pallas_call(
            kern,
            grid=(nb, 3),
            in_specs=in_specs,
            out_specs=pl.BlockSpec((length, 256), lambda b, j, off=off: (off + b, j)),
            out_shape=jax.ShapeDtypeStruct((t_all, HY_END), BF16),
            compiler_params=_cparams("arbitrary", "arbitrary"),
            name="hyena_short_conv",
            **kwargs,
        )(*args)
    return out


def _dft_plan(length):
    p = min(DFT_P, length)
    return p, 2 * length // p


def _dft_tables(length):
    p, na = _dft_plan(length)
    m = 2 * length
    a = jnp.arange(na, dtype=jnp.int32)
    ang_w = (2.0 * math.pi / na) * ((a[:, None] * a[None, :]) % na).astype(F32)
    twr, twi = jnp.cos(ang_w), -jnp.sin(ang_w)
    r = jnp.arange(p, dtype=jnp.int32)
    k = a[:, None, None] + na * r[None, :, None]
    ang = (2.0 * math.pi / m) * ((k * r[None, None, :]) % m).astype(F32)
    er, ei = jnp.cos(ang), -jnp.sin(ang)
    return twr, twi, er, ei


def _filter_kernel(twr_ref, twi_ref, w1_ref, b1_ref, f1_ref, w2_ref, b2_ref, f2_ref, w3_ref,
                   bands_ref, decay_ref, er_ref, ei_ref, h_ref, kern_ref, norm_ref, *, length, p, na):
    ka = pl.program_id(0)
    m = 2 * length
    hdot = functools.partial(jnp.dot, precision=HIGHEST, preferred_element_type=F32)

    @pl.when(ka == 0)
    def _():
        norm = jnp.zeros((1, 2 * HY_W), F32)
        for a in range(na):
            n_idx = a * p + lax.broadcasted_iota(jnp.int32, (p, 1), 0)
            is_fwd = n_idx < length
            pos = jnp.where(is_fwd, n_idx, m - n_idx).astype(F32)
            t = pos / float(max(length - 1, 1))
            ang = (2.0 * math.pi / length) * pos * bands_ref[...]
            pre = (t * w1_ref[0:1, :] + hdot(jnp.cos(ang), w1_ref[1:1 + HY_BANDS, :])
                   + hdot(-jnp.sin(ang), w1_ref[1 + HY_BANDS:, :]) + b1_ref[...])
            hid = jnp.sin(f1_ref[...] * pre)
            hid = jnp.sin(f2_ref[...] * (hdot(hid, w2_ref[...]) + b2_ref[...]))
            taps = hdot(hid, w3_ref[...])
            wnd = jnp.exp(-t * decay_ref[...])
            live = n_idx != length
            for o in range(2):
                fwd = taps[:, o * 2 * HY_W:o * 2 * HY_W + HY_W]
                bwd = taps[:, o * 2 * HY_W + HY_W:(o + 1) * 2 * HY_W]
                kern = jnp.where(live, jnp.where(is_fwd, fwd, bwd) * wnd, 0.0)
                kern_ref[a, :, o * HY_W:(o + 1) * HY_W] = kern
            norm = norm + jnp.sum(jnp.abs(kern_ref[a]), axis=0, keepdims=True)
        norm_ref[...] = norm

    gr = jnp.zeros((p, 2 * HY_W), F32)
    gi = jnp.zeros((p, 2 * HY_W), F32)
    for a in range(na):
        slab = kern_ref[a]
        gr = gr + twr_ref[ka, a] * slab
        gi = gi + twi_ref[ka, a] * slab
    er = er_ref[...]
    ei = ei_ref[...]
    grb = gr.astype(BF16)
    gib = gi.astype(BF16)
    inv = 1.0 / norm_ref[...]
    h_ref[0] = ((_dot(er, grb) - _dot(ei, gib)) * inv).astype(h_ref.dtype)
    h_ref[1] = ((_dot(ei, grb) + _dot(er, gib)) * inv).astype(h_ref.dtype)


def _filter_spectrum(length, tables, w1, b1, f1, w2, b2, f2, w3):
    p, na = _dft_plan(length)
    twr, twi, er, ei = tables
    bands = jnp.linspace(1e-4, HY_BANDS - 1, HY_BANDS, dtype=F32).reshape(1, HY_BANDS)
    decay = jnp.abs(jnp.linspace(HY_DECAY_MIN, HY_DECAY_MAX, HY_W, dtype=F32)).reshape(1, HY_W)
    full = lambda arr: pl.BlockSpec(arr.shape, lambda ka, *_: (0,) * arr.ndim)
    vec = lambda v: v.reshape(1, -1)
    ins = [w1, vec(b1), vec(f1), w2, vec(b2), vec(f2), w3, bands, decay]
    return pl.pallas_call(
        functools.partial(_filter_kernel, length=length, p=p, na=na),
        grid_spec=pltpu.PrefetchScalarGridSpec(
            num_scalar_prefetch=2,
            grid=(na,),
            in_specs=[full(x) for x in ins] + [
                pl.BlockSpec((None, p, p), lambda ka, *_: (ka, 0, 0)),
                pl.BlockSpec((None, p, p), lambda ka, *_: (ka, 0, 0)),
            ],
            out_specs=pl.BlockSpec((None, 2, p, 2 * HY_W), lambda ka, *_: (ka, 0, 0, 0)),
            scratch_shapes=[pltpu.VMEM((na, p, 2 * HY_W), F32), pltpu.VMEM((1, 2 * HY_W), F32)],
        ),
        out_shape=jax.ShapeDtypeStruct((na, 2, p, 2 * HY_W), BF16),
        compiler_params=_cparams("arbitrary"),
        name="hyena_filter_spectrum",
    )(twr, twi, *[vec(x) if x.ndim == 1 else x for x in ins], er, ei)


def _cadd(x, y):
    return x[0] + y[0], x[1] + y[1]


def _csub(x, y):
    return x[0] - y[0], x[1] - y[1]


def _radix_mix(parts, sign):
    if len(parts) == 1:
        return parts
    p0, p1, p2, p3 = parts
    t0, t1, t2, t3 = _cadd(p0, p2), _csub(p0, p2), _cadd(p1, p3), _csub(p1, p3)
    it3 = (-t3[1], t3[0])
    if sign < 0:
        return [_cadd(t0, t2), _csub(t1, it3), _csub(t0, t2), _cadd(t1, it3)]
    return [_cadd(t0, t2), _cadd(t1, it3), _csub(t0, t2), _csub(t1, it3)]


def _conv_kernel(twr_ref, twi_ref, u_ref, gate_ref, bias_ref, er_ref, ei_ref, h_ref, *rest, length, p, radix):
    o_ref, acc_ref = rest[-2], rest[-1]
    q = pl.program_id(1)
    nz = length // p
    tdot = lambda x, y: lax.dot_general(x, y, (((0,), (0,)), ((), ())), preferred_element_type=F32)

    parts = [None] * radix
    for a in range(nz):
        wr = twr_ref[q, a]
        wi = twi_ref[q, a]
        zr = u_ref[a * p:(a + 1) * p, :].astype(F32)
        zi = u_ref[length + a * p:length + (a + 1) * p, :].astype(F32)
        term = (wr * zr - wi * zi, wr * zi + wi * zr)
        parts[a % radix] = term if parts[a % radix] is None else _cadd(parts[a % radix], term)
    zero = jnp.zeros((p, o_ref.shape[-1]), F32)
    parts = [(zero, zero) if t is None else t for t in parts]
    g = _radix_mix(parts, -1)

    v = []
    for m in range(radix):
        grb = g[m][0].astype(BF16)
        gib = g[m][1].astype(BF16)
        er = er_ref[m]
        ei = ei_ref[m]
        sr = _dot(er, grb) - _dot(ei, gib)
        si = _dot(ei, grb) + _dot(er, gib)
        hr = h_ref[m, 0].astype(F32)
        hi = h_ref[m, 1].astype(F32)
        yr = (sr * hr - si * hi).astype(BF16)
        yi = (sr * hi + si * hr).astype(BF16)
        v.append((tdot(er, yr) + tdot(ei, yi), tdot(er, yi) - tdot(ei, yr)))
    qs = _radix_mix(v, +1)

    @pl.when(q == 0)
    def _():
        acc_ref[...] = jnp.zeros(acc_ref.shape, F32)

    scale = 1.0 / (2 * length)
    for a in range(nz):
        wr = twr_ref[q, a] * scale
        wi = twi_ref[q, a] * scale
        vr, vi = qs[a % radix]
        acc_ref[a * p:(a + 1) * p, :] += wr * vr + wi * vi
        acc_ref[length + a * p:length + (a + 1) * p, :] += wr * vi - wi * vr

    @pl.when(q == pl.num_programs(1) - 1)
    def _():
        u = u_ref[...].astype(F32)
        o_ref[...] = (gate_ref[...].astype(F32) * (acc_ref[...] + u * bias_ref[...])).astype(o_ref.dtype)


def _long_conv(u_arr, u_col, gate_arr, gate_col, bias, spec, order, tables_bf, length, row_off, nb, prev_out):
    p, na = _dft_plan(length)
    radix = 4 if na % 4 == 0 and na >= 8 else 1
    nq = na // radix
    twr, twi, er, ei = tables_bf
    t_all = u_arr.shape[0]
    blk = 2 * length
    off = row_off // blk
    mat = lambda: pl.BlockSpec((radix, None, p, p), lambda j, q, *_: (0, q, 0, 0))
    args = [twr, twi, u_arr, gate_arr, bias.reshape(1, HY_W), er.reshape(radix, nq, p, p),
            ei.reshape(radix, nq, p, p), spec.reshape(radix, nq, 2, p, 2 * HY_W)]
    in_specs = [
        pl.BlockSpec((blk, HY_W), lambda j, q, *_: (off + j, u_col), pipeline_mode=pl.Buffered(1)),
        pl.BlockSpec((blk, HY_W), lambda j, q, *_: (off + j, gate_col), pipeline_mode=pl.Buffered(1)),
        pl.BlockSpec((1, HY_W), lambda j, q, *_: (0, 0)),
        mat(), mat(),
        pl.BlockSpec((radix, None, 2, p, HY_W), lambda j, q, *_: (0, q, 0, 0, order)),
    ]
    kwargs = {}
    if prev_out is not None:
        args.append(prev_out)
        in_specs.append(pl.BlockSpec(memory_space=pl.ANY))
        kwargs["input_output_aliases"] = {len(args) - 1: 0}
    return pl.pallas_call(
        functools.partial(_conv_kernel, length=length, p=p, radix=radix),
        grid_spec=pltpu.PrefetchScalarGridSpec(
            num_scalar_prefetch=2,
            grid=(nb // 2, nq),
            in_specs=in_specs,
            out_specs=pl.BlockSpec((blk, HY_W), lambda j, q, *_: (off + j, 0)),
            scratch_shapes=[pltpu.VMEM((blk, HY_W), F32)],
        ),
        out_shape=jax.ShapeDtypeStruct((t_all, HY_W), BF16),
        compiler_params=_cparams("arbitrary", "arbitrary"),
        name="hyena_long_conv",
        **kwargs,
    )(*args)


def _attn_kernel(sink_ref, q_ref, *refs, local):
    if local:
        kp_ref, kc_ref, kn_ref, vp_ref, vc_ref, vn_ref, kx_ref, vx_ref, o_ref = refs
        n = pl.program_id(1)
        last = pl.num_programs(1) - 1
        ki = lax.broadcasted_iota(jnp.int32, (QBLK, QBLK), 0)
        qi = lax.broadcasted_iota(jnp.int32, (QBLK, QBLK), 1)
        ok_prev = (ki >= qi) & (n > 0)
        ok_next = (ki <= qi) & (n < last)
    else:
        kx_ref, vx_ref, o_ref = refs
    nt = (((1,), (1,)), ((), ()))
    nq = q_ref.shape[0]
    if local:
        bias1 = jnp.concatenate([jnp.where(ok_prev, 0.0, NEG_INF), jnp.zeros((QBLK, QBLK), F32),
                                 jnp.where(ok_next, 0.0, NEG_INF), jnp.zeros((kx_ref.shape[0], QBLK), F32)], axis=0)
        bias = jnp.concatenate([bias1] * Q_PER_KV, axis=1)
    for kv in range(N_KV_HEADS):
        cs = slice(kv * HEAD_DIM, (kv + 1) * HEAD_DIM)
        heads = range(kv * Q_PER_KV, (kv + 1) * Q_PER_KV)
        qs = jnp.concatenate([q_ref[:, h * HEAD_DIM:(h + 1) * HEAD_DIM] for h in heads], axis=0)
        sink = jnp.concatenate([jnp.full((1, nq), sink_ref[h], F32) for h in heads], axis=1)
        if local:
            keys = jnp.concatenate([kp_ref[:, cs], kc_ref[:, cs], kn_ref[:, cs], kx_ref[:, cs]], axis=0)
            vals = jnp.concatenate([vp_ref[:, cs], vc_ref[:, cs], vn_ref[:, cs], vx_ref[:, cs]], axis=0)
        else:
            keys, vals = kx_ref[:, cs], vx_ref[:, cs]
        s = lax.dot_general(keys, qs, nt, preferred_element_type=F32)
        if local:
            s = s + bias
        mx = jnp.maximum(jnp.max(s, axis=0, keepdims=True), sink)
        p = jnp.exp((s - mx).astype(BF16))
        vals1 = jnp.concatenate([vals, jnp.ones(vals.shape, BF16)], axis=1)
        pv = lax.dot_general(vals1, p, (((0,), (0,)), ((), ())), preferred_element_type=F32)
        den = pv[HEAD_DIM:HEAD_DIM + 1, :] + jnp.exp(sink - mx)
        out = pv[:HEAD_DIM, :] / den
        for g, h in enumerate(heads):
            o_ref[:, h * HEAD_DIM:(h + 1) * HEAD_DIM] = out[:, g * nq:(g + 1) * nq].T.astype(o_ref.dtype)


def _attention(q, k, v, sink, nb, n, c):
    t_all = q.shape[0]
    nqb = n // QBLK
    ctx_blk0 = nb * n // c
    kvspec = lambda fn: pl.BlockSpec((QBLK, KV_W), fn)
    prev = lambda b, j: (b * nqb + jnp.maximum(j - 1, 0), 0)
    cur = lambda b, j: (b * nqb + j, 0)
    nxt = lambda b, j: (b * nqb + jnp.minimum(j + 1, nqb - 1), 0)
    ctxs = pl.BlockSpec((c, KV_W), lambda b, j: (ctx_blk0 + b, 0))
    smem = pl.BlockSpec(memory_space=pltpu.SMEM)
    lat = pl.pallas_call(
        functools.partial(_attn_kernel, local=True),
        grid=(nb, nqb),
        in_specs=[smem, pl.BlockSpec((QBLK, ATT_W), cur),
                  kvspec(prev), kvspec(cur), kvspec(nxt), kvspec(prev), kvspec(cur), kvspec(nxt), ctxs, ctxs],
        out_specs=pl.BlockSpec((QBLK, ATT_W), cur),
        out_shape=jax.ShapeDtypeStruct((t_all, ATT_W), BF16),
        compiler_params=_cparams("arbitrary", "arbitrary"),
        name="banded_attention",
    )(sink, q, k, k, k, v, v, v, k, v)
    ctx1 = pl.BlockSpec((c, KV_W), lambda b: (ctx_blk0 + b, 0))
    return pl.pallas_call(
        lambda s, qq, kx, vx, prev_o, o: _attn_kernel(s, qq, kx, vx, o, local=False),
        grid=(nb,),
        in_specs=[smem, pl.BlockSpec((c, ATT_W), lambda b: (ctx_blk0 + b, 0)), ctx1, ctx1,
                  pl.BlockSpec(memory_space=pl.ANY)],
        out_specs=pl.BlockSpec((c, ATT_W), lambda b: (ctx_blk0 + b, 0)),
        out_shape=jax.ShapeDtypeStruct((t_all, ATT_W), BF16),
        input_output_aliases={4: 0},
        compiler_params=_cparams("arbitrary"),
        name="context_attention",
    )(sink, q, k, v, lat)


def _s5disc_kernel(lre_ref, lim_ref, dt_ref, bre_ref, bim_ref, are_ref, aim_ref, bbre_ref, bbim_ref):
    lam_re = lre_ref[...]
    lam_im = lim_ref[...]
    dt = jnp.exp(dt_ref[...])
    mag = jnp.exp(lam_re * dt)
    ab_re = mag * jnp.cos(lam_im * dt)
    ab_im = mag * jnp.sin(lam_im * dt)
    num_re = ab_re - 1.0
    num_im = ab_im
    den = lam_re * lam_re + lam_im * lam_im
    co_re = (num_re * lam_re + num_im * lam_im) / den
    co_im = (num_im * lam_re - num_re * lam_im) / den
    b_re = bre_ref[...]
    b_im = bim_ref[...]
    are_ref[...] = ab_re
    aim_ref[...] = ab_im
    bbre_ref[...] = co_re * b_re - co_im * b_im
    bbim_ref[...] = co_re * b_im + co_im * b_re


def _s5_discretize(lam_re, lam_im, log_dt, b_re, b_im):
    lead = lam_re.shape[:-1]
    rep = lambda a: jnp.repeat(a.reshape(-1, 1, S5_STATE), S5_CPG, axis=1).reshape(-1, S5_STATE)
    dt = jnp.broadcast_to(log_dt.reshape(-1, 1, 1), (math.prod(lead), S5_CPG, S5_STATE)).reshape(-1, S5_STATE)
    tr = lambda b: jnp.swapaxes(b, -1, -2).reshape(-1, S5_STATE)
    rows = math.prod(lead) * S5_CPG
    shp = jax.ShapeDtypeStruct((rows, S5_STATE), F32)
    ab_re, ab_im, bb_re, bb_im = pl.pallas_call(
        _s5disc_kernel, out_shape=[shp] * 4, name="s5_discretize",
    )(rep(lam_re), rep(lam_im), dt, tr(b_re), tr(b_im))
    full = lead + (S5_CPG, S5_STATE)
    return (ab_re.reshape(full)[..., 0, :], ab_im.reshape(full)[..., 0, :],
            bb_re.reshape(full), bb_im.reshape(full))


def _s5_kernel(*refs, tc, nbatch, n_ctx_steps):
    u_refs = (refs[:nbatch], refs[nbatch:2 * nbatch])
    bmat_ref, cmat_ref, a_ref = refs[2 * nbatch:2 * nbatch + 3]
    y_refs = (refs[2 * nbatch + 3:2 * nbatch + 5], refs[2 * nbatch + 5:2 * nbatch + 7])
    st_ref, bu_ref, tm_ref = refs[2 * nbatch + 7:]
    i = pl.program_id(0)
    gp = S5_GROUPS * S5_STATE
    halves = S5_W // 128

    @pl.when(i == 0)
    def _():
        st_ref[...] = jnp.zeros(st_ref.shape, F32)

    for d in range(2):
        for b in range(nbatch):
            for j in range(halves):
                tm_ref[j, pl.ds(b, tc, stride=nbatch), :] = u_refs[d][b][:, j * 128:(j + 1) * 128]
        u_tm = jnp.concatenate([tm_ref[j] for j in range(halves)], axis=1)
        bu_ref[...] = _dot(u_tm.astype(BF16), bmat_ref[d])
        ar = a_ref[d, 0]
        ai = a_ref[d, 1]

        def body(j, carry, d=d, ar=ar, ai=ai):
            hr, hi = carry
            t = j if d == 0 else tc - 1 - j
            r0 = pl.multiple_of(t * nbatch, nbatch)
            nr = ar * hr - ai * hi + bu_ref[pl.ds(r0, nbatch), 0:gp]
            ni = ar * hi + ai * hr + bu_ref[pl.ds(r0, nbatch), gp:2 * gp]
            bu_ref[pl.ds(r0, nbatch), 0:gp] = nr
            bu_ref[pl.ds(r0, nbatch), gp:2 * gp] = ni
            return nr, ni

        hr, hi = lax.fori_loop(0, tc, body, (st_ref[d, 0], st_ref[d, 1]))
        st_ref[d, 0] = hr
        st_ref[d, 1] = hi
        y = _dot(bu_ref[...].astype(BF16), cmat_ref[d])
        for j in range(halves):
            tm_ref[j] = y[:, j * 128:(j + 1) * 128]

        def emit(y_ref):
            for b in range(nbatch):
                for j in range(halves):
                    y_ref[b, :, j * 128:(j + 1) * 128] = tm_ref[j, pl.ds(b, tc, stride=nbatch), :]

        pl.when(i >= n_ctx_steps)(functools.partial(emit, y_refs[d][0]))
        pl.when(i < n_ctx_steps)(functools.partial(emit, y_refs[d][1]))


def _s5_scan(u, bmat, cmat, a_bc, nb, n, c):
    tc = S5_TC
    rows = tc * nb
    nl, nc = n // tc, c // tc
    gp = S5_GROUPS * S5_STATE
    ctx0 = nb * nl

    def chunk(step, b, backward):
        in_ctx = step < nc
        ctx_chunk = (nc - 1 - step) if backward else step
        lat_chunk = (nl - 1 - (step - nc)) if backward else (step - nc)
        return jnp.where(in_ctx, ctx0 + b * nc + ctx_chunk, b * nl + lat_chunk)

    specs = [pl.BlockSpec((tc, S5_W), lambda i, b=b, bw=bw: (chunk(i, b, bw), 0))
             for bw in (False, True) for b in range(nb)]
    full = lambda arr: pl.BlockSpec(arr.shape, lambda i: (0,) * arr.ndim)
    oblk = lambda fn: pl.BlockSpec((nb, tc, S5_W), lambda i: (0, fn(i), 0))
    out_specs = [oblk(lambda i: jnp.maximum(i - nc, 0)), oblk(lambda i: jnp.minimum(i, nc - 1)),
                 oblk(lambda i: jnp.minimum(nl + nc - 1 - i, nl - 1)), oblk(lambda i: jnp.maximum(nc - 1 - i, 0))]
    lat = jax.ShapeDtypeStruct((nb, n, S5_W), F32)
    ctx = jax.ShapeDtypeStruct((nb, c, S5_W), F32)
    return pl.pallas_call(
        functools.partial(_s5_kernel, tc=tc, nbatch=nb, n_ctx_steps=nc),
        grid=(nl + nc,),
        in_specs=specs + [full(bmat), full(cmat), full(a_bc)],
        out_specs=out_specs,
        out_shape=[lat, ctx, lat, ctx],
        scratch_shapes=[pltpu.VMEM((2, 2, nb, gp), F32), pltpu.VMEM((rows, 2 * gp), F32),
                        pltpu.VMEM((S5_W // 128, rows, 128), F32)],
        compiler_params=_cparams("arbitrary"),
        name="s5_scan",
    )(*([u] * (2 * nb)), bmat, cmat, a_bc)


def _s5_matrices(ab_re, ab_im, bb_re, bb_im, c_re, c_im, nb):
    eye = jnp.eye(S5_GROUPS, dtype=F32)
    bd_in = lambda b: jnp.einsum("dgcp,gh->dgchp", b, eye).reshape(2, S5_W, S5_GROUPS * S5_STATE)
    bd_out = lambda cc: jnp.einsum("dgcp,gh->dgphc", cc, eye).reshape(2, S5_GROUPS * S5_STATE, S5_W)
    bmat = jnp.concatenate([bd_in(bb_re), bd_in(bb_im)], axis=-1).astype(BF16)
    cmat = jnp.concatenate([bd_out(c_re), -bd_out(c_im)], axis=1).astype(BF16)
    a = jnp.stack([ab_re.reshape(2, -1), ab_im.reshape(2, -1)], axis=1)
    a_bc = jnp.broadcast_to(a[:, :, None, :], (2, 2, nb, S5_GROUPS * S5_STATE))
    return bmat, cmat, a_bc


def _out_kernel(x_ref, hy_ref, att_ref, yfl_ref, yfc_ref, ybl_ref, ybc_ref, u_ref, d_ref, gluw_ref, glub_ref,
                ng_ref, wout_ref, g1_ref, sh2_ref, sc2_ref, n2g_ref, rw_ref, rb_ref, tri_ref,
                xo_ref, h2_ref, eidx_ref, gate_ref, rank_ref, cnt_ref, *, n_lat_tiles):
    is_ctx = pl.program_id(0) >= n_lat_tiles
    y = jnp.where(is_ctx, yfc_ref[...] + ybc_ref[...], yfl_ref[...] + ybl_ref[...])
    y = y + u_ref[...] * d_ref[...]
    g = jax.nn.gelu(y)
    s5 = g * jax.nn.sigmoid(_dot(g.astype(BF16), gluw_ref[...]) + glub_ref[...])

    def nrm(part):
        return part * lax.rsqrt(jnp.mean(part * part, axis=-1, keepdims=True) + EPS)

    mix = jnp.concatenate([nrm(hy_ref[...].astype(F32)), nrm(att_ref[...].astype(F32)), nrm(s5)], axis=-1) * ng_ref[...]
    x = x_ref[...] + g1_ref[...] * _dot(mix.astype(BF16), wout_ref[...])
    xo_ref[...] = x
    h2 = _rms(x, n2g_ref[...]) * (1.0 + sc2_ref[...]) + sh2_ref[...]
    h2_hi = h2.astype(BF16)
    h2_ref[...] = h2_hi

    h2_lo = (h2 - h2_hi.astype(F32)).astype(BF16)
    logits = (_dot(h2_hi, rw_ref[0]) + _dot(h2_lo, rw_ref[0]) + _dot(h2_hi, rw_ref[1])).T
    e = jnp.exp(logits - jnp.max(logits, axis=0, keepdims=True))
    probs = e / jnp.sum(e, axis=0, keepdims=True)
    sel = probs + rb_ref[...]
    rows = [sel[r:r + 1, :] for r in range(N_EXPERTS)]
    best = None
    for grp in range(N_EXPERT_GROUPS):
        a = rows[grp * EXPERTS_PER_GROUP:(grp + 1) * EXPERTS_PER_GROUP]
        score = None
        for p0 in range(EXPERTS_PER_GROUP):
            for p1 in range(p0 + 1, EXPERTS_PER_GROUP):
                pair = a[p0] + a[p1]
                score = pair if score is None else jnp.maximum(score, pair)
        if best is None:
            best, gidx = score, jnp.zeros(score.shape, jnp.int32)
        else:
            better = score > best
            gidx = jnp.where(better, grp, gidx)
            best = jnp.where(better, score, best)
    ing = []
    for j in range(EXPERTS_PER_GROUP):
        v = rows[j]
        for grp in range(1, N_EXPERT_GROUPS):
            v = jnp.where(gidx == grp, rows[grp * EXPERTS_PER_GROUP + j], v)
        ing.append(v)
    first_v, first_i = ing[0], jnp.zeros(gidx.shape, jnp.int32)
    for j in range(1, EXPERTS_PER_GROUP):
        better = ing[j] > first_v
        first_i = jnp.where(better, j, first_i)
        first_v = jnp.where(better, ing[j], first_v)
    second_v, second_i = None, None
    for j in range(EXPERTS_PER_GROUP):
        cand = jnp.where(first_i == j, -jnp.inf, ing[j])
        if second_v is None:
            second_v, second_i = cand, jnp.zeros(gidx.shape, jnp.int32)
        else:
            better = cand > second_v
            second_i = jnp.where(better, j, second_i)
            second_v = jnp.where(better, cand, second_v)
    e0 = gidx * EXPERTS_PER_GROUP + first_i
    e1 = gidx * EXPERTS_PER_GROUP + second_i
    eid = lax.broadcasted_iota(jnp.int32, probs.shape, 0)
    oh0 = eid == e0
    oh1 = eid == e1
    p0v = jnp.sum(jnp.where(oh0, probs, 0.0), axis=0, keepdims=True)
    p1v = jnp.sum(jnp.where(oh1, probs, 0.0), axis=0, keepdims=True)
    tot = p0v + p1v
    member = jnp.where(oh0 | oh1, 1.0, 0.0)
    before = _dot(member.astype(BF16), tri_ref[...])
    r0 = jnp.sum(jnp.where(oh0, before, 0.0), axis=0, keepdims=True)
    r1 = jnp.sum(jnp.where(oh1, before, 0.0), axis=0, keepdims=True)
    cnt_ref[...] = jnp.sum(member, axis=1, keepdims=True)
    eidx_ref[...] = jnp.concatenate([e0, e1], axis=0)
    gate_ref[...] = jnp.concatenate([p0v / tot, p1v / tot], axis=0)
    rank_ref[...] = jnp.concatenate([r0, r1], axis=0).astype(jnp.int32)


def _mixer_out(x_all, hy, att, y_scan, u, s5_d, glu_w_bf, glu_b, mix_g, w_out_bf, mod4, n2g,
               router_hl, router_b, nb, n, layer):
    t_all = x_all.shape[0]
    n_lat_tiles = nb * n // TM
    tpb = n // TM
    assert y_scan[1].shape[1] == TM
    mrow = lambda i: jnp.where(i < n_lat_tiles, i // tpb, nb)
    row = lambda w: pl.BlockSpec((TM, w), lambda i: (i, 0))
    full = lambda arr: pl.BlockSpec(arr.shape, lambda i: (0,) * arr.ndim)
    layer_mat = lambda arr: pl.BlockSpec((None,) + arr.shape[1:], lambda i: (layer, 0, 0))
    modspec = lambda k: pl.BlockSpec((None, None, 1, D_MODEL), lambda i, k=k: (mrow(i), k, 0, 0))
    lat_y = pl.BlockSpec((None, TM, S5_W), lambda i: (jnp.minimum(i, n_lat_tiles - 1) // tpb,
                                                       jnp.minimum(i, n_lat_tiles - 1) % tpb, 0))
    ctx_y = pl.BlockSpec((None, TM, S5_W), lambda i: (jnp.maximum(i - n_lat_tiles, 0), 0, 0))
    tri = (lax.broadcasted_iota(jnp.int32, (TM, TM), 0) < lax.broadcasted_iota(jnp.int32, (TM, TM), 1)).astype(BF16)
    v1 = lambda a: a.reshape(1, -1)
    ins = [v1(s5_d), glu_w_bf, v1(glu_b), v1(mix_g), w_out_bf]
    in_w_specs = [full(ins[0]), layer_mat(glu_w_bf), full(ins[2]), full(ins[3]), layer_mat(w_out_bf)]
    tail = [v1(n2g), router_hl, router_b.reshape(N_EXPERTS, 1), tri]
    tok = lambda: pl.BlockSpec((2, TM), lambda i: (0, i))
    return pl.pallas_call(
        functools.partial(_out_kernel, n_lat_tiles=n_lat_tiles),
        grid=(t_all // TM,),
        in_specs=[row(D_MODEL), row(HY_W), row(ATT_W), lat_y, ctx_y, lat_y, ctx_y, row(S5_W)] + in_w_specs
                 + [modspec(2), modspec(3), modspec(4)] + [full(a) for a in tail],
        out_specs=[row(D_MODEL), row(D_MODEL), tok(), tok(), tok(),
                   pl.BlockSpec((None, N_EXPERTS, 1), lambda i: (i, 0, 0))],
        out_shape=[
            jax.ShapeDtypeStruct((t_all, D_MODEL), F32),
            jax.ShapeDtypeStruct((t_all, D_MODEL), BF16),
            jax.ShapeDtypeStruct((2, t_all), jnp.int32),
            jax.ShapeDtypeStruct((2, t_all), F32),
            jax.ShapeDtypeStruct((2, t_all), jnp.int32),
            jax.ShapeDtypeStruct((t_all // TM, N_EXPERTS, 1), F32),
        ],
        compiler_params=_cparams("arbitrary"),
        name="mixer_out_router",
    )(x_all, hy, att, *y_scan, u, *ins, mod4, mod4, mod4, *tail)


SORT_ROWS = 2 * TM + 8 * N_EXPERTS
CHUNK_BITS = 6
TOT_BITS = 7
TAIL_BITS = 6


def _for_chunks(n8, bits, fn):
    for b in range(bits - 1, -1, -1):
        done = ((n8 >> (b + 1)) << (b + 1)) * 8

        @pl.when(((n8 >> b) & 1) == 1)
        def _(done=done, rows=8 << b):
            fn(pl.multiple_of(done, 8), rows)


def _dispatch_kernel(loc_ref, dst_ref, n8_ref, tot_ref, tail_ref, h2_ref, eidx_ref, lrank_ref, adj_ref,
                     xb_ref, spos_ref, sorted_ref, zero_ref, sem, zsem):
    i = pl.program_id(0)
    n_tiles = pl.num_programs(0)
    slot = i % 2

    def wait_tile(tile, buf):
        def wait(off, rows):
            pltpu.make_async_copy(sorted_ref.at[buf, pl.ds(0, rows)], xb_ref.at[pl.ds(0, rows)], sem.at[buf]).wait()
        _for_chunks(tot_ref[tile], TOT_BITS, wait)

    @pl.when(i >= 2)
    def _():
        wait_tile(i - 2, slot)

    @pl.when(i == 0)
    def _():
        zero_ref[...] = jnp.zeros(zero_ref.shape, F32)
        for start_wait in (True, False):
            for e in range(N_EXPERTS):
                def tail(off, rows, e=e, start_wait=start_wait):
                    cp = pltpu.make_async_copy(zero_ref.at[pl.ds(0, rows)],
                                               xb_ref.at[pl.ds(pl.multiple_of(tail_ref[e] + off, 8), rows)], zsem)
                    cp.start() if start_wait else cp.wait()
                _for_chunks(tail_ref[N_EXPERTS + e], TAIL_BITS, tail)

    eid = lax.broadcasted_iota(jnp.int32, (N_EXPERTS, TM), 0)
    adj = adj_ref[...]
    spos = []
    for k in range(2):
        chunk_start = jnp.sum(jnp.where(eid == eidx_ref[k:k + 1, :], adj, 0.0), axis=0, keepdims=True)
        spos.append(lrank_ref[k:k + 1, :] + chunk_start.astype(jnp.int32))
    spos_ref[...] = jnp.concatenate(spos, axis=0)
    rows = lax.broadcasted_iota(jnp.int32, (SORT_ROWS, TM), 0)
    perm = jnp.where((rows == spos[0]) | (rows == spos[1]), 1.0, 0.0).astype(BF16)
    sorted_ref[slot] = _dot(perm, h2_ref[...])

    for e in range(N_EXPERTS):
        idx = i * N_EXPERTS + e

        def send(off, rows, idx=idx):
            pltpu.make_async_copy(sorted_ref.at[slot, pl.ds(pl.multiple_of(loc_ref[idx] + off, 8), rows)],
                                  xb_ref.at[pl.ds(pl.multiple_of(dst_ref[idx] + off, 8), rows)], sem.at[slot]).start()
        _for_chunks(n8_ref[idx], CHUNK_BITS, send)

    @pl.when(i == n_tiles - 1)
    def _():
        wait_tile(i - 1, 1 - slot)
        wait_tile(i, slot)


def _dispatch(tables, h2, eidx, lrank, adj, n_slots):
    t_all = h2.shape[0]
    tok = lambda: pl.BlockSpec((2, TM), lambda i, *_: (0, i))
    return pl.pallas_call(
        _dispatch_kernel,
        grid_spec=pltpu.PrefetchScalarGridSpec(
            num_scalar_prefetch=5,
            grid=(t_all // TM,),
            in_specs=[pl.BlockSpec((TM, D_MODEL), lambda i, *_: (i, 0)), tok(), tok(),
                      pl.BlockSpec((None, N_EXPERTS, 1), lambda i, *_: (i, 0, 0))],
            out_specs=[pl.BlockSpec(memory_space=pl.ANY), tok()],
            scratch_shapes=[pltpu.VMEM((2, SORT_ROWS, D_MODEL), F32), pltpu.VMEM((MOE_ROWS // 2, D_MODEL), F32),
                            pltpu.SemaphoreType.DMA((2,)), pltpu.SemaphoreType.DMA(())],
        ),
        out_shape=[jax.ShapeDtypeStruct((n_slots, D_MODEL), F32), jax.ShapeDtypeStruct((2, t_all), jnp.int32)],
        compiler_params=_cparams("arbitrary"),
        name="moe_dispatch",
    )(*tables, h2, eidx, lrank, adj)


def _ffn_kernel(be_ref, nu_ref, x_ref, wg_ref, wu_ref, wd_ref, o_ref, wbf_ref):
    j = pl.program_id(0)

    @pl.when(j < nu_ref[0])
    def _():
        @pl.when((j == 0) | (be_ref[j] != be_ref[jnp.maximum(j - 1, 0)]))
        def _():
            wbf_ref[0] = wg_ref[...].astype(BF16)
            wbf_ref[1] = wu_ref[...].astype(BF16)
            wbf_ref[2] = wd_ref[...].astype(BF16)

        x = x_ref[...].astype(BF16)
        a = _dot(x, wbf_ref[0])
        u = _dot(x, wbf_ref[1])
        hidden = (a * jax.nn.sigmoid(a)) * u
        o_ref[...] = _dot(hidden.astype(BF16), wbf_ref[2])


def _expert_ffn(block_e, n_used, xb, wg, wu, wd, layer):
    n_slots = xb.shape[0]
    blk = lambda j, be, nu: jnp.minimum(j, nu[0] - 1)
    wspec = lambda: pl.BlockSpec((None, None, D_MODEL, D_MODEL),
                                 lambda j, be, nu: (layer, be[blk(j, be, nu)], 0, 0))
    return pl.pallas_call(
        _ffn_kernel,
        grid_spec=pltpu.PrefetchScalarGridSpec(
            num_scalar_prefetch=2,
            grid=(n_slots // MOE_ROWS,),
            in_specs=[pl.BlockSpec((MOE_ROWS, D_MODEL), lambda j, be, nu: (blk(j, be, nu), 0)),
                      wspec(), wspec(), wspec()],
            out_specs=pl.BlockSpec((MOE_ROWS, D_MODEL), lambda j, be, nu: (blk(j, be, nu), 0)),
            scratch_shapes=[pltpu.VMEM((3, D_MODEL, D_MODEL), BF16)],
        ),
        out_shape=jax.ShapeDtypeStruct((n_slots, D_MODEL), F32),
        compiler_params=_cparams("arbitrary"),
        name="moe_expert_ffn",
    )(block_e, n_used, xb, wg, wu, wd)


def _combine_kernel(loc_ref, dst_ref, n8_ref, tot_ref, x_ref, spos_ref, gates_ref, g2_ref, yb_ref, *rest,
                    final):
    if final:
        fg_ref, xo_ref, ybuf_ref, sem = rest
    else:
        xo_ref, ybuf_ref, sem = rest
    i = pl.program_id(0)
    n_tiles = pl.num_programs(0)
    slot = i % 2

    def fetch_tile(tile, buf):
        for e in range(N_EXPERTS):
            idx = tile * N_EXPERTS + e

            def fetch(off, rows, idx=idx):
                pltpu.make_async_copy(yb_ref.at[pl.ds(pl.multiple_of(dst_ref[idx] + off, 8), rows)],
                                      ybuf_ref.at[buf, pl.ds(pl.multiple_of(loc_ref[idx] + off, 8), rows)],
                                      sem.at[buf]).start()
            _for_chunks(n8_ref[idx], CHUNK_BITS, fetch)

    @pl.when(i == 0)
    def _():
        ybuf_ref[...] = jnp.zeros(ybuf_ref.shape, F32)
        fetch_tile(0, 0)

    @pl.when(i + 1 < n_tiles)
    def _():
        fetch_tile(i + 1, 1 - slot)

    def wait(off, rows):
        pltpu.make_async_copy(yb_ref.at[pl.ds(0, rows)], ybuf_ref.at[slot, pl.ds(0, rows)], sem.at[slot]).wait()
    _for_chunks(tot_ref[i], TOT_BITS, wait)

    y_sorted = ybuf_ref[slot].astype(BF16)
    lane = lax.broadcasted_iota(jnp.int32, (TM, SORT_ROWS), 1)
    gates = gates_ref[...]
    y = jnp.zeros((TM, D_MODEL), F32)
    for k in range(2):
        pick = jnp.where(lane == spos_ref[:, k:k + 1], 1.0, 0.0).astype(BF16)
        y = y + gates[:, k:k + 1] * _dot(pick, y_sorted)
    x = x_ref[...] + g2_ref[...] * y
    xo_ref[...] = _rms(x, fg_ref[...]) if final else x


def _combine(tables, x_all, spos_t, gates_t, mod4, yb, nb, n, final_g=None):
    final = final_g is not None
    t_all = nb * n if final else x_all.shape[0]
    n_lat_tiles = nb * n // TM
    tpb = n // TM
    mrow = lambda i: jnp.where(i < n_lat_tiles, i // tpb, nb)
    extra_specs = [pl.BlockSpec((1, D_MODEL), lambda i, *_: (0, 0))] if final else []
    extra_args = [final_g.reshape(1, D_MODEL)] if final else []
    return pl.pallas_call(
        functools.partial(_combine_kernel, final=final),
        grid_spec=pltpu.PrefetchScalarGridSpec(
            num_scalar_prefetch=4,
            grid=(t_all // TM,),
            in_specs=[pl.BlockSpec((TM, D_MODEL), lambda i, *_: (i, 0)),
                      pl.BlockSpec((TM, 2), lambda i, *_: (i, 0)),
                      pl.BlockSpec((TM, 2), lambda i, *_: (i, 0)),
                      pl.BlockSpec((None, None, 1, D_MODEL), lambda i, *_: (mrow(i), 5, 0, 0)),
                      pl.BlockSpec(memory_space=pl.ANY)] + extra_specs,
            out_specs=pl.BlockSpec((TM, D_MODEL), lambda i, *_: (i, 0)),
            scratch_shapes=[pltpu.VMEM((2, SORT_ROWS, D_MODEL), F32), pltpu.SemaphoreType.DMA((2,))],
        ),
        out_shape=jax.ShapeDtypeStruct((t_all, D_MODEL), F32),
        compiler_params=_cparams("arbitrary"),
        name="moe_combine",
    )(*tables, x_all, spos_t, gates_t, mod4, yb, *extra_args)


def kernel(x, c, ctx, c_ctx, norm1_g, norm2_g, ada_w, ada_b, w_in, w_out, mix_norm_g, hy_conv_w, hy_conv_b, hy_f_w1, hy_f_b1, hy_f_freq1, hy_f_w2, hy_f_b2, hy_f_freq2, hy_f_w3, hy_bias, attn_sink, s5_lam_re, s5_lam_im, s5_log_dt, s5_b_re, s5_b_im, s5_c_re, s5_c_im, s5_d, s5_glu_w, s5_glu_b, router_w, router_b, moe_w_gate, moe_w_up, moe_w_down, final_g):
    nb, n, d = x.shape
    cl = ctx.shape[1]
    depth = w_in.shape[0]
    t_all = nb * (n + cl)

    cc = jnp.zeros((16, d), F32).at[:nb].set(c).at[nb].set(c_ctx)
    mod_all = _modulation(cc, ada_w, ada_b).reshape(depth, 16, 6, 1, d)
    cos, sin = _rope_tables(n)
    ab_re, ab_im, bb_re, bb_im = _s5_discretize(s5_lam_re, s5_lam_im, s5_log_dt, s5_b_re, s5_b_im)

    tabs = {}
    for length in (n, cl):
        twr, twi, er, ei = _dft_tables(length)
        tabs[length] = (twr, twi, er.astype(BF16), ei.astype(BF16))

    rw_hi = router_w.astype(BF16)
    router_hl = jnp.stack([rw_hi, (router_w - rw_hi.astype(F32)).astype(BF16)])
    w_in_bf, w_out_bf, glu_w_bf = w_in.astype(BF16), w_out.astype(BF16), s5_glu_w.astype(BF16)
    expert_ids = jnp.arange(N_EXPERTS, dtype=jnp.int32)
    n_tiles = t_all // TM
    tile_ids = jnp.arange(n_tiles, dtype=jnp.int32)
    n_slots = (-(-(2 * t_all + 7 * N_EXPERTS * n_tiles) // MOE_ROWS) + N_EXPERTS) * MOE_ROWS
    x_all = jnp.concatenate([x.reshape(nb * n, d), ctx.reshape(nb * cl, d)], axis=0)

    for l in range(depth):
        mod4 = mod_all[l]
        hy, q, k, v, u = _input_proj(x_all, norm1_g[l], mod4, w_in_bf, cos, sin, nb, n, l)

        zc = _short_conv(hy, hy_conv_w[l], hy_conv_b[l], nb, n, cl)
        hy_out = None
        z1 = None
        specs = {}
        for length in (n, cl):
            specs[length] = _filter_spectrum(length, tabs[length], hy_f_w1[l], hy_f_b1[l], hy_f_freq1[l],
                                             hy_f_w2[l], hy_f_b2[l], hy_f_freq2[l], hy_f_w3[l])
        for length, off in ((n, 0), (cl, nb * n)):
            z1 = _long_conv(zc, 0, zc, 1, hy_bias[l, 0], specs[length], 0, tabs[length], length, off, nb, z1)
        for length, off in ((n, 0), (cl, nb * n)):
            hy_out = _long_conv(z1, 0, zc, 2, hy_bias[l, 1], specs[length], 1, tabs[length], length, off, nb,
                                hy_out)

        att = _attention(q, k, v, attn_sink[l], nb, n, cl)

        bmat, cmat, a_bc = _s5_matrices(ab_re[l], ab_im[l], bb_re[l], bb_im[l], s5_c_re[l], s5_c_im[l], nb)
        y_scan = _s5_scan(u, bmat, cmat, a_bc, nb, n, cl)

        x_all, h2, eidx, gates, lrank, counts = _mixer_out(
            x_all, hy_out, att, y_scan, u, s5_d[l], glu_w_bf, s5_glu_b[l], mix_norm_g[l],
            w_out_bf, mod4, norm2_g[l], router_hl, router_b, nb, n, l)

        cnt = counts.reshape(n_tiles, N_EXPERTS).astype(jnp.int32)
        c8 = (cnt + 7) // 8 * 8
        loc = jnp.sum(jnp.where(expert_ids[None, :] < expert_ids[:, None], c8[:, None, :], 0), axis=2)
        seg_off = jnp.sum(jnp.where((tile_ids[None, :] < tile_ids[:, None])[:, :, None], c8[None, :, :], 0), axis=1)
        tot8 = jnp.sum(c8, axis=0)
        padded = (tot8 + MOE_ROWS - 1) // MOE_ROWS * MOE_ROWS
        ends = jnp.sum(jnp.where(expert_ids[None, :] <= expert_ids[:, None], padded[None, :], 0), axis=1)
        pstart = ends - padded
        dst = pstart[None, :] + seg_off
        tables = (loc.reshape(-1), dst.reshape(-1), (c8 // 8).reshape(-1), jnp.sum(c8, axis=1) // 8)
        tail = jnp.concatenate([pstart + tot8, (padded - tot8) // 8])
        n_blocks = n_slots // MOE_ROWS
        block_start = jnp.arange(n_blocks, dtype=jnp.int32) * MOE_ROWS
        block_e = jnp.minimum(jnp.sum((ends[None, :] <= block_start[:, None]).astype(jnp.int32), axis=1),
                              N_EXPERTS - 1)
        n_used = (ends[-1] // MOE_ROWS).reshape(1)

        xb, spos = _dispatch(tables + (tail,), h2, eidx, lrank, loc.astype(F32).reshape(n_tiles, N_EXPERTS, 1),
                             n_slots)
        yb = _expert_ffn(block_e, n_used, xb, moe_w_gate, moe_w_up, moe_w_down, l)
        x_all = _combine(tables, x_all, spos.T, gates.T, mod4, yb, nb, n, final_g if l == depth - 1 else None)

    return x_all.reshape(nb, n, d)
```

```python
import functools
import math

import jax
import jax.numpy as jnp
from jax import lax
from jax.experimental import pallas as pl
from jax.experimental.pallas import tpu as pltpu

F32 = jnp.float32
BF16 = jnp.bfloat16
HIGHEST = lax.Precision.HIGHEST

D_MODEL = 1024
GRID_W = 64
HY_W = 256
ATT_W = 512
S5_W = 256
HEAD_DIM = 64
N_HEADS = 8
N_KV_HEADS = 2
Q_PER_KV = 4
KV_W = 128
WINDOW = 128
QBLK = 128
ROPE_BASE = 10000.0
NEG_INF = -1e30
HY_BANDS = 16
HY_DECAY_MIN = math.log(1e-2) / 1.5
HY_DECAY_MAX = math.log(1e-2) / 0.3
S5_CPG = 16
S5_GROUPS = 16
S5_STATE = 64
N_EXPERTS = 16
N_EXPERT_GROUPS = 4
EXPERTS_PER_GROUP = 4
HY_END = 768
Q_END = 1280
K_END = 1408
V_END = 1536
IN_W = 1792
EPS = 1e-6

TM = 256
MOE_ROWS = 512
S5_TC = 128
DFT_P = 512
VMEM_LIMIT = 56 * 1024 * 1024


def _cparams(*sem):
    return pltpu.CompilerParams(dimension_semantics=sem, vmem_limit_bytes=VMEM_LIMIT)


def _dot(a, b):
    return jnp.dot(a, b, preferred_element_type=F32)


def _mod_kernel(c_ref, w_ref, b_ref, o_ref):
    c = c_ref[...]
    a = c * jax.nn.sigmoid(c)
    o_ref[...] = _dot(a.astype(BF16), w_ref[...].astype(BF16)) + b_ref[...]


def _modulation(cc, ada_w, ada_b):
    depth, d, w6 = ada_w.shape
    nb = 1536
    return pl.pallas_call(
        _mod_kernel,
        grid=(depth, w6 // nb),
        in_specs=[
            pl.BlockSpec((16, d), lambda l, j: (0, 0)),
            pl.BlockSpec((None, d, nb), lambda l, j: (l, 0, j)),
            pl.BlockSpec((None, 1, nb), lambda l, j: (l, 0, j)),
        ],
        out_specs=pl.BlockSpec((None, 16, nb), lambda l, j: (l, 0, j)),
        out_shape=jax.ShapeDtypeStruct((depth, 16, w6), F32),
        compiler_params=_cparams("arbitrary", "arbitrary"),
        name="adaln_mod",
    )(cc, ada_w, ada_b.reshape(depth, 1, w6))


def _rms(x, g):
    return x * lax.rsqrt(jnp.mean(x * x, axis=-1, keepdims=True) + EPS) * g


def _in_kernel(x_ref, g_ref, sh_ref, sc_ref, w_ref, cos_ref, sin_ref,
               hy_ref, q_ref, k_ref, v_ref, u_ref):
    h = _rms(x_ref[...], g_ref[...]) * (1.0 + sc_ref[...]) + sh_ref[...]
    p = _dot(h.astype(BF16), w_ref[...])
    hy_ref[...] = p[:, :HY_END].astype(hy_ref.dtype)
    cos = cos_ref[...]
    sin = sin_ref[...]
    lane = lax.broadcasted_iota(jnp.int32, cos.shape, 1)
    first = (lane % 32) < 16

    def rope(z):
        swapped = jnp.where(first, pltpu.roll(z, 112, 1), pltpu.roll(z, 16, 1))
        return z * cos + swapped * sin

    for j in range(ATT_W // 128):
        qj = rope(p[:, HY_END + j * 128:HY_END + (j + 1) * 128]) * (HEAD_DIM ** -0.5)
        q_ref[:, j * 128:(j + 1) * 128] = qj.astype(BF16)
    k_ref[...] = rope(p[:, Q_END:K_END]).astype(BF16)
    v_ref[...] = p[:, K_END:V_END].astype(BF16)
    u_ref[...] = p[:, V_END:]


def _rope_tables(n):
    quarter = HEAD_DIM // 4
    inv_freq = ROPE_BASE ** (-jnp.arange(quarter, dtype=F32) / quarter)
    t = jnp.arange(n)
    rows = (t // GRID_W).astype(F32)
    cols = (t % GRID_W).astype(F32)
    ang_r = rows[:, None] * inv_freq[None, :]
    ang_c = cols[:, None] * inv_freq[None, :]
    cos64 = jnp.concatenate([jnp.cos(ang_r)] * 2 + [jnp.cos(ang_c)] * 2, axis=-1)
    sin64 = jnp.concatenate([-jnp.sin(ang_r), jnp.sin(ang_r), -jnp.sin(ang_c), jnp.sin(ang_c)], axis=-1)
    cos = jnp.concatenate([jnp.tile(cos64, (1, 2)), jnp.ones((TM, 128), F32)], axis=0)
    sin = jnp.concatenate([jnp.tile(sin64, (1, 2)), jnp.zeros((TM, 128), F32)], axis=0)
    return cos, sin


def _input_proj(x_all, g, mod4, w_in_bf, cos, sin, nb, n, layer):
    t_all = x_all.shape[0]
    n_lat_tiles = nb * n // TM
    tpb = n // TM

    def mrow(i):
        return jnp.where(i < n_lat_tiles, i // tpb, nb)

    def trow(i):
        return jnp.where(i < n_lat_tiles, i % tpb, tpb)

    row = lambda w: pl.BlockSpec((TM, w), lambda i: (i, 0))
    return pl.pallas_call(
        _in_kernel,
        grid=(t_all // TM,),
        in_specs=[
            row(D_MODEL),
            pl.BlockSpec((1, D_MODEL), lambda i: (0, 0)),
            pl.BlockSpec((None, None, 1, D_MODEL), lambda i: (mrow(i), 0, 0, 0)),
            pl.BlockSpec((None, None, 1, D_MODEL), lambda i: (mrow(i), 1, 0, 0)),
            pl.BlockSpec((None, D_MODEL, IN_W), lambda i: (layer, 0, 0)),
            pl.BlockSpec((TM, 128), lambda i: (trow(i), 0)),
            pl.BlockSpec((TM, 128), lambda i: (trow(i), 0)),
        ],
        out_specs=[row(HY_END), row(ATT_W), row(KV_W), row(KV_W), row(S5_W)],
        out_shape=[
            jax.ShapeDtypeStruct((t_all, HY_END), BF16),
            jax.ShapeDtypeStruct((t_all, ATT_W), BF16),
            jax.ShapeDtypeStruct((t_all, KV_W), BF16),
            jax.ShapeDtypeStruct((t_all, KV_W), BF16),
            jax.ShapeDtypeStruct((t_all, S5_W), F32),
        ],
        compiler_params=_cparams("arbitrary"),
        name="norm_mod_inproj",
    )(x_all, g.reshape(1, D_MODEL), mod4, mod4, w_in_bf, cos, sin)


def _sconv_kernel(z_ref, w_ref, b_ref, o_ref):
    z = z_ref[...].astype(F32)
    length = z.shape[0]
    row = lax.broadcasted_iota(jnp.int32, z.shape, 0)
    zm = jnp.where(row == 0, 0.0, pltpu.roll(z, 1, 0))
    zp = jnp.where(row == length - 1, 0.0, pltpu.roll(z, length - 1, 0))
    o_ref[...] = (b_ref[...] + zm * w_ref[0:1, :] + z * w_ref[1:2, :] + zp * w_ref[2:3, :]).astype(o_ref.dtype)


def _short_conv(hy, conv_w, conv_b, nb, n, c):
    t_all = hy.shape[0]
    out = None
    for length, off in ((n, 0), (c, nb * n // c)):
        kwargs = {}
        args = [hy, conv_w, conv_b.reshape(1, HY_END)]
        in_specs = [
            pl.BlockSpec((length, 256), lambda b, j, off=off: (off + b, j)),
            pl.BlockSpec((3, 256), lambda b, j: (0, j)),
            pl.BlockSpec((1, 256), lambda b, j: (0, j)),
        ]
        kern = _sconv_kernel
        if out is not None:
            args.append(out)
            in_specs.append(pl.BlockSpec(memory_space=pl.ANY))
            kwargs["input_output_aliases"] = {3: 0}
            kern = lambda z, w, b, prev, o: _sconv_kernel(z, w, b, o)
        out = pl.pallas_call(
            kern,
            grid=(nb, 3),
            in_specs=in_specs,
            out_specs=pl.BlockSpec((length, 256), lambda b, j, off=off: (off + b, j)),
            out_shape=jax.ShapeDtypeStruct((t_all, HY_END), BF16),
            compiler_params=_cparams("arbitrary", "arbitrary"),
            name="hyena_short_conv",
            **kwargs,
        )(*args)
    return out


def _dft_plan(length):
    p = min(DFT_P, length)
    return p, 2 * length // p


def _dft_tables(length):
    p, na = _dft_plan(length)
    m = 2 * length
    a = jnp.arange(na, dtype=jnp.int32)
    ang_w = (2.0 * math.pi / na) * ((a[:, None] * a[None, :]) % na).astype(F32)
    twr, twi = jnp.cos(ang_w), -jnp.sin(ang_w)
    r = jnp.arange(p, dtype=jnp.int32)
    k = a[:, None, None] + na * r[None, :, None]
    ang = (2.0 * math.pi / m) * ((k * r[None, None, :]) % m).astype(F32)
    er, ei = jnp.cos(ang), -jnp.sin(ang)
    return twr, twi, er, ei


def _filter_kernel(twr_ref, twi_ref, w1_ref, b1_ref, f1_ref, w2_ref, b2_ref, f2_ref, w3_ref,
                   bands_ref, decay_ref, er_ref, ei_ref, h_ref, kern_ref, norm_ref, *, length, p, na):
    ka = pl.program_id(0)
    m = 2 * length
    hdot = functools.partial(jnp.dot, precision=HIGHEST, preferred_element_type=F32)

    @pl.when(ka == 0)
    def _():
        norm = jnp.zeros((1, 2 * HY_W), F32)
        for a in range(na):
            n_idx = a * p + lax.broadcasted_iota(jnp.int32, (p, 1), 0)
            is_fwd = n_idx < length
            pos = jnp.where(is_fwd, n_idx, m - n_idx).astype(F32)
            t = pos / float(max(length - 1, 1))
            ang = (2.0 * math.pi / length) * pos * bands_ref[...]
            pre = (t * w1_ref[0:1, :] + hdot(jnp.cos(ang), w1_ref[1:1 + HY_BANDS, :])
                   + hdot(-jnp.sin(ang), w1_ref[1 + HY_BANDS:, :]) + b1_ref[...])
            hid = jnp.sin(f1_ref[...] * pre)
            hid = jnp.sin(f2_ref[...] * (hdot(hid, w2_ref[...]) + b2_ref[...]))
            taps = hdot(hid, w3_ref[...])
            wnd = jnp.exp(-t * decay_ref[...])
            live = n_idx != length
            for o in range(2):
                fwd = taps[:, o * 2 * HY_W:o * 2 * HY_W + HY_W]
                bwd = taps[:, o * 2 * HY_W + HY_W:(o + 1) * 2 * HY_W]
                kern = jnp.where(live, jnp.where(is_fwd, fwd, bwd) * wnd, 0.0)
                kern_ref[a, :, o * HY_W:(o + 1) * HY_W] = kern
            norm = norm + jnp.sum(jnp.abs(kern_ref[a]), axis=0, keepdims=True)
        norm_ref[...] = norm

    gr = jnp.zeros((p, 2 * HY_W), F32)
    gi = jnp.zeros((p, 2 * HY_W), F32)
    for a in range(na):
        slab = kern_ref[a]
        gr = gr + twr_ref[ka, a] * slab
        gi = gi + twi_ref[ka, a] * slab
    er = er_ref[...]
    ei = ei_ref[...]
    grb = gr.astype(BF16)
    gib = gi.astype(BF16)
    inv = 1.0 / norm_ref[...]
    h_ref[0] = ((_dot(er, grb) - _dot(ei, gib)) * inv).astype(h_ref.dtype)
    h_ref[1] = ((_dot(ei, grb) + _dot(er, gib)) * inv).astype(h_ref.dtype)


def _filter_spectrum(length, tables, w1, b1, f1, w2, b2, f2, w3):
    p, na = _dft_plan(length)
    twr, twi, er, ei = tables
    bands = jnp.linspace(1e-4, HY_BANDS - 1, HY_BANDS, dtype=F32).reshape(1, HY_BANDS)
    decay = jnp.abs(jnp.linspace(HY_DECAY_MIN, HY_DECAY_MAX, HY_W, dtype=F32)).reshape(1, HY_W)
    full = lambda arr: pl.BlockSpec(arr.shape, lambda ka, *_: (0,) * arr.ndim)
    vec = lambda v: v.reshape(1, -1)
    ins = [w1, vec(b1), vec(f1), w2, vec(b2), vec(f2), w3, bands, decay]
    return pl.pallas_call(
        functools.partial(_filter_kernel, length=length, p=p, na=na),
        grid_spec=pltpu.PrefetchScalarGridSpec(
            num_scalar_prefetch=2,
            grid=(na,),
            in_specs=[full(x) for x in ins] + [
                pl.BlockSpec((None, p, p), lambda ka, *_: (ka, 0, 0)),
                pl.BlockSpec((None, p, p), lambda ka, *_: (ka, 0, 0)),
            ],
            out_specs=pl.BlockSpec((None, 2, p, 2 * HY_W), lambda ka, *_: (ka, 0, 0, 0)),
            scratch_shapes=[pltpu.VMEM((na, p, 2 * HY_W), F32), pltpu.VMEM((1, 2 * HY_W), F32)],
        ),
        out_shape=jax.ShapeDtypeStruct((na, 2, p, 2 * HY_W), BF16),
        compiler_params=_cparams("arbitrary"),
        name="hyena_filter_spectrum",
    )(twr, twi, *[vec(x) if x.ndim == 1 else x for x in ins], er, ei)


def _cadd(x, y):
    return x[0] + y[0], x[1] + y[1]


def _csub(x, y):
    return x[0] - y[0], x[1] - y[1]


def _radix_mix(parts, sign):
    if len(parts) == 1:
        return parts
    p0, p1, p2, p3 = parts
    t0, t1, t2, t3 = _cadd(p0, p2), _csub(p0, p2), _cadd(p1, p3), _csub(p1, p3)
    it3 = (-t3[1], t3[0])
    if sign < 0:
        return [_cadd(t0, t2), _csub(t1, it3), _csub(t0, t2), _cadd(t1, it3)]
    return [_cadd(t0, t2), _cadd(t1, it3), _csub(t0, t2), _csub(t1, it3)]


def _conv_kernel(twr_ref, twi_ref, u_ref, gate_ref, bias_ref, er_ref, ei_ref, h_ref, *rest, length, p, radix):
    o_ref, acc_ref = rest[-2], rest[-1]
    q = pl.program_id(1)
    nz = length // p
    tdot = lambda x, y: lax.dot_general(x, y, (((0,), (0,)), ((), ())), preferred_element_type=F32)

    parts = [None] * radix
    for a in range(nz):
        wr = twr_ref[q, a]
        wi = twi_ref[q, a]
        zr = u_ref[a * p:(a + 1) * p, :].astype(F32)
        zi = u_ref[length + a * p:length + (a + 1) * p, :].astype(F32)
        term = (wr * zr - wi * zi, wr * zi + wi * zr)
        parts[a % radix] = term if parts[a % radix] is None else _cadd(parts[a % radix], term)
    zero = jnp.zeros((p, o_ref.shape[-1]), F32)
    parts = [(zero, zero) if t is None else t for t in parts]
    g = _radix_mix(parts, -1)

    v = []
    for m in range(radix):
        grb = g[m][0].astype(BF16)
        gib = g[m][1].astype(BF16)
        er = er_ref[m]
        ei = ei_ref[m]
        sr = _dot(er, grb) - _dot(ei, gib)
        si = _dot(ei, grb) + _dot(er, gib)
        hr = h_ref[m, 0].astype(F32)
        hi = h_ref[m, 1].astype(F32)
        yr = (sr * hr - si * hi).astype(BF16)
        yi = (sr * hi + si * hr).astype(BF16)
        v.append((tdot(er, yr) + tdot(ei, yi), tdot(er, yi) - tdot(ei, yr)))
    qs = _radix_mix(v, +1)

    @pl.when(q == 0)
    def _():
        acc_ref[...] = jnp.zeros(acc_ref.shape, F32)

    scale = 1.0 / (2 * length)
    for a in range(nz):
        wr = twr_ref[q, a] * scale
        wi = twi_ref[q, a] * scale
        vr, vi = qs[a % radix]
        acc_ref[a * p:(a + 1) * p, :] += wr * vr + wi * vi
        acc_ref[length + a * p:length + (a + 1) * p, :] += wr * vi - wi * vr

    @pl.when(q == pl.num_programs(1) - 1)
    def _():
        u = u_ref[...].astype(F32)
        o_ref[...] = (gate_ref[...].astype(F32) * (acc_ref[...] + u * bias_ref[...])).astype(o_ref.dtype)


def _long_conv(u_arr, u_col, gate_arr, gate_col, bias, spec, order, tables_bf, length, row_off, nb, prev_out):
    p, na = _dft_plan(length)
    radix = 4 if na % 4 == 0 and na >= 8 else 1
    nq = na // radix
    twr, twi, er, ei = tables_bf
    t_all = u_arr.shape[0]
    blk = 2 * length
    off = row_off // blk
    mat = lambda: pl.BlockSpec((radix, None, p, p), lambda j, q, *_: (0, q, 0, 0))
    args = [twr, twi, u_arr, gate_arr, bias.reshape(1, HY_W), er.reshape(radix, nq, p, p),
            ei.reshape(radix, nq, p, p), spec.reshape(radix, nq, 2, p, 2 * HY_W)]
    in_specs = [
        pl.BlockSpec((blk, HY_W), lambda j, q, *_: (off + j, u_col), pipeline_mode=pl.Buffered(1)),
        pl.BlockSpec((blk, HY_W), lambda j, q, *_: (off + j, gate_col), pipeline_mode=pl.Buffered(1)),
        pl.BlockSpec((1, HY_W), lambda j, q, *_: (0, 0)),
        mat(), mat(),
        pl.BlockSpec((radix, None, 2, p, HY_W), lambda j, q, *_: (0, q, 0, 0, order)),
    ]
    kwargs = {}
    if prev_out is not None:
        args.append(prev_out)
        in_specs.append(pl.BlockSpec(memory_space=pl.ANY))
        kwargs["input_output_aliases"] = {len(args) - 1: 0}
    return pl.pallas_call(
        functools.partial(_conv_kernel, length=length, p=p, radix=radix),
        grid_spec=pltpu.PrefetchScalarGridSpec(
            num_scalar_prefetch=2,
            grid=(nb // 2, nq),
            in_specs=in_specs,
            out_specs=pl.BlockSpec((blk, HY_W), lambda j, q, *_: (off + j, 0)),
            scratch_shapes=[pltpu.VMEM((blk, HY_W), F32)],
        ),
        out_shape=jax.ShapeDtypeStruct((t_all, HY_W), BF16),
        compiler_params=_cparams("arbitrary", "arbitrary"),
        name="hyena_long_conv",
        **kwargs,
    )(*args)


def _attn_kernel(sink_ref, q_ref, *refs, local):
    if local:
        kp_ref, kc_ref, kn_ref, vp_ref, vc_ref, vn_ref, kx_ref, vx_ref, o_ref = refs
        n = pl.program_id(1)
        last = pl.num_programs(1) - 1
        ki = lax.broadcasted_iota(jnp.int32, (QBLK, QBLK), 0)
        qi = lax.broadcasted_iota(jnp.int32, (QBLK, QBLK), 1)
        ok_prev = (ki >= qi) & (n > 0)
        ok_next = (ki <= qi) & (n < last)
    else:
        kx_ref, vx_ref, o_ref = refs
    nt = (((1,), (1,)), ((), ()))
    nq = q_ref.shape[0]
    if local:
        bias1 = jnp.concatenate([jnp.where(ok_prev, 0.0, NEG_INF), jnp.zeros((QBLK, QBLK), F32),
                                 jnp.where(ok_next, 0.0, NEG_INF), jnp.zeros((kx_ref.shape[0], QBLK), F32)], axis=0)
        bias = jnp.concatenate([bias1] * Q_PER_KV, axis=1)
    for kv in range(N_KV_HEADS):
        cs = slice(kv * HEAD_DIM, (kv + 1) * HEAD_DIM)
        heads = range(kv * Q_PER_KV, (kv + 1) * Q_PER_KV)
        qs = jnp.concatenate([q_ref[:, h * HEAD_DIM:(h + 1) * HEAD_DIM] for h in heads], axis=0)
        sink = jnp.concatenate([jnp.full((1, nq), sink_ref[h], F32) for h in heads], axis=1)
        if local:
            keys = jnp.concatenate([kp_ref[:, cs], kc_ref[:, cs], kn_ref[:, cs], kx_ref[:, cs]], axis=0)
            vals = jnp.concatenate([vp_ref[:, cs], vc_ref[:, cs], vn_ref[:, cs], vx_ref[:, cs]], axis=0)
        else:
            keys, vals = kx_ref[:, cs], vx_ref[:, cs]
        s = lax.dot_general(keys, qs, nt, preferred_element_type=F32)
        if local:
            s = s + bias
        mx = jnp.maximum(jnp.max(s, axis=0, keepdims=True), sink)
        p = jnp.exp((s - mx).astype(BF16))
        vals1 = jnp.concatenate([vals, jnp.ones(vals.shape, BF16)], axis=1)
        pv = lax.dot_general(vals1, p, (((0,), (0,)), ((), ())), preferred_element_type=F32)
        den = pv[HEAD_DIM:HEAD_DIM + 1, :] + jnp.exp(sink - mx)
        out = pv[:HEAD_DIM, :] / den
        for g, h in enumerate(heads):
            o_ref[:, h * HEAD_DIM:(h + 1) * HEAD_DIM] = out[:, g * nq:(g + 1) * nq].T.astype(o_ref.dtype)


def _attention(q, k, v, sink, nb, n, c):
    t_all = q.shape[0]
    nqb = n // QBLK
    ctx_blk0 = nb * n // c
    kvspec = lambda fn: pl.BlockSpec((QBLK, KV_W), fn)
    prev = lambda b, j: (b * nqb + jnp.maximum(j - 1, 0), 0)
    cur = lambda b, j: (b * nqb + j, 0)
    nxt = lambda b, j: (b * nqb + jnp.minimum(j + 1, nqb - 1), 0)
    ctxs = pl.BlockSpec((c, KV_W), lambda b, j: (ctx_blk0 + b, 0))
    smem = pl.BlockSpec(memory_space=pltpu.SMEM)
    lat = pl.pallas_call(
        functools.partial(_attn_kernel, local=True),
        grid=(nb, nqb),
        in_specs=[smem, pl.BlockSpec((QBLK, ATT_W), cur),
                  kvspec(prev), kvspec(cur), kvspec(nxt), kvspec(prev), kvspec(cur), kvspec(nxt), ctxs, ctxs],
        out_specs=pl.BlockSpec((QBLK, ATT_W), cur),
        out_shape=jax.ShapeDtypeStruct((t_all, ATT_W), BF16),
        compiler_params=_cparams("arbitrary", "arbitrary"),
        name="banded_attention",
    )(sink, q, k, k, k, v, v, v, k, v)
    ctx1 = pl.BlockSpec((c, KV_W), lambda b: (ctx_blk0 + b, 0))
    return pl.pallas_call(
        lambda s, qq, kx, vx, prev_o, o: _attn_kernel(s, qq, kx, vx, o, local=False),
        grid=(nb,),
        in_specs=[smem, pl.BlockSpec((c, ATT_W), lambda b: (ctx_blk0 + b, 0)), ctx1, ctx1,
                  pl.BlockSpec(memory_space=pl.ANY)],
        out_specs=pl.BlockSpec((c, ATT_W), lambda b: (ctx_blk0 + b, 0)),
        out_shape=jax.ShapeDtypeStruct((t_all, ATT_W), BF16),
        input_output_aliases={4: 0},
        compiler_params=_cparams("arbitrary"),
        name="context_attention",
    )(sink, q, k, v, lat)


def _s5disc_kernel(lre_ref, lim_ref, dt_ref, bre_ref, bim_ref, are_ref, aim_ref, bbre_ref, bbim_ref):
    lam_re = lre_ref[...]
    lam_im = lim_ref[...]
    dt = jnp.exp(dt_ref[...])
    mag = jnp.exp(lam_re * dt)
    ab_re = mag * jnp.cos(lam_im * dt)
    ab_im = mag * jnp.sin(lam_im * dt)
    num_re = ab_re - 1.0
    num_im = ab_im
    den = lam_re * lam_re + lam_im * lam_im
    co_re = (num_re * lam_re + num_im * lam_im) / den
    co_im = (num_im * lam_re - num_re * lam_im) / den
    b_re = bre_ref[...]
    b_im = bim_ref[...]
    are_ref[...] = ab_re
    aim_ref[...] = ab_im
    bbre_ref[...] = co_re * b_re - co_im * b_im
    bbim_ref[...] = co_re * b_im + co_im * b_re


def _s5_discretize(lam_re, lam_im, log_dt, b_re, b_im):
    lead = lam_re.shape[:-1]
    rep = lambda a: jnp.repeat(a.reshape(-1, 1, S5_STATE), S5_CPG, axis=1).reshape(-1, S5_STATE)
    dt = jnp.broadcast_to(log_dt.reshape(-1, 1, 1), (math.prod(lead), S5_CPG, S5_STATE)).reshape(-1, S5_STATE)
    tr = lambda b: jnp.swapaxes(b, -1, -2).reshape(-1, S5_STATE)
    rows = math.prod(lead) * S5_CPG
    shp = jax.ShapeDtypeStruct((rows, S5_STATE), F32)
    ab_re, ab_im, bb_re, bb_im = pl.pallas_call(
        _s5disc_kernel, out_shape=[shp] * 4, name="s5_discretize",
    )(rep(lam_re), rep(lam_im), dt, tr(b_re), tr(b_im))
    full = lead + (S5_CPG, S5_STATE)
    return (ab_re.reshape(full)[..., 0, :], ab_im.reshape(full)[..., 0, :],
            bb_re.reshape(full), bb_im.reshape(full))


def _s5_kernel(*refs, tc, nbatch, n_ctx_steps):
    u_refs = (refs[:nbatch], refs[nbatch:2 * nbatch])
    bmat_ref, cmat_ref, a_ref = refs[2 * nbatch:2 * nbatch + 3]
    y_refs = (refs[2 * nbatch + 3:2 * nbatch + 5], refs[2 * nbatch + 5:2 * nbatch + 7])
    st_ref, bu_ref, tm_ref = refs[2 * nbatch + 7:]
    i = pl.program_id(0)
    gp = S5_GROUPS * S5_STATE
    halves = S5_W // 128

    @pl.when(i == 0)
    def _():
        st_ref[...] = jnp.zeros(st_ref.shape, F32)

    for d in range(2):
        for b in range(nbatch):
            for j in range(halves):
                tm_ref[j, pl.ds(b, tc, stride=nbatch), :] = u_refs[d][b][:, j * 128:(j + 1) * 128]
        u_tm = jnp.concatenate([tm_ref[j] for j in range(halves)], axis=1)
        bu_ref[...] = _dot(u_tm.astype(BF16), bmat_ref[d])
        ar = a_ref[d, 0]
        ai = a_ref[d, 1]

        def body(j, carry, d=d, ar=ar, ai=ai):
            hr, hi = carry
            t = j if d == 0 else tc - 1 - j
            r0 = pl.multiple_of(t * nbatch, nbatch)
            nr = ar * hr - ai * hi + bu_ref[pl.ds(r0, nbatch), 0:gp]
            ni = ar * hi + ai * hr + bu_ref[pl.ds(r0, nbatch), gp:2 * gp]
            bu_ref[pl.ds(r0, nbatch), 0:gp] = nr
            bu_ref[pl.ds(r0, nbatch), gp:2 * gp] = ni
            return nr, ni

        hr, hi = lax.fori_loop(0, tc, body, (st_ref[d, 0], st_ref[d, 1]), unroll=2)
        st_ref[d, 0] = hr
        st_ref[d, 1] = hi
        y = _dot(bu_ref[...].astype(BF16), cmat_ref[d])
        for j in range(halves):
            tm_ref[j] = y[:, j * 128:(j + 1) * 128]

        def emit(y_ref):
            for b in range(nbatch):
                for j in range(halves):
                    y_ref[b, :, j * 128:(j + 1) * 128] = tm_ref[j, pl.ds(b, tc, stride=nbatch), :]

        pl.when(i >= n_ctx_steps)(functools.partial(emit, y_refs[d][0]))
        pl.when(i < n_ctx_steps)(functools.partial(emit, y_refs[d][1]))


def _s5_scan(u, bmat, cmat, a_bc, nb, n, c):
    tc = S5_TC
    rows = tc * nb
    nl, nc = n // tc, c // tc
    gp = S5_GROUPS * S5_STATE
    ctx0 = nb * nl

    def chunk(step, b, backward):
        in_ctx = step < nc
        ctx_chunk = (nc - 1 - step) if backward else step
        lat_chunk = (nl - 1 - (step - nc)) if backward else (step - nc)
        return jnp.where(in_ctx, ctx0 + b * nc + ctx_chunk, b * nl + lat_chunk)

    specs = [pl.BlockSpec((tc, S5_W), lambda i, b=b, bw=bw: (chunk(i, b, bw), 0))
             for bw in (False, True) for b in range(nb)]
    full = lambda arr: pl.BlockSpec(arr.shape, lambda i: (0,) * arr.ndim)
    oblk = lambda fn: pl.BlockSpec((nb, tc, S5_W), lambda i: (0, fn(i), 0))
    out_specs = [oblk(lambda i: jnp.maximum(i - nc, 0)), oblk(lambda i: jnp.minimum(i, nc - 1)),
                 oblk(lambda i: jnp.minimum(nl + nc - 1 - i, nl - 1)), oblk(lambda i: jnp.maximum(nc - 1 - i, 0))]
    lat = jax.ShapeDtypeStruct((nb, n, S5_W), F32)
    ctx = jax.ShapeDtypeStruct((nb, c, S5_W), F32)
    return pl.pallas_call(
        functools.partial(_s5_kernel, tc=tc, nbatch=nb, n_ctx_steps=nc),
        grid=(nl + nc,),
        in_specs=specs + [full(bmat), full(cmat), full(a_bc)],
        out_specs=out_specs,
        out_shape=[lat, ctx, lat, ctx],
        scratch_shapes=[pltpu.VMEM((2, 2, nb, gp), F32), pltpu.VMEM((rows, 2 * gp), F32),
                        pltpu.VMEM((S5_W // 128, rows, 128), F32)],
        compiler_params=_cparams("arbitrary"),
        name="s5_scan",
    )(*([u] * (2 * nb)), bmat, cmat, a_bc)


def _s5_matrices(ab_re, ab_im, bb_re, bb_im, c_re, c_im, nb):
    eye = jnp.eye(S5_GROUPS, dtype=F32)
    bd_in = lambda b: jnp.einsum("dgcp,gh->dgchp", b, eye).reshape(2, S5_W, S5_GROUPS * S5_STATE)
    bd_out = lambda cc: jnp.einsum("dgcp,gh->dgphc", cc, eye).reshape(2, S5_GROUPS * S5_STATE, S5_W)
    bmat = jnp.concatenate([bd_in(bb_re), bd_in(bb_im)], axis=-1).astype(BF16)
    cmat = jnp.concatenate([bd_out(c_re), -bd_out(c_im)], axis=1).astype(BF16)
    a = jnp.stack([ab_re.reshape(2, -1), ab_im.reshape(2, -1)], axis=1)
    a_bc = jnp.broadcast_to(a[:, :, None, :], (2, 2, nb, S5_GROUPS * S5_STATE))
    return bmat, cmat, a_bc


def _out_kernel(x_ref, hy_ref, att_ref, yfl_ref, yfc_ref, ybl_ref, ybc_ref, u_ref, d_ref, gluw_ref, glub_ref,
                ng_ref, wout_ref, g1_ref, sh2_ref, sc2_ref, n2g_ref, rw_ref, rb_ref, tri_ref,
                xo_ref, h2_ref, eidx_ref, gate_ref, rank_ref, cnt_ref, *, n_lat_tiles):
    is_ctx = pl.program_id(0) >= n_lat_tiles
    y = jnp.where(is_ctx, yfc_ref[...] + ybc_ref[...], yfl_ref[...] + ybl_ref[...])
    y = y + u_ref[...] * d_ref[...]
    g = jax.nn.gelu(y)
    s5 = g * jax.nn.sigmoid(_dot(g.astype(BF16), gluw_ref[...]) + glub_ref[...])

    def nrm(part):
        return part * lax.rsqrt(jnp.mean(part * part, axis=-1, keepdims=True) + EPS)

    mix = jnp.concatenate([nrm(hy_ref[...].astype(F32)), nrm(att_ref[...].astype(F32)), nrm(s5)], axis=-1) * ng_ref[...]
    x = x_ref[...] + g1_ref[...] * _dot(mix.astype(BF16), wout_ref[...])
    xo_ref[...] = x
    h2 = _rms(x, n2g_ref[...]) * (1.0 + sc2_ref[...]) + sh2_ref[...]
    h2_hi = h2.astype(BF16)
    h2_ref[...] = h2_hi

    h2_lo = (h2 - h2_hi.astype(F32)).astype(BF16)
    logits = (_dot(h2_hi, rw_ref[0]) + _dot(h2_lo, rw_ref[0]) + _dot(h2_hi, rw_ref[1])).T
    e = jnp.exp(logits - jnp.max(logits, axis=0, keepdims=True))
    probs = e / jnp.sum(e, axis=0, keepdims=True)
    sel = probs + rb_ref[...]
    rows = [sel[r:r + 1, :] for r in range(N_EXPERTS)]
    best = None
    for grp in range(N_EXPERT_GROUPS):
        a = rows[grp * EXPERTS_PER_GROUP:(grp + 1) * EXPERTS_PER_GROUP]
        score = None
        for p0 in range(EXPERTS_PER_GROUP):
            for p1 in range(p0 + 1, EXPERTS_PER_GROUP):
                pair = a[p0] + a[p1]
                score = pair if score is None else jnp.maximum(score, pair)
        if best is None:
            best, gidx = score, jnp.zeros(score.shape, jnp.int32)
        else:
            better = score > best
            gidx = jnp.where(better, grp, gidx)
            best = jnp.where(better, score, best)
    ing = []
    for j in range(EXPERTS_PER_GROUP):
        v = rows[j]
        for grp in range(1, N_EXPERT_GROUPS):
            v = jnp.where(gidx == grp, rows[grp * EXPERTS_PER_GROUP + j], v)
        ing.append(v)
    first_v, first_i = ing[0], jnp.zeros(gidx.shape, jnp.int32)
    for j in range(1, EXPERTS_PER_GROUP):
        better = ing[j] > first_v
        first_i = jnp.where(better, j, first_i)
        first_v = jnp.where(better, ing[j], first_v)
    second_v, second_i = None, None
    for j in range(EXPERTS_PER_GROUP):
        cand = jnp.where(first_i == j, -jnp.inf, ing[j])
        if second_v is None:
            second_v, second_i = cand, jnp.zeros(gidx.shape, jnp.int32)
        else:
            better = cand > second_v
            second_i = jnp.where(better, j, second_i)
            second_v = jnp.where(better, cand, second_v)
    e0 = gidx * EXPERTS_PER_GROUP + first_i
    e1 = gidx * EXPERTS_PER_GROUP + second_i
    eid = lax.broadcasted_iota(jnp.int32, probs.shape, 0)
    oh0 = eid == e0
    oh1 = eid == e1
    p0v = jnp.sum(jnp.where(oh0, probs, 0.0), axis=0, keepdims=True)
    p1v = jnp.sum(jnp.where(oh1, probs, 0.0), axis=0, keepdims=True)
    tot = p0v + p1v
    member = jnp.where(oh0 | oh1, 1.0, 0.0)
    before = _dot(member.astype(BF16), tri_ref[...])
    r0 = jnp.sum(jnp.where(oh0, before, 0.0), axis=0, keepdims=True)
    r1 = jnp.sum(jnp.where(oh1, before, 0.0), axis=0, keepdims=True)
    cnt_ref[...] = jnp.sum(member, axis=1, keepdims=True)
    eidx_ref[...] = jnp.concatenate([e0, e1], axis=0)
    gate_ref[...] = jnp.concatenate([p0v / tot, p1v / tot], axis=0)
    rank_ref[...] = jnp.concatenate([r0, r1], axis=0).astype(jnp.int32)


def _mixer_out(x_all, hy, att, y_scan, u, s5_d, glu_w_bf, glu_b, mix_g, w_out_bf, mod4, n2g,
               router_hl, router_b, nb, n, layer):
    t_all = x_all.shape[0]
    n_lat_tiles = nb * n // TM
    tpb = n // TM
    assert y_scan[1].shape[1] == TM
    mrow = lambda i: jnp.where(i < n_lat_tiles, i // tpb, nb)
    row = lambda w: pl.BlockSpec((TM, w), lambda i: (i, 0))
    full = lambda arr: pl.BlockSpec(arr.shape, lambda i: (0,) * arr.ndim)
    layer_mat = lambda arr: pl.BlockSpec((None,) + arr.shape[1:], lambda i: (layer, 0, 0))
    modspec = lambda k: pl.BlockSpec((None, None, 1, D_MODEL), lambda i, k=k: (mrow(i), k, 0, 0))
    lat_y = pl.BlockSpec((None, TM, S5_W), lambda i: (jnp.minimum(i, n_lat_tiles - 1) // tpb,
                                                       jnp.minimum(i, n_lat_tiles - 1) % tpb, 0))
    ctx_y = pl.BlockSpec((None, TM, S5_W), lambda i: (jnp.maximum(i - n_lat_tiles, 0), 0, 0))
    tri = (lax.broadcasted_iota(jnp.int32, (TM, TM), 0) < lax.broadcasted_iota(jnp.int32, (TM, TM), 1)).astype(BF16)
    v1 = lambda a: a.reshape(1, -1)
    ins = [v1(s5_d), glu_w_bf, v1(glu_b), v1(mix_g), w_out_bf]
    in_w_specs = [full(ins[0]), layer_mat(glu_w_bf), full(ins[2]), full(ins[3]), layer_mat(w_out_bf)]
    tail = [v1(n2g), router_hl, router_b.reshape(N_EXPERTS, 1), tri]
    tok = lambda: pl.BlockSpec((2, TM), lambda i: (0, i))
    return pl.pallas_call(
        functools.partial(_out_kernel, n_lat_tiles=n_lat_tiles),
        grid=(t_all // TM,),
        in_specs=[row(D_MODEL), row(HY_W), row(ATT_W), lat_y, ctx_y, lat_y, ctx_y, row(S5_W)] + in_w_specs
                 + [modspec(2), modspec(3), modspec(4)] + [full(a) for a in tail],
        out_specs=[row(D_MODEL), row(D_MODEL), tok(), tok(), tok(),
                   pl.BlockSpec((None, N_EXPERTS, 1), lambda i: (i, 0, 0))],
        out_shape=[
            jax.ShapeDtypeStruct((t_all, D_MODEL), F32),
            jax.ShapeDtypeStruct((t_all, D_MODEL), BF16),
            jax.ShapeDtypeStruct((2, t_all), jnp.int32),
            jax.ShapeDtypeStruct((2, t_all), F32),
            jax.ShapeDtypeStruct((2, t_all), jnp.int32),
            jax.ShapeDtypeStruct((t_all // TM, N_EXPERTS, 1), F32),
        ],
        compiler_params=_cparams("arbitrary"),
        name="mixer_out_router",
    )(x_all, hy, att, *y_scan, u, *ins, mod4, mod4, mod4, *tail)


SORT_ROWS = 2 * TM + 8 * N_EXPERTS
CHUNK_BITS = 6
TOT_BITS = 7
TAIL_BITS = 6


def _for_chunks(n8, bits, fn):
    for b in range(bits - 1, -1, -1):
        done = ((n8 >> (b + 1)) << (b + 1)) * 8

        @pl.when(((n8 >> b) & 1) == 1)
        def _(done=done, rows=8 << b):
            fn(pl.multiple_of(done, 8), rows)


def _dispatch_kernel(loc_ref, dst_ref, n8_ref, tot_ref, tail_ref, h2_ref, eidx_ref, lrank_ref, adj_ref,
                     xb_ref, spos_ref, sorted_ref, zero_ref, sem, zsem):
    i = pl.program_id(0)
    n_tiles = pl.num_programs(0)
    slot = i % 2

    def wait_tile(tile, buf):
        def wait(off, rows):
            pltpu.make_async_copy(sorted_ref.at[buf, pl.ds(0, rows)], xb_ref.at[pl.ds(0, rows)], sem.at[buf]).wait()
        _for_chunks(tot_ref[tile], TOT_BITS, wait)

    @pl.when(i >= 2)
    def _():
        wait_tile(i - 2, slot)

    @pl.when(i == 0)
    def _():
        zero_ref[...] = jnp.zeros(zero_ref.shape, F32)
        for start_wait in (True, False):
            for e in range(N_EXPERTS):
                def tail(off, rows, e=e, start_wait=start_wait):
                    cp = pltpu.make_async_copy(zero_ref.at[pl.ds(0, rows)],
                                               xb_ref.at[pl.ds(pl.multiple_of(tail_ref[e] + off, 8), rows)], zsem)
                    cp.start() if start_wait else cp.wait()
                _for_chunks(tail_ref[N_EXPERTS + e], TAIL_BITS, tail)

    eid = lax.broadcasted_iota(jnp.int32, (N_EXPERTS, TM), 0)
    adj = adj_ref[...]
    spos = []
    for k in range(2):
        chunk_start = jnp.sum(jnp.where(eid == eidx_ref[k:k + 1, :], adj, 0.0), axis=0, keepdims=True)
        spos.append(lrank_ref[k:k + 1, :] + chunk_start.astype(jnp.int32))
    spos_ref[...] = jnp.concatenate(spos, axis=0)
    rows = lax.broadcasted_iota(jnp.int32, (SORT_ROWS, TM), 0)
    perm = jnp.where((rows == spos[0]) | (rows == spos[1]), 1.0, 0.0).astype(BF16)
    sorted_ref[slot] = _dot(perm, h2_ref[...])

    for e in range(N_EXPERTS):
        idx = i * N_EXPERTS + e

        def send(off, rows, idx=idx):
            pltpu.make_async_copy(sorted_ref.at[slot, pl.ds(pl.multiple_of(loc_ref[idx] + off, 8), rows)],
                                  xb_ref.at[pl.ds(pl.multiple_of(dst_ref[idx] + off, 8), rows)], sem.at[slot]).start()
        _for_chunks(n8_ref[idx], CHUNK_BITS, send)

    @pl.when(i == n_tiles - 1)
    def _():
        wait_tile(i - 1, 1 - slot)
        wait_tile(i, slot)


def _dispatch(tables, h2, eidx, lrank, adj, n_slots):
    t_all = h2.shape[0]
    tok = lambda: pl.BlockSpec((2, TM), lambda i, *_: (0, i))
    return pl.pallas_call(
        _dispatch_kernel,
        grid_spec=pltpu.PrefetchScalarGridSpec(
            num_scalar_prefetch=5,
            grid=(t_all // TM,),
            in_specs=[pl.BlockSpec((TM, D_MODEL), lambda i, *_: (i, 0)), tok(), tok(),
                      pl.BlockSpec((None, N_EXPERTS, 1), lambda i, *_: (i, 0, 0))],
            out_specs=[pl.BlockSpec(memory_space=pl.ANY), tok()],
            scratch_shapes=[pltpu.VMEM((2, SORT_ROWS, D_MODEL), F32), pltpu.VMEM((MOE_ROWS // 2, D_MODEL), F32),
                            pltpu.SemaphoreType.DMA((2,)), pltpu.SemaphoreType.DMA(())],
        ),
        out_shape=[jax.ShapeDtypeStruct((n_slots, D_MODEL), F32), jax.ShapeDtypeStruct((2, t_all), jnp.int32)],
        compiler_params=_cparams("arbitrary"),
        name="moe_dispatch",
    )(*tables, h2, eidx, lrank, adj)


def _ffn_kernel(be_ref, nu_ref, x_ref, wg_ref, wu_ref, wd_ref, o_ref, wbf_ref):
    j = pl.program_id(0)

    @pl.when(j < nu_ref[0])
    def _():
        @pl.when((j == 0) | (be_ref[j] != be_ref[jnp.maximum(j - 1, 0)]))
        def _():
            wbf_ref[0] = wg_ref[...].astype(BF16)
            wbf_ref[1] = wu_ref[...].astype(BF16)
            wbf_ref[2] = wd_ref[...].astype(BF16)

        x = x_ref[...].astype(BF16)
        a = _dot(x, wbf_ref[0])
        u = _dot(x, wbf_ref[1])
        hidden = (a * jax.nn.sigmoid(a)) * u
        o_ref[...] = _dot(hidden.astype(BF16), wbf_ref[2])


def _expert_ffn(block_e, n_used, xb, wg, wu, wd, layer):
    n_slots = xb.shape[0]
    blk = lambda j, be, nu: jnp.minimum(j, nu[0] - 1)
    wspec = lambda: pl.BlockSpec((None, None, D_MODEL, D_MODEL),
                                 lambda j, be, nu: (layer, be[blk(j, be, nu)], 0, 0))
    return pl.pallas_call(
        _ffn_kernel,
        grid_spec=pltpu.PrefetchScalarGridSpec(
            num_scalar_prefetch=2,
            grid=(n_slots // MOE_ROWS,),
            in_specs=[pl.BlockSpec((MOE_ROWS, D_MODEL), lambda j, be, nu: (blk(j, be, nu), 0)),
                      wspec(), wspec(), wspec()],
            out_specs=pl.BlockSpec((MOE_ROWS, D_MODEL), lambda j, be, nu: (blk(j, be, nu), 0)),
            scratch_shapes=[pltpu.VMEM((3, D_MODEL, D_MODEL), BF16)],
        ),
        out_shape=jax.ShapeDtypeStruct((n_slots, D_MODEL), F32),
        compiler_params=_cparams("arbitrary"),
        name="moe_expert_ffn",
    )(block_e, n_used, xb, wg, wu, wd)


def _combine_kernel(loc_ref, dst_ref, n8_ref, tot_ref, x_ref, spos_ref, gates_ref, g2_ref, yb_ref, *rest,
                    final):
    if final:
        fg_ref, xo_ref, ybuf_ref, sem = rest
    else:
        xo_ref, ybuf_ref, sem = rest
    i = pl.program_id(0)
    n_tiles = pl.num_programs(0)
    slot = i % 2

    def fetch_tile(tile, buf):
        for e in range(N_EXPERTS):
            idx = tile * N_EXPERTS + e

            def fetch(off, rows, idx=idx):
                pltpu.make_async_copy(yb_ref.at[pl.ds(pl.multiple_of(dst_ref[idx] + off, 8), rows)],
                                      ybuf_ref.at[buf, pl.ds(pl.multiple_of(loc_ref[idx] + off, 8), rows)],
                                      sem.at[buf]).start()
            _for_chunks(n8_ref[idx], CHUNK_BITS, fetch)

    @pl.when(i == 0)
    def _():
        ybuf_ref[...] = jnp.zeros(ybuf_ref.shape, F32)
        fetch_tile(0, 0)

    @pl.when(i + 1 < n_tiles)
    def _():
        fetch_tile(i + 1, 1 - slot)

    def wait(off, rows):
        pltpu.make_async_copy(yb_ref.at[pl.ds(0, rows)], ybuf_ref.at[slot, pl.ds(0, rows)], sem.at[slot]).wait()
    _for_chunks(tot_ref[i], TOT_BITS, wait)

    y_sorted = ybuf_ref[slot].astype(BF16)
    lane = lax.broadcasted_iota(jnp.int32, (TM, SORT_ROWS), 1)
    gates = gates_ref[...]
    weights = (jnp.where(lane == spos_ref[:, 0:1], gates[:, 0:1], 0.0)
               + jnp.where(lane == spos_ref[:, 1:2], gates[:, 1:2], 0.0))
    y = _dot(weights.astype(BF16), y_sorted)
    x = x_ref[...] + g2_ref[...] * y
    xo_ref[...] = _rms(x, fg_ref[...]) if final else x


def _combine(tables, x_all, spos_t, gates_t, mod4, yb, nb, n, final_g=None):
    final = final_g is not None
    t_all = nb * n if final else x_all.shape[0]
    n_lat_tiles = nb * n // TM
    tpb = n // TM
    mrow = lambda i: jnp.where(i < n_lat_tiles, i // tpb, nb)
    extra_specs = [pl.BlockSpec((1, D_MODEL), lambda i, *_: (0, 0))] if final else []
    extra_args = [final_g.reshape(1, D_MODEL)] if final else []
    return pl.pallas_call(
        functools.partial(_combine_kernel, final=final),
        grid_spec=pltpu.PrefetchScalarGridSpec(
            num_scalar_prefetch=4,
            grid=(t_all // TM,),
            in_specs=[pl.BlockSpec((TM, D_MODEL), lambda i, *_: (i, 0)),
                      pl.BlockSpec((TM, 2), lambda i, *_: (i, 0)),
                      pl.BlockSpec((TM, 2), lambda i, *_: (i, 0)),
                      pl.BlockSpec((None, None, 1, D_MODEL), lambda i, *_: (mrow(i), 5, 0, 0)),
                      pl.BlockSpec(memory_space=pl.ANY)] + extra_specs,
            out_specs=pl.BlockSpec((TM, D_MODEL), lambda i, *_: (i, 0)),
            scratch_shapes=[pltpu.VMEM((2, SORT_ROWS, D_MODEL), F32), pltpu.SemaphoreType.DMA((2,))],
        ),
        out_shape=jax.ShapeDtypeStruct((t_all, D_MODEL), F32),
        compiler_params=_cparams("arbitrary"),
        name="moe_combine",
    )(*tables, x_all, spos_t, gates_t, mod4, yb, *extra_args)


def kernel(x, c, ctx, c_ctx, norm1_g, norm2_g, ada_w, ada_b, w_in, w_out, mix_norm_g, hy_conv_w, hy_conv_b, hy_f_w1, hy_f_b1, hy_f_freq1, hy_f_w2, hy_f_b2, hy_f_freq2, hy_f_w3, hy_bias, attn_sink, s5_lam_re, s5_lam_im, s5_log_dt, s5_b_re, s5_b_im, s5_c_re, s5_c_im, s5_d, s5_glu_w, s5_glu_b, router_w, router_b, moe_w_gate, moe_w_up, moe_w_down, final_g):
    nb, n, d = x.shape
    cl = ctx.shape[1]
    depth = w_in.shape[0]
    t_all = nb * (n + cl)

    cc = jnp.zeros((16, d), F32).at[:nb].set(c).at[nb].set(c_ctx)
    mod_all = _modulation(cc, ada_w, ada_b).reshape(depth, 16, 6, 1, d)
    cos, sin = _rope_tables(n)
    ab_re, ab_im, bb_re, bb_im = _s5_discretize(s5_lam_re, s5_lam_im, s5_log_dt, s5_b_re, s5_b_im)

    tabs = {}
    for length in (n, cl):
        twr, twi, er, ei = _dft_tables(length)
        tabs[length] = (twr, twi, er.astype(BF16), ei.astype(BF16))

    rw_hi = router_w.astype(BF16)
    router_hl = jnp.stack([rw_hi, (router_w - rw_hi.astype(F32)).astype(BF16)])
    w_in_bf, w_out_bf, glu_w_bf = w_in.astype(BF16), w_out.astype(BF16), s5_glu_w.astype(BF16)
    expert_ids = jnp.arange(N_EXPERTS, dtype=jnp.int32)
    n_tiles = t_all // TM
    tile_ids = jnp.arange(n_tiles, dtype=jnp.int32)
    n_slots = (-(-(2 * t_all + 7 * N_EXPERTS * n_tiles) // MOE_ROWS) + N_EXPERTS) * MOE_ROWS
    x_all = jnp.concatenate([x.reshape(nb * n, d), ctx.reshape(nb * cl, d)], axis=0)

    for l in range(depth):
        mod4 = mod_all[l]
        hy, q, k, v, u = _input_proj(x_all, norm1_g[l], mod4, w_in_bf, cos, sin, nb, n, l)

        zc = _short_conv(hy, hy_conv_w[l], hy_conv_b[l], nb, n, cl)
        hy_out = None
        z1 = None
        specs = {}
        for length in (n, cl):
            specs[length] = _filter_spectrum(length, tabs[length], hy_f_w1[l], hy_f_b1[l], hy_f_freq1[l],
                                             hy_f_w2[l], hy_f_b2[l], hy_f_freq2[l], hy_f_w3[l])
        for length, off in ((n, 0), (cl, nb * n)):
            z1 = _long_conv(zc, 0, zc, 1, hy_bias[l, 0], specs[length], 0, tabs[length], length, off, nb, z1)
        for length, off in ((n, 0), (cl, nb * n)):
            hy_out = _long_conv(z1, 0, zc, 2, hy_bias[l, 1], specs[length], 1, tabs[length], length, off, nb,
                                hy_out)

        att = _attention(q, k, v, attn_sink[l], nb, n, cl)

        bmat, cmat, a_bc = _s5_matrices(ab_re[l], ab_im[l], bb_re[l], bb_im[l], s5_c_re[l], s5_c_im[l], nb)
        y_scan = _s5_scan(u, bmat, cmat, a_bc, nb, n, cl)

        x_all, h2, eidx, gates, lrank, counts = _mixer_out(
            x_all, hy_out, att, y_scan, u, s5_d[l], glu_w_bf, s5_glu_b[l], mix_norm_g[l],
            w_out_bf, mod4, norm2_g[l], router_hl, router_b, nb, n, l)

        cnt = counts.reshape(n_tiles, N_EXPERTS).astype(jnp.int32)
        c8 = (cnt + 7) // 8 * 8
        loc = jnp.sum(jnp.where(expert_ids[None, :] < expert_ids[:, None], c8[:, None, :], 0), axis=2)
        seg_off = jnp.sum(jnp.where((tile_ids[None, :] < tile_ids[:, None])[:, :, None], c8[None, :, :], 0), axis=1)
        tot8 = jnp.sum(c8, axis=0)
        padded = (tot8 + MOE_ROWS - 1) // MOE_ROWS * MOE_ROWS
        ends = jnp.sum(jnp.where(expert_ids[None, :] <= expert_ids[:, None], padded[None, :], 0), axis=1)
        pstart = ends - padded
        dst = pstart[None, :] + seg_off
        tables = (loc.reshape(-1), dst.reshape(-1), (c8 // 8).reshape(-1), jnp.sum(c8, axis=1) // 8)
        tail = jnp.concatenate([pstart + tot8, (padded - tot8) // 8])
        n_blocks = n_slots // MOE_ROWS
        block_start = jnp.arange(n_blocks, dtype=jnp.int32) * MOE_ROWS
        block_e = jnp.minimum(jnp.sum((ends[None, :] <= block_start[:, None]).astype(jnp.int32), axis=1),
                              N_EXPERTS - 1)
        n_used = (ends[-1] // MOE_ROWS).reshape(1)

        xb, spos = _dispatch(tables + (tail,), h2, eidx, lrank, loc.astype(F32).reshape(n_tiles, N_EXPERTS, 1),
                             n_slots)
        yb = _expert_ffn(block_e, n_used, xb, moe_w_gate, moe_w_up, moe_w_down, l)
        x_all = _combine(tables, x_all, spos.T, gates.T, mod4, yb, nb, n, final_g if l == depth - 1 else None)

    return x_all.reshape(nb, n, d)
```
